```python
import math
import jax, jax.numpy as jnp
from jax import lax
import numpy as np

D_MODEL = 2048
BATCH = 1
SEQ = 16384
DEPTH = 2

GRID_W = 64
BLOCK_Q = 128
ROPE_THETA = 10000.0
NORM_EPS = 1e-6
LN_EPS = 1e-5

HEAD_DIM = 128
GQA_Q_HEADS = 8
GQA_KV_HEADS = 2
DIFF_HEADS = 4
DIFF_HEAD_DIM = 128
A_Q = GQA_Q_HEADS * HEAD_DIM
A_KV = GQA_KV_HEADS * HEAD_DIM
B_QK = DIFF_HEADS * 2 * DIFF_HEAD_DIM
B_V = DIFF_HEADS * 2 * DIFF_HEAD_DIM
AB_IN = A_Q + 2 * A_KV + 2 * B_QK + B_V
MIX_WIDTH = A_Q + B_V

MLA_HEADS = 16
MLA_Q_RANK = 512
MLA_KV_RANK = 512
MLA_NOPE = 128
MLA_ROPE = 64
MLA_V = 128
MLA_IN = MLA_Q_RANK + MLA_KV_RANK + MLA_ROPE
MLA_OUT = MLA_HEADS * MLA_V

N_EXPERTS = 16
EXPERT_FF = 2048
EC_FACTOR = 2

DEEPNORM_ALPHA = (2 * DEPTH) ** 0.25
DEEPNORM_BETA = (8 * DEPTH) ** -0.25
N_EVEN = (DEPTH + 1) // 2
N_ODD = DEPTH // 2

kernel_name = "hybrid_gqa_diff_mla_ec_moe_deepnorm"


def rms_norm(x, w):
    xf = x.astype(jnp.float32)
    y = xf * lax.rsqrt(jnp.mean(xf * xf, axis=-1, keepdims=True) + NORM_EPS)
    return (y * w.astype(jnp.float32)).astype(x.dtype)


def layer_norm(x, g, b):
    xf = x.astype(jnp.float32)
    mu = jnp.mean(xf, axis=-1, keepdims=True)
    var = jnp.mean(jnp.square(xf - mu), axis=-1, keepdims=True)
    y = (xf - mu) * lax.rsqrt(var + LN_EPS) * g.astype(jnp.float32) + b.astype(jnp.float32)
    return y.astype(x.dtype)


def grid_positions(S):
    n_rows = S // GRID_W
    row = jnp.repeat(jnp.arange(n_rows, dtype=jnp.int32), GRID_W)
    col = jnp.tile(jnp.arange(GRID_W, dtype=jnp.int32), n_rows)
    return row, col


def _rotate(xs, pos):
    n = xs.shape[-1]
    freqs = ROPE_THETA ** (-jnp.arange(0, n, 2, dtype=jnp.float32) / n)
    ang = pos.astype(jnp.float32)[:, None] * freqs[None, :]
    cos = jnp.cos(ang)[:, None, :].astype(xs.dtype)
    sin = jnp.sin(ang)[:, None, :].astype(xs.dtype)
    x1, x2 = jnp.split(xs, 2, axis=-1)
    return jnp.concatenate([x1 * cos - x2 * sin, x2 * cos + x1 * sin], axis=-1)


def axial_rope(x, row, col):
    half = x.shape[-1] // 2
    return jnp.concatenate([_rotate(x[..., :half], row), _rotate(x[..., half:], col)], axis=-1)


def to_blocks(t):
    B, S = t.shape[:2]
    return jnp.moveaxis(t.reshape((B, S // BLOCK_Q, BLOCK_Q) + t.shape[2:]), 1, 0)


def from_blocks(t):
    nb, B, bq = t.shape[:3]
    return jnp.moveaxis(t, 0, 1).reshape((B, nb * bq) + t.shape[3:])


def grouped_attention(q, k, v):
    B, S, Hq, D = q.shape
    G = k.shape[2]
    R = Hq // G
    scale = D ** -0.5
    qb = to_blocks(q.reshape(B, S, G, R, D))

    def one_block(qblk):
        s = jnp.einsum('bqgrd,bkgd->bgrqk', qblk, k, preferred_element_type=jnp.float32) * scale
        p = jax.nn.softmax(s, axis=-1).astype(v.dtype)
        return jnp.einsum('bgrqk,bkgd->bqgrd', p, v)

    out = from_blocks(lax.map(one_block, qb))
    return out.reshape(B, S, -1)


def differential_attention(q1, q2, k1, k2, v, lam, slopes):
    S = q1.shape[1]
    scale = q1.shape[-1] ** -0.5
    t_k = jnp.arange(S, dtype=jnp.int32)
    starts = jnp.arange(S // BLOCK_Q, dtype=jnp.int32) * BLOCK_Q

    def one_block(args):
        q1b, q2b, start = args
        t_q = start + jnp.arange(BLOCK_Q, dtype=jnp.int32)
        dist = jnp.abs(t_q[:, None] - t_k[None, :]).astype(jnp.float32)
        bias = -slopes[:, None, None] * dist[None]
        s1 = jnp.einsum('bqhd,bkhd->bhqk', q1b, k1, preferred_element_type=jnp.float32) * scale + bias
        s2 = jnp.einsum('bqhd,bkhd->bhqk', q2b, k2, preferred_element_type=jnp.float32) * scale + bias
        attn = (jax.nn.softmax(s1, axis=-1) - lam * jax.nn.softmax(s2, axis=-1)).astype(v.dtype)
        return jnp.einsum('bhqk,bkhd->bqhd', attn, v)

    return from_blocks(lax.map(one_block, (to_blocks(q1), to_blocks(q2), starts)))


def even_mixer(x, row, col, w_in, q_norm, k_norm, lq1, lk1, lq2, lk2, subln, w_out, lam_init):
    B, S, _ = x.shape
    h = x @ w_in
    cuts = np.cumsum([A_Q, A_KV, A_KV, B_QK, B_QK]).tolist()
    aq, ak, av, bq, bk, bv = jnp.split(h, cuts, axis=-1)
    aq = axial_rope(rms_norm(aq.reshape(B, S, GQA_Q_HEADS, HEAD_DIM), q_norm), row, col)
    ak = axial_rope(rms_norm(ak.reshape(B, S, GQA_KV_HEADS, HEAD_DIM), k_norm), row, col)
    av = av.reshape(B, S, GQA_KV_HEADS, HEAD_DIM)
    a_out = grouped_attention(aq, ak, av)
    bq = bq.reshape(B, S, DIFF_HEADS, 2, DIFF_HEAD_DIM)
    bk = bk.reshape(B, S, DIFF_HEADS, 2, DIFF_HEAD_DIM)
    bv = bv.reshape(B, S, DIFF_HEADS, 2 * DIFF_HEAD_DIM)
    lam = (jnp.exp(jnp.sum(lq1.astype(jnp.float32) * lk1.astype(jnp.float32)))
           - jnp.exp(jnp.sum(lq2.astype(jnp.float32) * lk2.astype(jnp.float32))) + lam_init)
    slopes = 2.0 ** (-8.0 * jnp.arange(1, DIFF_HEADS + 1, dtype=jnp.float32) / DIFF_HEADS)
    b_out = differential_attention(bq[..., 0, :], bq[..., 1, :], bk[..., 0, :], bk[..., 1, :],
                                   bv, lam, slopes)
    b_out = (rms_norm(b_out, subln) * (1.0 - lam_init)).reshape(B, S, B_V)
    return jnp.concatenate([a_out, b_out], axis=-1) @ w_out


def odd_mixer(x, row, col, w_in, q_norm, kv_norm, w_uq, w_ukv, w_out):
    B, S, _ = x.shape
    h = x @ w_in
    c_q, c_kv, k_rope = jnp.split(h, [MLA_Q_RANK, MLA_Q_RANK + MLA_KV_RANK], axis=-1)
    q = (rms_norm(c_q, q_norm) @ w_uq).reshape(B, S, MLA_HEADS, MLA_NOPE + MLA_ROPE)
    q = jnp.concatenate([q[..., :MLA_NOPE], axial_rope(q[..., MLA_NOPE:], row, col)], axis=-1)
    kv = (rms_norm(c_kv, kv_norm) @ w_ukv).reshape(B, S, MLA_HEADS, MLA_NOPE + MLA_V)
    k_nope, v = kv[..., :MLA_NOPE], kv[..., MLA_NOPE:]
    k_rope = axial_rope(k_rope[:, :, None, :], row, col)
    k = jnp.concatenate([k_nope, jnp.broadcast_to(k_rope, (B, S, MLA_HEADS, MLA_ROPE))], axis=-1)
    return grouped_attention(q, k, v) @ w_out


def expert_choice_ffn(x, w_router, w_gate, w_up, w_down):
    B, S, _ = x.shape
    cap = EC_FACTOR * S // N_EXPERTS
    aff = jax.nn.softmax((x @ w_router).astype(jnp.float32), axis=-1)
    g, idx = lax.top_k(jnp.swapaxes(aff, 1, 2), cap)
    bidx = jnp.arange(B)[:, None, None]
    xin = x[bidx, idx]
    hid = jax.nn.silu(jnp.einsum('becd,edf->becf', xin, w_gate)) * jnp.einsum('becd,edf->becf', xin, w_up)
    y = jnp.einsum('becf,efd->becd', hid, w_down) * g[..., None].astype(x.dtype)
    return jnp.zeros_like(x).at[bidx, idx].add(y)


def setup_inputs(seed: int = 0) -> dict:
    key = jax.random.key(seed)
    ks = iter(jax.random.split(key, 40))
    f32 = jnp.float32

    def nrm(shape, scale):
        return jax.random.normal(next(ks), shape, f32) * scale

    def gain(shape):
        return 1.0 + nrm(shape, 0.02)

    return {
        "x": nrm((BATCH, SEQ, D_MODEL), 1.0),
        "ab_w_in": nrm((N_EVEN, D_MODEL, AB_IN), D_MODEL ** -0.5),
        "ab_q_norm": gain((N_EVEN, HEAD_DIM)),
        "ab_k_norm": gain((N_EVEN, HEAD_DIM)),
        "ab_lambda_q1": nrm((N_EVEN, DIFF_HEAD_DIM), 0.1),
        "ab_lambda_k1": nrm((N_EVEN, DIFF_HEAD_DIM), 0.1),
        "ab_lambda_q2": nrm((N_EVEN, DIFF_HEAD_DIM), 0.1),
        "ab_lambda_k2": nrm((N_EVEN, DIFF_HEAD_DIM), 0.1),
        "ab_subln": gain((N_EVEN, 2 * DIFF_HEAD_DIM)),
        "ab_w_out": nrm((N_EVEN, MIX_WIDTH, D_MODEL), MIX_WIDTH ** -0.5 * DEEPNORM_BETA),
        "mla_w_in": nrm((N_ODD, D_MODEL, MLA_IN), D_MODEL ** -0.5),
        "mla_q_norm": gain((N_ODD, MLA_Q_RANK)),
        "mla_kv_norm": gain((N_ODD, MLA_KV_RANK)),
        "mla_w_uq": nrm((N_ODD, MLA_Q_RANK, MLA_HEADS * (MLA_NOPE + MLA_ROPE)), MLA_Q_RANK ** -0.5),
        "mla_w_ukv": nrm((N_ODD, MLA_KV_RANK, MLA_HEADS * (MLA_NOPE + MLA_V)), MLA_KV_RANK ** -0.5),
        "mla_w_out": nrm((N_ODD, MLA_OUT, D_MODEL), MLA_OUT ** -0.5 * DEEPNORM_BETA),
        "ln_mix_g": gain((DEPTH, D_MODEL)),
        "ln_mix_b": nrm((DEPTH, D_MODEL), 0.02),
        "moe_w_router": nrm((DEPTH, D_MODEL, N_EXPERTS), D_MODEL ** -0.5),
        "moe_w_gate": nrm((DEPTH, N_EXPERTS, D_MODEL, EXPERT_FF), D_MODEL ** -0.5),
        "moe_w_up": nrm((DEPTH, N_EXPERTS, D_MODEL, EXPERT_FF), D_MODEL ** -0.5),
        "moe_w_down": nrm((DEPTH, N_EXPERTS, EXPERT_FF, D_MODEL), EXPERT_FF ** -0.5 * DEEPNORM_BETA),
        "ln_ffn_g": gain((DEPTH, D_MODEL)),
        "ln_ffn_b": nrm((DEPTH, D_MODEL), 0.02),
    }


def reference(x, ab_w_in, ab_q_norm, ab_k_norm, ab_lambda_q1, ab_lambda_k1, ab_lambda_q2,
              ab_lambda_k2, ab_subln, ab_w_out, mla_w_in, mla_q_norm, mla_kv_norm, mla_w_uq,
              mla_w_ukv, mla_w_out, ln_mix_g, ln_mix_b, moe_w_router, moe_w_gate, moe_w_up,
              moe_w_down, ln_ffn_g, ln_ffn_b):
    S = x.shape[1]
    row, col = grid_positions(S)
    for layer in range(DEPTH):
        if layer % 2 == 0:
            i = layer // 2
            lam_init = 0.8 - 0.6 * math.exp(-0.3 * layer)
            mix = even_mixer(x, row, col, ab_w_in[i], ab_q_norm[i], ab_k_norm[i],
                             ab_lambda_q1[i], ab_lambda_k1[i], ab_lambda_q2[i], ab_lambda_k2[i],
                             ab_subln[i], ab_w_out[i], lam_init)
        else:
            i = layer // 2
            mix = odd_mixer(x, row, col, mla_w_in[i], mla_q_norm[i], mla_kv_norm[i],
                            mla_w_uq[i], mla_w_ukv[i], mla_w_out[i])
        x = layer_norm(DEEPNORM_ALPHA * x + mix, ln_mix_g[layer], ln_mix_b[layer])
        ffn = expert_choice_ffn(x, moe_w_router[layer], moe_w_gate[layer], moe_w_up[layer],
                                moe_w_down[layer])
        x = layer_norm(DEEPNORM_ALPHA * x + ffn, ln_ffn_g[layer], ln_ffn_b[layer])
    return x
```

```python
import functools
import math

import jax
import jax.numpy as jnp
from jax import lax
from jax.experimental import pallas as pl
from jax.experimental.pallas import tpu as pltpu

F32 = jnp.float32
BF16 = jnp.bfloat16
I32 = jnp.int32

D_MODEL = 2048
DEPTH = 2
GRID_W = 64
ROPE_THETA = 10000.0
NORM_EPS = 1e-6
LN_EPS = 1e-5

HEAD_DIM = 128
GQA_Q_HEADS = 8
GQA_KV_HEADS = 2
DIFF_HEADS = 4
A_Q = GQA_Q_HEADS * HEAD_DIM
A_KV = GQA_KV_HEADS * HEAD_DIM
B_QK = DIFF_HEADS * 2 * HEAD_DIM
B_V = DIFF_HEADS * 2 * HEAD_DIM
AB_HEADED = A_Q + 2 * A_KV + 2 * B_QK

MLA_HEADS = 16
MLA_Q_RANK = 512
MLA_KV_RANK = 512
MLA_NOPE = 128
MLA_ROPE = 64
MLA_V = 128
MLA_QK_PAD = 256

N_EXPERTS = 16
EXPERT_FF = 2048
EC_FACTOR = 2

DEEPNORM_ALPHA = (2 * DEPTH) ** 0.25

LANES = 128
BF16_ROWS = 16
GATHER_WIN = LANES + BF16_ROWS
VMEM_LIMIT = 56 * 1024 * 1024


def _cparams(sem, vmem=VMEM_LIMIT):
    return pltpu.CompilerParams(dimension_semantics=sem, vmem_limit_bytes=vmem)


def _dot(a, b):
    return jnp.dot(a, b, preferred_element_type=F32)


def _dot_nt(a, b):
    return lax.dot_general(a, b, (((1,), (1,)), ((), ())), preferred_element_type=F32)


def _rope(y, cos, sin, half):
    n = y.shape[-1]
    lane = lax.broadcasted_iota(I32, y.shape, 1)
    up = pltpu.roll(y, n - half, 1)
    dn = pltpu.roll(y, half, 1)
    partner = jnp.where((lane % (2 * half)) < half, up, dn)
    return y * cos + partner * sin


def _rms(y, w):
    return y * lax.rsqrt(jnp.mean(y * y, axis=-1, keepdims=True) + NORM_EPS) * w


def _layer_norm(y, g, b):
    mu = jnp.mean(y, axis=-1, keepdims=True)
    d = y - mu
    var = jnp.mean(d * d, axis=-1, keepdims=True)
    return d * lax.rsqrt(var + LN_EPS) * g + b


def _rope_tables(S, half, width):
    t = jnp.arange(S, dtype=I32)
    row = (t // GRID_W).astype(F32)[:, None]
    col = (t % GRID_W).astype(F32)[:, None]
    n = 2 * half
    freqs = ROPE_THETA ** (-jnp.arange(0, n, 2, dtype=F32) / n)[None, :]
    parts_c, parts_s = [], []
    for pos in (row, col):
        ang = pos * freqs
        c, s = jnp.cos(ang), jnp.sin(ang)
        parts_c += [c, c]
        parts_s += [-s, s]
    pad = width - 4 * half
    if pad:
        parts_c.append(jnp.ones((S, pad), F32))
        parts_s.append(jnp.zeros((S, pad), F32))
    return jnp.concatenate(parts_c, axis=1), jnp.concatenate(parts_s, axis=1)


def _proj0_kernel(x_ref, w_ref, cos_ref, sin_ref, qn_ref, kn_ref, o_ref, *, tn):
    j = pl.program_id(1)
    h = _dot(x_ref[...], w_ref[...])
    nsub = tn // HEAD_DIM
    scale = HEAD_DIM ** -0.5
    q_tiles = A_Q // tn
    k_tile = A_Q // tn
    bq_lo = (A_Q + 2 * A_KV) // tn
    bq_hi = bq_lo + B_QK // tn

    def normed(c, w):
        y = _rms(h[:, c * HEAD_DIM:(c + 1) * HEAD_DIM], w)
        return _rope(y, cos_ref[...], sin_ref[...], HEAD_DIM // 4)

    @pl.when(j < q_tiles)
    def _():
        for c in range(nsub):
            o_ref[c] = (normed(c, qn_ref[...]) * scale).astype(o_ref.dtype)

    @pl.when(j == k_tile)
    def _():
        for c in range(nsub):
            if c < GQA_KV_HEADS:
                o_ref[c] = normed(c, kn_ref[...]).astype(o_ref.dtype)
            else:
                o_ref[c] = h[:, c * HEAD_DIM:(c + 1) * HEAD_DIM].astype(o_ref.dtype)

    @pl.when(j > k_tile)
    def _():
        mul = jnp.where((j >= bq_lo) & (j < bq_hi), scale, 1.0).astype(F32)
        for c in range(nsub):
            o_ref[c] = (h[:, c * HEAD_DIM:(c + 1) * HEAD_DIM] * mul).astype(o_ref.dtype)


def _proj0(x_bf, w_bf, cos, sin, qn, kn, *, tm=512, tn=512):
    S, K = x_bf.shape
    N = w_bf.shape[1]
    assert A_Q % tn == 0 and tn == 2 * A_KV and N % tn == 0
    return pl.pallas_call(
        functools.partial(_proj0_kernel, tn=tn),
        grid=(S // tm, N // tn),
        in_specs=[
            pl.BlockSpec((tm, K), lambda i, j: (i, 0)),
            pl.BlockSpec((K, tn), lambda i, j: (0, j)),
            pl.BlockSpec((tm, HEAD_DIM), lambda i, j: (i, 0)),
            pl.BlockSpec((tm, HEAD_DIM), lambda i, j: (i, 0)),
            pl.BlockSpec((1, HEAD_DIM), lambda i, j: (0, 0)),
            pl.BlockSpec((1, HEAD_DIM), lambda i, j: (0, 0)),
        ],
        out_specs=pl.BlockSpec((tn // HEAD_DIM, tm, HEAD_DIM), lambda i, j: (j, i, 0)),
        out_shape=jax.ShapeDtypeStruct((N // HEAD_DIM, S, HEAD_DIM), BF16),
        compiler_params=_cparams(("parallel", "parallel")),
        name="proj0",
    )(x_bf, w_bf, cos, sin, qn, kn)


def _mm_kernel(x_ref, w_ref, o_ref):
    o_ref[...] = _dot(x_ref[...], w_ref[...]).astype(o_ref.dtype)


def _mm(x_bf, w_bf, *, tm=512, tn=512, name="mm"):
    S, K = x_bf.shape
    N = w_bf.shape[1]
    return pl.pallas_call(
        _mm_kernel,
        grid=(S // tm, N // tn),
        in_specs=[pl.BlockSpec((tm, K), lambda i, j: (i, 0)),
                  pl.BlockSpec((K, tn), lambda i, j: (0, j))],
        out_specs=pl.BlockSpec((tm, tn), lambda i, j: (i, j)),
        out_shape=jax.ShapeDtypeStruct((S, N), BF16),
        compiler_params=_cparams(("parallel", "parallel")),
        name=name,
    )(x_bf, w_bf)


def _flash_kernel(q_ref, k_ref, v_ref, o_ref, m_sc, l_sc, acc_sc, *, kc):
    R, bq, D = q_ref.shape
    S = k_ref.shape[1]
    Dv = v_ref.shape[2]
    q = q_ref[...].reshape(R * bq, D)
    m_sc[...] = jnp.full(m_sc.shape, -jnp.inf, F32)
    l_sc[...] = jnp.zeros(l_sc.shape, F32)
    acc_sc[...] = jnp.zeros(acc_sc.shape, F32)

    def body(c, carry):
        off = pl.multiple_of(c * kc, kc)
        k = k_ref[0, pl.ds(off, kc), :]
        v = v_ref[0, pl.ds(off, kc), :]
        s = _dot_nt(q, k)
        m_prev = m_sc[...]
        m_new = jnp.maximum(m_prev, jnp.max(s, axis=1, keepdims=True))
        alpha = jnp.exp(m_prev - m_new)
        p = jnp.exp(s - m_new)
        l_sc[...] = alpha * l_sc[...] + jnp.sum(p, axis=1, keepdims=True)
        acc_sc[...] = alpha * acc_sc[...] + _dot(p.astype(BF16), v)
        m_sc[...] = m_new
        return carry

    lax.fori_loop(0, S // kc, body, 0)
    out = acc_sc[...] / l_sc[...]
    for r in range(R):
        o_ref[:, r * Dv:(r + 1) * Dv] = out[r * bq:(r + 1) * bq].astype(o_ref.dtype)


def _flash(q_arr, k_arr, v_arr, *, n_kv, rep, q_tile0, k_tile0, v_tile0, bq, kc, name):
    _, S, D = q_arr.shape
    Dv = v_arr.shape[2]
    kc = min(kc, S)
    bq = min(bq, S)
    assert q_tile0 % rep == 0
    M = rep * bq
    return pl.pallas_call(
        functools.partial(_flash_kernel, kc=kc),
        grid=(n_kv, S // bq),
        in_specs=[
            pl.BlockSpec((rep, bq, D), lambda g, i: (q_tile0 // rep + g, i, 0)),
            pl.BlockSpec((1, S, D), lambda g, i: (k_tile0 + g, 0, 0)),
            pl.BlockSpec((1, S, Dv), lambda g, i: (v_tile0 + g, 0, 0)),
        ],
        out_specs=pl.BlockSpec((bq, rep * Dv), lambda g, i: (i, g)),
        out_shape=jax.ShapeDtypeStruct((S, n_kv * rep * Dv), BF16),
        scratch_shapes=[pltpu.VMEM((M, 1), F32), pltpu.VMEM((M, 1), F32), pltpu.VMEM((M, Dv), F32)],
        compiler_params=_cparams(("parallel", "parallel")),
        name=name,
    )(q_arr, k_arr, v_arr)


def _diff_kernel(q_ref, k_ref, v_ref, lq1_ref, lk1_ref, lq2_ref, lk2_ref, subln_ref, o_ref,
                 m_sc, l_sc, acc_sc, *, kc, lam_init):
    _, bq, D = q_ref.shape
    S = k_ref.shape[1]
    h = pl.program_id(0)
    q0 = pl.program_id(1) * bq
    slope = jnp.float32(2.0 ** (-8.0 * DIFF_HEADS / DIFF_HEADS))
    for hh in range(DIFF_HEADS - 1):
        slope = jnp.where(h == hh, jnp.float32(2.0 ** (-8.0 * (hh + 1) / DIFF_HEADS)), slope)
    rel = (lax.broadcasted_iota(I32, (bq, kc), 0) - lax.broadcasted_iota(I32, (bq, kc), 1))
    m_sc[...] = jnp.full(m_sc.shape, -jnp.inf, F32)
    l_sc[...] = jnp.zeros(l_sc.shape, F32)
    acc_sc[...] = jnp.zeros(acc_sc.shape, F32)

    def body(c, carry):
        off = pl.multiple_of(c * kc, kc)
        v = v_ref[pl.ds(off, kc), :]
        bias = jnp.abs(rel + (q0 - off)).astype(F32) * (-slope)
        for j in range(2):
            k = k_ref[j, pl.ds(off, kc), :]
            s = _dot_nt(q_ref[j], k) + bias
            m_prev = m_sc[j]
            m_new = jnp.maximum(m_prev, jnp.max(s, axis=1, keepdims=True))
            alpha = jnp.exp(m_prev - m_new)
            p = jnp.exp(s - m_new)
            l_sc[j] = alpha * l_sc[j] + jnp.sum(p, axis=1, keepdims=True)
            acc_sc[j] = alpha * acc_sc[j] + _dot(p.astype(BF16), v)
            m_sc[j] = m_new
        return carry

    lax.fori_loop(0, S // kc, body, 0)
    lam = (jnp.exp(jnp.sum(lq1_ref[...] * lk1_ref[...], axis=1, keepdims=True))
           - jnp.exp(jnp.sum(lq2_ref[...] * lk2_ref[...], axis=1, keepdims=True)) + lam_init)
    out = acc_sc[0] / l_sc[0] - lam * (acc_sc[1] / l_sc[1])
    o_ref[...] = (_rms(out, subln_ref[...]) * (1.0 - lam_init)).astype(o_ref.dtype)


def _diff_attention(hq, bv, lq1, lk1, lq2, lk2, subln, *, q_tile0, k_tile0, lam_init, bq=512, kc=512):
    _, S, D = hq.shape
    kc = min(kc, S)
    bq = min(bq, S)
    vec = pl.BlockSpec((1, D), lambda h, i: (0, 0))
    return pl.pallas_call(
        functools.partial(_diff_kernel, kc=kc, lam_init=lam_init),
        grid=(DIFF_HEADS, S // bq),
        in_specs=[
            pl.BlockSpec((2, bq, D), lambda h, i: (q_tile0 // 2 + h, i, 0)),
            pl.BlockSpec((2, S, D), lambda h, i: (k_tile0 // 2 + h, 0, 0)),
            pl.BlockSpec((S, 2 * D), lambda h, i: (0, h)),
            vec, vec, vec, vec,
            pl.BlockSpec((1, 2 * D), lambda h, i: (0, 0)),
        ],
        out_specs=pl.BlockSpec((bq, 2 * D), lambda h, i: (i, h)),
        out_shape=jax.ShapeDtypeStruct((S, DIFF_HEADS * 2 * D), BF16),
        scratch_shapes=[pltpu.VMEM((2, bq, 1), F32), pltpu.VMEM((2, bq, 1), F32),
                        pltpu.VMEM((2, bq, 2 * D), F32)],
        compiler_params=_cparams(("parallel", "parallel")),
        name="diff_attention",
    )(hq, hq, bv, lq1, lk1, lq2, lk2, subln)


def _wout_kernel(*refs, n_in):
    a_refs = refs[:n_in]
    w_refs = refs[n_in:2 * n_in]
    x_ref, g_ref, b_ref, wr_ref, xo_ref, xb_ref, lg_ref = refs[2 * n_in:]
    mix = _dot(a_refs[0][...], w_refs[0][...])
    for a, w in zip(a_refs[1:], w_refs[1:]):
        mix = mix + _dot(a[...], w[...])
    y = _layer_norm(DEEPNORM_ALPHA * x_ref[...] + mix, g_ref[...], b_ref[...])
    xo_ref[...] = y
    xb_ref[...] = y.astype(BF16)
    lg_ref[...] = lax.dot_general(wr_ref[...], y, (((1,), (1,)), ((), ())),
                                  precision=lax.Precision.HIGHEST, preferred_element_type=F32)


def _wout_ln_router(a_list, w_list, x, g, b, wr_t, *, tm=256):
    S, D = x.shape
    n_in = len(a_list)
    E = wr_t.shape[0]
    in_specs = ([pl.BlockSpec((tm, a.shape[1]), lambda i: (i, 0)) for a in a_list]
                + [pl.BlockSpec(w.shape, lambda i: (0, 0)) for w in w_list]
                + [pl.BlockSpec((tm, D), lambda i: (i, 0)),
                   pl.BlockSpec((1, D), lambda i: (0, 0)),
                   pl.BlockSpec((1, D), lambda i: (0, 0)),
                   pl.BlockSpec((E, D), lambda i: (0, 0))])
    return pl.pallas_call(
        functools.partial(_wout_kernel, n_in=n_in),
        grid=(S // tm,),
        in_specs=in_specs,
        out_specs=[pl.BlockSpec((tm, D), lambda i: (i, 0)),
                   pl.BlockSpec((tm, D), lambda i: (i, 0)),
                   pl.BlockSpec((E, tm), lambda i: (0, i))],
        out_shape=[jax.ShapeDtypeStruct((S, D), F32),
                   jax.ShapeDtypeStruct((S, D), BF16),
                   jax.ShapeDtypeStruct((E, S), F32)],
        compiler_params=_cparams(("parallel",)),
        name="wout_ln_router",
    )(*a_list, *w_list, x, g, b, wr_t)


def _select_kernel(lg_ref, pos_ref, gate_ref, off_ref, cnt_ref, *, E, nb, cap):
    lg = lg_ref[...].reshape(E, nb, LANES)
    mx = jnp.max(lg, axis=0, keepdims=True)
    ex = jnp.exp(lg - mx)
    aff = ex / jnp.sum(ex, axis=0, keepdims=True)
    bits = lax.bitcast_convert_type(aff, I32)

    def count(msk):
        c = jnp.sum(msk.astype(F32), axis=1, keepdims=True)
        return jnp.sum(c, axis=2, keepdims=True)

    def search(i, thr):
        cand = thr | lax.shift_left(jnp.int32(1), 30 - i)
        return jnp.where(count(bits >= cand) >= cap, cand, thr)

    thr = lax.fori_loop(0, 31, search, jnp.zeros((E, 1, 1), I32))
    gt = bits > thr
    eq = bits == thr
    need = cap - count(gt)

    col = lax.broadcasted_iota(I32, (LANES, LANES), 1)
    rw = lax.broadcasted_iota(I32, (LANES, LANES), 0)
    upper = (rw < col).astype(BF16)
    ones = jnp.ones((LANES, LANES), BF16)
    lower = (lax.broadcasted_iota(I32, (nb, nb), 1) < lax.broadcasted_iota(I32, (nb, nb), 0)).astype(BF16)

    def prefix(msk):
        m2 = jnp.where(msk, 1.0, 0.0).reshape(E * nb, LANES).astype(BF16)
        within = _dot(m2, upper).reshape(E, nb, LANES)
        tot = _dot(m2, ones).reshape(E, nb, LANES)
        offs = jnp.stack([_dot(lower, tot[e].astype(BF16)) for e in range(E)], axis=0)
        return within + offs, offs, tot

    eq_rank, _, _ = prefix(eq)
    sel = gt | (eq & (eq_rank < need))
    pos, offs, tot = prefix(sel)
    pos_ref[...] = jnp.where(sel, pos, -1.0).astype(I32).reshape(E * nb, LANES)
    gate_ref[...] = jnp.where(sel, aff, 0.0).reshape(E * nb, LANES)
    off_ref[...] = offs.astype(I32).reshape(E * nb, LANES)
    cnt_ref[...] = tot.astype(I32).reshape(E * nb, LANES)


def _select(logits_t, cap):
    E, S = logits_t.shape
    nb = S // LANES
    shp = (E * nb, LANES)
    full = pl.BlockSpec(shp, lambda: (0, 0))
    return pl.pallas_call(
        functools.partial(_select_kernel, E=E, nb=nb, cap=cap),
        in_specs=[full],
        out_specs=[full, full, full, full],
        out_shape=[jax.ShapeDtypeStruct(shp, I32), jax.ShapeDtypeStruct(shp, F32),
                   jax.ShapeDtypeStruct(shp, I32), jax.ShapeDtypeStruct(shp, I32)],
        compiler_params=pltpu.CompilerParams(vmem_limit_bytes=VMEM_LIMIT),
        name="select",
    )(logits_t.reshape(shp))


def _gather_kernel(off_s, cnt_s, x_ref, pos_ref, gate_ref, xe_ref, gs_ref, *, cap):
    e = pl.program_id(0)
    tb = pl.program_id(1)

    @pl.when(tb == 0)
    def _():
        xe_ref[...] = jnp.zeros(xe_ref.shape, xe_ref.dtype)
        gs_ref[...] = jnp.zeros(gs_ref.shape, gs_ref.dtype)

    @pl.when(cnt_s[e, tb] > 0)
    def _():
        base = jnp.minimum((off_s[e, tb] // BF16_ROWS) * BF16_ROWS, cap - GATHER_WIN)
        base = pl.multiple_of(base, BF16_ROWS)
        slot = base + lax.broadcasted_iota(I32, (GATHER_WIN, LANES), 0)
        hit = pos_ref[0] == slot
        rows = _dot(hit.astype(BF16), x_ref[...])
        win = pl.ds(base, GATHER_WIN)
        xe_ref[0, win, :] = xe_ref[0, win, :] + rows.astype(xe_ref.dtype)
        g = jnp.sum(jnp.where(hit, gate_ref[0], 0.0), axis=1, keepdims=True)
        gs_ref[0, win, :] = gs_ref[0, win, :] + jnp.broadcast_to(g, (GATHER_WIN, LANES))


def _gather(off_s, cnt_s, x_bf, pos, gate, cap):
    S, D = x_bf.shape
    E, nb = off_s.shape
    assert cap >= GATHER_WIN and cap % BF16_ROWS == 0
    pos3 = pos.reshape(E * nb, 1, LANES)
    gate3 = gate.reshape(E * nb, 1, LANES)
    return pl.pallas_call(
        functools.partial(_gather_kernel, cap=cap),
        grid_spec=pltpu.PrefetchScalarGridSpec(
            num_scalar_prefetch=2,
            grid=(E, nb),
            in_specs=[
                pl.BlockSpec((LANES, D), lambda e, t, o, c: (t, 0)),
                pl.BlockSpec((1, 1, LANES), lambda e, t, o, c: (e * nb + t, 0, 0)),
                pl.BlockSpec((1, 1, LANES), lambda e, t, o, c: (e * nb + t, 0, 0)),
            ],
            out_specs=[pl.BlockSpec((1, cap, D), lambda e, t, o, c: (e, 0, 0)),
                       pl.BlockSpec((1, cap, LANES), lambda e, t, o, c: (e, 0, 0))],
        ),
        out_shape=[jax.ShapeDtypeStruct((E, cap, D), BF16),
                   jax.ShapeDtypeStruct((E, cap, LANES), F32)],
        compiler_params=_cparams(("parallel", "arbitrary")),
        name="gather",
    )(off_s, cnt_s, x_bf, pos3, gate3)


def _ffn_kernel(xe_ref, wg_ref, wu_ref, wd_ref, gs_ref, ye_ref, acc_sc, *, sub):
    fc = pl.program_id(2)
    rows = xe_ref.shape[1]
    wg = wg_ref[...].astype(BF16)
    wu = wu_ref[...].astype(BF16)
    wd = wd_ref[...].astype(BF16)
    for r in range(rows // sub):
        sl = pl.ds(r * sub, sub)
        xt = xe_ref[0, sl, :]
        hg = _dot(xt, wg)
        hu = _dot(xt, wu)
        hid = (hg * jax.nn.sigmoid(hg) * hu).astype(BF16)
        part = _dot(hid, wd)

        @pl.when(fc == 0)
        def _():
            acc_sc[sl, :] = part

        @pl.when(fc > 0)
        def _():
            acc_sc[sl, :] = acc_sc[sl, :] + part

    @pl.when(fc == pl.num_programs(2) - 1)
    def _():
        ye_ref[0] = (acc_sc[...] * gs_ref[0][:, :1]).astype(ye_ref.dtype)


def _ffn(xe, gs, w_gate, w_up, w_down, layer, *, halves=2, fcw=256):
    E, cap, D = xe.shape
    F = w_gate.shape[-1]
    rows = cap // halves
    sub = min(256, rows)
    return pl.pallas_call(
        functools.partial(_ffn_kernel, sub=sub),
        grid=(E, halves, F // fcw),
        in_specs=[
            pl.BlockSpec((1, rows, D), lambda e, t, f: (e, t, 0)),
            pl.BlockSpec((None, None, D, fcw), lambda e, t, f: (layer, e, 0, f)),
            pl.BlockSpec((None, None, D, fcw), lambda e, t, f: (layer, e, 0, f)),
            pl.BlockSpec((None, None, fcw, D), lambda e, t, f: (layer, e, f, 0)),
            pl.BlockSpec((1, rows, LANES), lambda e, t, f: (e, t, 0)),
        ],
        out_specs=pl.BlockSpec((1, rows, D), lambda e, t, f: (e, t, 0)),
        out_shape=jax.ShapeDtypeStruct((E, cap, D), BF16),
        scratch_shapes=[pltpu.VMEM((rows, D), F32)],
        compiler_params=_cparams(("parallel", "parallel", "arbitrary")),
        name="expert_ffn",
    )(xe, w_gate, w_up, w_down, gs)


def _combine_kernel(blk_s, cnt_s, ya_ref, yb_ref, post_ref, x_ref, g_ref, b_ref, xo_ref, xb_ref, acc_sc):
    tb = pl.program_id(0)
    e = pl.program_id(1)
    E = pl.num_programs(1)

    @pl.when(e == 0)
    def _():
        acc_sc[...] = jnp.zeros(acc_sc.shape, F32)

    @pl.when(cnt_s[e, tb] > 0)
    def _():
        pt = post_ref[...]
        lane = lax.broadcasted_iota(I32, pt.shape, 1)
        pcol = jnp.sum(jnp.where(lane == e, pt, 0).astype(F32), axis=1, keepdims=True).astype(I32)
        slot = blk_s[e, tb] * LANES + lax.broadcasted_iota(I32, (LANES, 2 * LANES), 1)
        hit = (pcol == slot).astype(BF16)
        y2 = jnp.concatenate([ya_ref[0], yb_ref[0]], axis=0)
        acc_sc[...] = acc_sc[...] + _dot(hit, y2)

    @pl.when(e == E - 1)
    def _():
        y = _layer_norm(DEEPNORM_ALPHA * x_ref[...] + acc_sc[...], g_ref[...], b_ref[...])
        xo_ref[...] = y
        xb_ref[...] = y.astype(BF16)


def _combine_ln(blk_s, cnt_s, ye, pos_t, x, g, b):
    S, D = x.shape
    E, cap, _ = ye.shape
    nb = S // LANES
    last = cap // LANES - 1
    return pl.pallas_call(
        _combine_kernel,
        grid_spec=pltpu.PrefetchScalarGridSpec(
            num_scalar_prefetch=2,
            grid=(nb, E),
            in_specs=[
                pl.BlockSpec((1, LANES, D), lambda t, e, bs, c: (e, bs[e, t], 0)),
                pl.BlockSpec((1, LANES, D), lambda t, e, bs, c: (e, jnp.minimum(bs[e, t] + 1, last), 0)),
                pl.BlockSpec((LANES, E), lambda t, e, bs, c: (t, 0)),
                pl.BlockSpec((LANES, D), lambda t, e, bs, c: (t, 0)),
                pl.BlockSpec((1, D), lambda t, e, bs, c: (0, 0)),
                pl.BlockSpec((1, D), lambda t, e, bs, c: (0, 0)),
            ],
            out_specs=[pl.BlockSpec((LANES, D), lambda t, e, bs, c: (t, 0)),
                       pl.BlockSpec((LANES, D), lambda t, e, bs, c: (t, 0))],
            scratch_shapes=[pltpu.VMEM((LANES, D), F32)],
        ),
        out_shape=[jax.ShapeDtypeStruct((S, D), F32), jax.ShapeDtypeStruct((S, D), BF16)],
        compiler_params=_cparams(("parallel", "arbitrary")),
        name="combine_ln",
    )(blk_s, cnt_s, ye, ye, pos_t, x, g, b)


def _moe(x, x_bf, logits_t, w_gate, w_up, w_down, layer, g, b):
    S, D = x.shape
    E = logits_t.shape[0]
    nb = S // LANES
    cap = EC_FACTOR * S // E
    pos, gate, off, cnt = _select(logits_t, cap)
    off_s = off[:, 0].reshape(E, nb)
    cnt_s = cnt[:, 0].reshape(E, nb)
    xe, gs = _gather(off_s, cnt_s, x_bf, pos, gate, cap)
    ye = _ffn(xe, gs, w_gate, w_up, w_down, layer)
    pos_t = pos.reshape(E, S).T
    return _combine_ln(off_s // LANES, cnt_s, ye, pos_t, x, g, b)


def _mla_in_kernel(x_ref, w_ref, cos_ref, sin_ref, qn_ref, kvn_ref, cq_ref, ckv_ref, kr_ref):
    h = _dot(x_ref[...], w_ref[...])
    cq_ref[...] = _rms(h[:, :MLA_Q_RANK], qn_ref[...]).astype(BF16)
    ckv_ref[...] = _rms(h[:, MLA_Q_RANK:MLA_Q_RANK + MLA_KV_RANK], kvn_ref[...]).astype(BF16)
    kr = h[:, MLA_Q_RANK + MLA_KV_RANK:]
    kr_ref[...] = _rope(kr, cos_ref[...], sin_ref[...], MLA_ROPE // 4).astype(BF16)


def _mla_in(x_bf, w_bf, cos, sin, qn, kvn, *, tm=512):
    S, K = x_bf.shape
    N = w_bf.shape[1]
    row = lambda i: (i, 0)
    fixed = lambda i: (0, 0)
    return pl.pallas_call(
        _mla_in_kernel,
        grid=(S // tm,),
        in_specs=[pl.BlockSpec((tm, K), row), pl.BlockSpec((K, N), fixed),
                  pl.BlockSpec((tm, LANES), row), pl.BlockSpec((tm, LANES), row),
                  pl.BlockSpec((1, MLA_Q_RANK), fixed), pl.BlockSpec((1, MLA_KV_RANK), fixed)],
        out_specs=[pl.BlockSpec((tm, MLA_Q_RANK), row), pl.BlockSpec((tm, MLA_KV_RANK), row),
                   pl.BlockSpec((tm, LANES), row)],
        out_shape=[jax.ShapeDtypeStruct((S, MLA_Q_RANK), BF16),
                   jax.ShapeDtypeStruct((S, MLA_KV_RANK), BF16),
                   jax.ShapeDtypeStruct((S, LANES), BF16)],
        compiler_params=_cparams(("parallel",)),
        name="mla_in",
    )(x_bf, w_bf, cos, sin, qn, kvn)


def _mla_q_kernel(c_ref, w_ref, cos_ref, sin_ref, o_ref, *, heads):
    h = _dot(c_ref[...], w_ref[...])
    scale = (MLA_NOPE + MLA_ROPE) ** -0.5
    for c in range(heads):
        base = c * MLA_QK_PAD
        o_ref[c, :, :MLA_NOPE] = (h[:, base:base + MLA_NOPE] * scale).astype(BF16)
        rp = _rope(h[:, base + MLA_NOPE:base + MLA_QK_PAD], cos_ref[...], sin_ref[...], MLA_ROPE // 4)
        o_ref[c, :, MLA_NOPE:] = (rp * scale).astype(BF16)


def _mla_q(cq, w_bf, cos, sin, *, tm=512, heads=2):
    S, K = cq.shape
    tn = heads * MLA_QK_PAD
    return pl.pallas_call(
        functools.partial(_mla_q_kernel, heads=heads),
        grid=(S // tm, MLA_HEADS // heads),
        in_specs=[pl.BlockSpec((tm, K), lambda i, j: (i, 0)),
                  pl.BlockSpec((K, tn), lambda i, j: (0, j)),
                  pl.BlockSpec((tm, LANES), lambda i, j: (i, 0)),
                  pl.BlockSpec((tm, LANES), lambda i, j: (i, 0))],
        out_specs=pl.BlockSpec((heads, tm, MLA_QK_PAD), lambda i, j: (j, i, 0)),
        out_shape=jax.ShapeDtypeStruct((MLA_HEADS, S, MLA_QK_PAD), BF16),
        compiler_params=_cparams(("parallel", "parallel")),
        name="mla_q",
    )(cq, w_bf, cos, sin)


def _mla_kv_kernel(c_ref, wk_ref, wv_ref, kr_ref, k_ref, v_ref, *, heads):
    c = c_ref[...]
    kn = _dot(c, wk_ref[...])
    vv = _dot(c, wv_ref[...])
    for hh in range(heads):
        k_ref[hh, :, :MLA_NOPE] = kn[:, hh * MLA_NOPE:(hh + 1) * MLA_NOPE].astype(BF16)
        k_ref[hh, :, MLA_NOPE:] = kr_ref[...]
        v_ref[hh] = vv[:, hh * MLA_V:(hh + 1) * MLA_V].astype(BF16)


def _mla_kv(ckv, wk_bf, wv_bf, kr, *, tm=512, heads=4):
    S, K = ckv.shape
    return pl.pallas_call(
        functools.partial(_mla_kv_kernel, heads=heads),
        grid=(S // tm, MLA_HEADS // heads),
        in_specs=[pl.BlockSpec((tm, K), lambda i, j: (i, 0)),
                  pl.BlockSpec((K, heads * MLA_NOPE), lambda i, j: (0, j)),
                  pl.BlockSpec((K, heads * MLA_V), lambda i, j: (0, j)),
                  pl.BlockSpec((tm, LANES), lambda i, j: (i, 0))],
        out_specs=[pl.BlockSpec((heads, tm, MLA_QK_PAD), lambda i, j: (j, i, 0)),
                   pl.BlockSpec((heads, tm, MLA_V), lambda i, j: (j, i, 0))],
        out_shape=[jax.ShapeDtypeStruct((MLA_HEADS, S, MLA_QK_PAD), BF16),
                   jax.ShapeDtypeStruct((MLA_HEADS, S, MLA_V), BF16)],
        compiler_params=_cparams(("parallel", "parallel")),
        name="mla_kv",
    )(ckv, wk_bf, wv_bf, kr)


def kernel(x, ab_w_in, ab_q_norm, ab_k_norm, ab_lambda_q1, ab_lambda_k1, ab_lambda_q2, ab_lambda_k2, ab_subln, ab_w_out, mla_w_in, mla_q_norm, mla_kv_norm, mla_w_uq, mla_w_ukv, mla_w_out, ln_mix_g, ln_mix_b, moe_w_router, moe_w_gate, moe_w_up, moe_w_down, ln_ffn_g, ln_ffn_b):
    B, S, D = x.shape
    assert B == 1 and D == D_MODEL and S % 512 == 0
    xf = x.reshape(S, D)
    x_bf = xf.astype(BF16)
    cos_a, sin_a = _rope_tables(S, HEAD_DIM // 4, HEAD_DIM)
    cos_c, sin_c = _rope_tables(S, MLA_ROPE // 4, LANES)
    row2 = lambda v: v.reshape(1, -1)

    lam_init = 0.8 - 0.6 * math.exp(-0.3 * 0)
    w_in = ab_w_in[0].astype(BF16)
    hq = _proj0(x_bf, w_in[:, :AB_HEADED], cos_a, sin_a, row2(ab_q_norm[0]), row2(ab_k_norm[0]))
    bv = _mm(x_bf, w_in[:, AB_HEADED:], name="proj0_bv")
    t_ak = A_Q // HEAD_DIM
    t_av = t_ak + GQA_KV_HEADS
    t_bq = t_av + GQA_KV_HEADS
    t_bk = t_bq + 2 * DIFF_HEADS
    a_out = _flash(hq, hq, hq, n_kv=GQA_KV_HEADS, rep=GQA_Q_HEADS // GQA_KV_HEADS,
                   q_tile0=0, k_tile0=t_ak, v_tile0=t_av, bq=256, kc=512, name="gqa_attention")
    b_out = _diff_attention(hq, bv, row2(ab_lambda_q1[0]), row2(ab_lambda_k1[0]),
                            row2(ab_lambda_q2[0]), row2(ab_lambda_k2[0]), row2(ab_subln[0]),
                            q_tile0=t_bq, k_tile0=t_bk, lam_init=lam_init)
    w_out = ab_w_out[0].astype(BF16)
    xf, x_bf, logits_t = _wout_ln_router(
        [a_out, b_out], [w_out[:A_Q], w_out[A_Q:]], xf, row2(ln_mix_g[0]), row2(ln_mix_b[0]),
        moe_w_router[0].T)
    xf, x_bf = _moe(xf, x_bf, logits_t, moe_w_gate, moe_w_up, moe_w_down, 0,
                    row2(ln_ffn_g[0]), row2(ln_ffn_b[0]))

    w1 = jnp.pad(mla_w_in[0], ((0, 0), (0, LANES - MLA_ROPE))).astype(BF16)
    cq, ckv, kr = _mla_in(x_bf, w1, cos_c, sin_c, row2(mla_q_norm[0]), row2(mla_kv_norm[0]))
    w_uq = mla_w_uq[0].reshape(MLA_Q_RANK, MLA_HEADS, MLA_NOPE + MLA_ROPE)
    w_uq = jnp.pad(w_uq, ((0, 0), (0, 0), (0, MLA_QK_PAD - MLA_NOPE - MLA_ROPE)))
    w_uq = w_uq.reshape(MLA_Q_RANK, MLA_HEADS * MLA_QK_PAD).astype(BF16)
    w_ukv = mla_w_ukv[0].reshape(MLA_KV_RANK, MLA_HEADS, MLA_NOPE + MLA_V)
    w_uk = w_ukv[:, :, :MLA_NOPE].reshape(MLA_KV_RANK, MLA_HEADS * MLA_NOPE).astype(BF16)
    w_uv = w_ukv[:, :, MLA_NOPE:].reshape(MLA_KV_RANK, MLA_HEADS * MLA_V).astype(BF16)
    q_pad = _mla_q(cq, w_uq, cos_c, sin_c)
    k_pad, v_h = _mla_kv(ckv, w_uk, w_uv, kr)
    c_out = _flash(q_pad, k_pad, v_h, n_kv=MLA_HEADS, rep=1, q_tile0=0, k_tile0=0, v_tile0=0,
                   bq=512, kc=512, name="mla_attention")
    xf, x_bf, logits_t = _wout_ln_router(
        [c_out], [mla_w_out[0].astype(BF16)], xf, row2(ln_mix_g[1]), row2(ln_mix_b[1]),
        moe_w_router[1].T)
    xf, x_bf = _moe(xf, x_bf, logits_t, moe_w_gate, moe_w_up, moe_w_down, 1,
                    row2(ln_ffn_g[1]), row2(ln_ffn_b[1]))
    return xf.reshape(B, S, D)
```

```python
import functools
import math

import jax
import jax.numpy as jnp
from jax import lax
from jax.experimental import pallas as pl
from jax.experimental.pallas import tpu as pltpu

F32 = jnp.float32
BF16 = jnp.bfloat16
I32 = jnp.int32

D_MODEL = 2048
DEPTH = 2
GRID_W = 64
ROPE_THETA = 10000.0
NORM_EPS = 1e-6
LN_EPS = 1e-5

HEAD_DIM = 128
GQA_Q_HEADS = 8
GQA_KV_HEADS = 2
DIFF_HEADS = 4
A_Q = GQA_Q_HEADS * HEAD_DIM
A_KV = GQA_KV_HEADS * HEAD_DIM
B_QK = DIFF_HEADS * 2 * HEAD_DIM
B_V = DIFF_HEADS * 2 * HEAD_DIM
AB_HEADED = A_Q + 2 * A_KV + 2 * B_QK

MLA_HEADS = 16
MLA_Q_RANK = 512
MLA_KV_RANK = 512
MLA_NOPE = 128
MLA_ROPE = 64
MLA_V = 128
MLA_QK_PAD = 256

N_EXPERTS = 16
EXPERT_FF = 2048
EC_FACTOR = 2

DEEPNORM_ALPHA = (2 * DEPTH) ** 0.25
LOG2E = math.log2(math.e)

LANES = 128
BF16_ROWS = 16
GATHER_WIN = LANES + BF16_ROWS
GATHER_SMALL = 32
VMEM_LIMIT = 56 * 1024 * 1024


def _cparams(sem, vmem=VMEM_LIMIT):
    return pltpu.CompilerParams(dimension_semantics=sem, vmem_limit_bytes=vmem)


def _dot(a, b):
    return jnp.dot(a, b, preferred_element_type=F32)


def _dot_nt(a, b):
    return lax.dot_general(a, b, (((1,), (1,)), ((), ())), preferred_element_type=F32)


def _rope(y, cos, sin, half):
    n = y.shape[-1]
    lane = lax.broadcasted_iota(I32, y.shape, 1)
    up = pltpu.roll(y, n - half, 1)
    dn = pltpu.roll(y, half, 1)
    partner = jnp.where((lane % (2 * half)) < half, up, dn)
    return y * cos + partner * sin


def _rms(y, w):
    return y * lax.rsqrt(jnp.mean(y * y, axis=-1, keepdims=True) + NORM_EPS) * w


def _layer_norm(y, g, b):
    mu = jnp.mean(y, axis=-1, keepdims=True)
    d = y - mu
    var = jnp.mean(d * d, axis=-1, keepdims=True)
    return d * lax.rsqrt(var + LN_EPS) * g + b


def _rope_tables(S, half, width):
    t = jnp.arange(S, dtype=I32)
    row = (t // GRID_W).astype(F32)[:, None]
    col = (t % GRID_W).astype(F32)[:, None]
    n = 2 * half
    freqs = ROPE_THETA ** (-jnp.arange(0, n, 2, dtype=F32) / n)[None, :]
    parts_c, parts_s = [], []
    for pos in (row, col):
        ang = pos * freqs
        c, s = jnp.cos(ang), jnp.sin(ang)
        parts_c += [c, c]
        parts_s += [-s, s]
    pad = width - 4 * half
    if pad:
        parts_c.append(jnp.ones((S, pad), F32))
        parts_s.append(jnp.zeros((S, pad), F32))
    return jnp.concatenate(parts_c, axis=1), jnp.concatenate(parts_s, axis=1)


def _proj0_kernel(x_ref, w_ref, cos_ref, sin_ref, qn_ref, kn_ref, o_ref, *, tn):
    j = pl.program_id(1)
    h = _dot(x_ref[...], w_ref[...])
    nsub = tn // HEAD_DIM
    scale = HEAD_DIM ** -0.5 * LOG2E
    k_tile = A_Q // tn
    bq_lo = (A_Q + A_KV) // tn
    bq_hi = bq_lo + B_QK // tn

    def normed(c, w):
        y = _rms(h[:, c * HEAD_DIM:(c + 1) * HEAD_DIM], w)
        return _rope(y, cos_ref[...], sin_ref[...], HEAD_DIM // 4)

    @pl.when(j < k_tile)
    def _():
        for c in range(nsub):
            o_ref[c] = (normed(c, qn_ref[...]) * scale).astype(o_ref.dtype)

    @pl.when(j == k_tile)
    def _():
        for c in range(nsub):
            o_ref[c] = normed(c, kn_ref[...]).astype(o_ref.dtype)

    @pl.when(j > k_tile)
    def _():
        mul = jnp.where((j >= bq_lo) & (j < bq_hi), scale, 1.0).astype(F32)
        for c in range(nsub):
            o_ref[c] = (h[:, c * HEAD_DIM:(c + 1) * HEAD_DIM] * mul).astype(o_ref.dtype)


def _proj0(x_bf, w_bf, cos, sin, qn, kn, *, tm=512, tn=256):
    S, K = x_bf.shape
    N = w_bf.shape[1]
    assert A_Q % tn == 0 and tn == A_KV and N % tn == 0
    return pl.pallas_call(
        functools.partial(_proj0_kernel, tn=tn),
        grid=(S // tm, N // tn),
        in_specs=[
            pl.BlockSpec((tm, K), lambda i, j: (i, 0)),
            pl.BlockSpec((K, tn), lambda i, j: (0, j)),
            pl.BlockSpec((tm, HEAD_DIM), lambda i, j: (i, 0)),
            pl.BlockSpec((tm, HEAD_DIM), lambda i, j: (i, 0)),
            pl.BlockSpec((1, HEAD_DIM), lambda i, j: (0, 0)),
            pl.BlockSpec((1, HEAD_DIM), lambda i, j: (0, 0)),
        ],
        out_specs=pl.BlockSpec((tn // HEAD_DIM, tm, HEAD_DIM), lambda i, j: (j, i, 0)),
        out_shape=jax.ShapeDtypeStruct((N // HEAD_DIM, S, HEAD_DIM), BF16),
        compiler_params=_cparams(("parallel", "parallel")),
        name="proj0",
    )(x_bf, w_bf, cos, sin, qn, kn)


def _mm_nt_kernel(wt_ref, x_ref, o_ref):
    o_ref[...] = _dot_nt(wt_ref[...], x_ref[...]).astype(o_ref.dtype)


def _mm_nt(wt_bf, x_bf, *, tm=512, tn=256, name="mm_nt"):
    S, K = x_bf.shape
    N = wt_bf.shape[0]
    return pl.pallas_call(
        _mm_nt_kernel,
        grid=(S // tm, N // tn),
        in_specs=[pl.BlockSpec((tn, K), lambda i, j: (j, 0)),
                  pl.BlockSpec((tm, K), lambda i, j: (i, 0))],
        out_specs=pl.BlockSpec((tn, tm), lambda i, j: (j, i)),
        out_shape=jax.ShapeDtypeStruct((N, S), BF16),
        compiler_params=_cparams(("parallel", "parallel")),
        name=name,
    )(wt_bf, x_bf)


def _online_softmax_pv(s, vt, m_ref, l_ref, acc_ref):
    m_prev = m_ref[...]
    m_new = jnp.maximum(m_prev, jnp.max(s, axis=0, keepdims=True))
    alpha = jnp.exp2(m_prev - m_new)
    p = jnp.exp2(s - m_new)
    l_ref[...] = alpha * l_ref[...] + jnp.sum(p, axis=0, keepdims=True)
    acc_ref[...] = alpha * acc_ref[...] + _dot(vt, p.astype(BF16))
    m_ref[...] = m_new


def _flash_kernel(q_ref, k_ref, vt_ref, o_ref, qt_sc, s_sc, m_sc, l_sc, acc_sc, *, kc):
    R, bq, D = q_ref.shape
    S = k_ref.shape[1]
    Dv = vt_ref.shape[0]
    n = S // kc
    assert n % 2 == 0
    qt_sc[...] = q_ref[...].reshape(R * bq, D).astype(F32).T.astype(BF16)
    m_sc[...] = jnp.full(m_sc.shape, -jnp.inf, F32)
    l_sc[...] = jnp.zeros(l_sc.shape, F32)
    acc_sc[...] = jnp.zeros(acc_sc.shape, F32)

    def scores(c):
        off = pl.multiple_of(c * kc, kc)
        return _dot(k_ref[0, pl.ds(off, kc), :], qt_sc[...])

    def values_t(c):
        return vt_ref[:, pl.ds(pl.multiple_of(c * kc, kc), kc)]

    s_sc[0] = scores(0)

    def body(i, carry):
        c = 2 * i
        s_sc[1] = scores(c + 1)
        _online_softmax_pv(s_sc[0], values_t(c), m_sc, l_sc, acc_sc)
        s_sc[0] = scores(jnp.minimum(c + 2, n - 1))
        _online_softmax_pv(s_sc[1], values_t(c + 1), m_sc, l_sc, acc_sc)
        return carry

    lax.fori_loop(0, n // 2, body, 0)
    out = (acc_sc[...] / l_sc[...]).T
    for r in range(R):
        o_ref[:, r * Dv:(r + 1) * Dv] = out[r * bq:(r + 1) * bq].astype(o_ref.dtype)


def _flash(q_arr, k_arr, vt_arr, *, n_kv, rep, q_tile0, k_tile0, dv, bq, kc, name):
    _, S, D = q_arr.shape
    kc = min(kc, S)
    bq = min(bq, S)
    assert q_tile0 % rep == 0
    N = rep * bq
    return pl.pallas_call(
        functools.partial(_flash_kernel, kc=kc),
        grid=(n_kv, S // bq),
        in_specs=[
            pl.BlockSpec((rep, bq, D), lambda g, i: (q_tile0 // rep + g, i, 0)),
            pl.BlockSpec((1, S, D), lambda g, i: (k_tile0 + g, 0, 0)),
            pl.BlockSpec((dv, S), lambda g, i: (g, 0)),
        ],
        out_specs=pl.BlockSpec((bq, rep * dv), lambda g, i: (i, g)),
        out_shape=jax.ShapeDtypeStruct((S, n_kv * rep * dv), BF16),
        scratch_shapes=[pltpu.VMEM((D, N), BF16), pltpu.VMEM((2, kc, N), F32),
                        pltpu.VMEM((1, N), F32), pltpu.VMEM((1, N), F32), pltpu.VMEM((dv, N), F32)],
        compiler_params=_cparams(("parallel", "parallel")),
        name=name,
    )(q_arr, k_arr, vt_arr)


def _diff_kernel(q_ref, k_ref, vt_ref, lq1_ref, lk1_ref, lq2_ref, lk2_ref, subln_ref, o_ref,
                 qt_sc, s_sc, m_sc, l_sc, acc_sc, *, kc, lam_init):
    _, bq, D = q_ref.shape
    S = k_ref.shape[1]
    h = pl.program_id(0)
    q0 = pl.program_id(1) * bq
    slope = jnp.float32(2.0 ** (-8.0 * DIFF_HEADS / DIFF_HEADS))
    for hh in range(DIFF_HEADS - 1):
        slope = jnp.where(h == hh, jnp.float32(2.0 ** (-8.0 * (hh + 1) / DIFF_HEADS)), slope)
    neg_slope = -(slope * LOG2E)
    rel = (lax.broadcasted_iota(I32, (kc, bq), 0) - lax.broadcasted_iota(I32, (kc, bq), 1)).astype(F32)
    for j in range(2):
        qt_sc[j] = q_ref[j].astype(F32).T.astype(BF16)
    m_sc[...] = jnp.full(m_sc.shape, -jnp.inf, F32)
    l_sc[...] = jnp.zeros(l_sc.shape, F32)
    acc_sc[...] = jnp.zeros(acc_sc.shape, F32)

    n = S // kc

    def scores(j, c):
        off = pl.multiple_of(c * kc, kc)
        return _dot(k_ref[j, pl.ds(off, kc), :], qt_sc[j])

    s_sc[0] = scores(0, 0)

    def body(c, carry):
        off = pl.multiple_of(c * kc, kc)
        vt = vt_ref[:, pl.ds(off, kc)]
        s_sc[1] = scores(1, c)
        bias = jnp.abs(rel + (off - q0).astype(F32)) * neg_slope
        _online_softmax_pv(s_sc[0] + bias, vt, m_sc.at[0], l_sc.at[0], acc_sc.at[0])
        s_sc[0] = scores(0, jnp.minimum(c + 1, n - 1))
        _online_softmax_pv(s_sc[1] + bias, vt, m_sc.at[1], l_sc.at[1], acc_sc.at[1])
        return carry

    lax.fori_loop(0, n, body, 0)
    lam = (jnp.exp(jnp.sum(lq1_ref[...] * lk1_ref[...], axis=1, keepdims=True))
           - jnp.exp(jnp.sum(lq2_ref[...] * lk2_ref[...], axis=1, keepdims=True)) + lam_init)
    out = (acc_sc[0] / l_sc[0] - lam * (acc_sc[1] / l_sc[1])).T
    o_ref[...] = (_rms(out, subln_ref[...]) * (1.0 - lam_init)).astype(o_ref.dtype)


def _diff_attention(hq, vt_arr, lq1, lk1, lq2, lk2, subln, *, q_tile0, k_tile0, vt_blk0, lam_init,
                    bq=512, kc=512):
    _, S, D = hq.shape
    kc = min(kc, S)
    bq = min(bq, S)
    vec = pl.BlockSpec((1, D), lambda h, i: (0, 0))
    return pl.pallas_call(
        functools.partial(_diff_kernel, kc=kc, lam_init=lam_init),
        grid=(DIFF_HEADS, S // bq),
        in_specs=[
            pl.BlockSpec((2, bq, D), lambda h, i: (q_tile0 // 2 + h, i, 0)),
            pl.BlockSpec((2, S, D), lambda h, i: (k_tile0 // 2 + h, 0, 0)),
            pl.BlockSpec((2 * D, S), lambda h, i: (vt_blk0 + h, 0)),
            vec, vec, vec, vec,
            pl.BlockSpec((1, 2 * D), lambda h, i: (0, 0)),
        ],
        out_specs=pl.BlockSpec((bq, 2 * D), lambda h, i: (i, h)),
        out_shape=jax.ShapeDtypeStruct((S, DIFF_HEADS * 2 * D), BF16),
        scratch_shapes=[pltpu.VMEM((2, D, bq), BF16), pltpu.VMEM((2, kc, bq), F32),
                        pltpu.VMEM((2, 1, bq), F32),
                        pltpu.VMEM((2, 1, bq), F32), pltpu.VMEM((2, 2 * D, bq), F32)],
        compiler_params=_cparams(("parallel", "parallel")),
        name="diff_attention",
    )(hq, hq, vt_arr, lq1, lk1, lq2, lk2, subln)


def _wout_kernel(*refs, n_in):
    a_refs = refs[:n_in]
    w_refs = refs[n_in:2 * n_in]
    x_ref, g_ref, b_ref, wr_ref, xo_ref, xb_ref, lg_ref = refs[2 * n_in:]
    mix = _dot(a_refs[0][...], w_refs[0][...])
    for a, w in zip(a_refs[1:], w_refs[1:]):
        mix = mix + _dot(a[...], w[...])
    y = _layer_norm(DEEPNORM_ALPHA * x_ref[...] + mix, g_ref[...], b_ref[...])
    xo_ref[...] = y
    xb_ref[...] = y.astype(BF16)
    lg_ref[...] = lax.dot_general(wr_ref[...], y, (((1,), (1,)), ((), ())),
                                  precision=lax.Precision.HIGHEST, preferred_element_type=F32)


def _wout_ln_router(a_list, w_list, x, g, b, wr_t, *, tm=256):
    S, D = x.shape
    n_in = len(a_list)
    E = wr_t.shape[0]
    in_specs = ([pl.BlockSpec((tm, a.shape[1]), lambda i: (i, 0)) for a in a_list]
                + [pl.BlockSpec(w.shape, lambda i: (0, 0)) for w in w_list]
                + [pl.BlockSpec((tm, D), lambda i: (i, 0)),
                   pl.BlockSpec((1, D), lambda i: (0, 0)),
                   pl.BlockSpec((1, D), lambda i: (0, 0)),
                   pl.BlockSpec((E, D), lambda i: (0, 0))])
    return pl.pallas_call(
        functools.partial(_wout_kernel, n_in=n_in),
        grid=(S // tm,),
        in_specs=in_specs,
        out_specs=[pl.BlockSpec((tm, D), lambda i: (i, 0)),
                   pl.BlockSpec((tm, D), lambda i: (i, 0)),
                   pl.BlockSpec((E, tm), lambda i: (0, i))],
        out_shape=[jax.ShapeDtypeStruct((S, D), F32),
                   jax.ShapeDtypeStruct((S, D), BF16),
                   jax.ShapeDtypeStruct((E, S), F32)],
        compiler_params=_cparams(("parallel",)),
        name="wout_ln_router",
    )(*a_list, *w_list, x, g, b, wr_t)


def _select_kernel(lg_ref, pos_ref, gate_ref, off_ref, cnt_ref, *, E, nb, cap):
    lg = lg_ref[...].reshape(E, nb, LANES)
    mx = jnp.max(lg, axis=0, keepdims=True)
    ex = jnp.exp(lg - mx)
    aff = ex / jnp.sum(ex, axis=0, keepdims=True)
    bits = lax.bitcast_convert_type(aff, I32)

    def count(msk):
        c = jnp.sum(msk.astype(F32), axis=1, keepdims=True)
        return jnp.sum(c, axis=2, keepdims=True)

    def search(i, thr):
        cand = thr | lax.shift_left(jnp.int32(1), 30 - i)
        return jnp.where(count(bits >= cand) >= cap, cand, thr)

    thr = lax.fori_loop(0, 31, search, jnp.zeros((E, 1, 1), I32))
    gt = bits > thr
    eq = bits == thr
    need = cap - count(gt)

    col = lax.broadcasted_iota(I32, (LANES, LANES), 1)
    rw = lax.broadcasted_iota(I32, (LANES, LANES), 0)
    upper = (rw < col).astype(BF16)
    ones = jnp.ones((LANES, LANES), BF16)
    lower = (lax.broadcasted_iota(I32, (nb, nb), 1) < lax.broadcasted_iota(I32, (nb, nb), 0)).astype(BF16)

    def prefix(msk):
        m2 = jnp.where(msk, 1.0, 0.0).reshape(E * nb, LANES).astype(BF16)
        within = _dot(m2, upper).reshape(E, nb, LANES)
        tot = _dot(m2, ones).reshape(E, nb, LANES)
        offs = jnp.stack([_dot(lower, tot[e].astype(BF16)) for e in range(E)], axis=0)
        return within + offs, offs, tot

    eq_rank, _, _ = prefix(eq)
    sel = gt | (eq & (eq_rank < need))
    pos, offs, tot = prefix(sel)
    pos_ref[...] = jnp.where(sel, pos, -1.0).astype(I32).reshape(E * nb, LANES)
    gate_ref[...] = jnp.where(sel, aff, 0.0).reshape(E * nb, LANES)
    off_ref[...] = offs.astype(I32).reshape(E * nb, LANES)
    cnt_ref[...] = tot.astype(I32).reshape(E * nb, LANES)


def _select(logits_t, cap):
    E, S = logits_t.shape
    nb = S // LANES
    shp = (E * nb, LANES)
    full = pl.BlockSpec(shp, lambda: (0, 0))
    return pl.pallas_call(
        functools.partial(_select_kernel, E=E, nb=nb, cap=cap),
        in_specs=[full],
        out_specs=[full, full, full, full],
        out_shape=[jax.ShapeDtypeStruct(shp, I32), jax.ShapeDtypeStruct(shp, F32),
                   jax.ShapeDtypeStruct(shp, I32), jax.ShapeDtypeStruct(shp, I32)],
        compiler_params=pltpu.CompilerParams(vmem_limit_bytes=VMEM_LIMIT),
        name="select",
    )(logits_t.reshape(shp))


def _gather_kernel(off_s, cnt_s, x_ref, pos_ref, gate_ref, xe_ref, gs_ref, *, cap, tbs):
    e = pl.program_id(0)
    t = pl.program_id(1)

    @pl.when(t == 0)
    def _():
        xe_ref[...] = jnp.zeros(xe_ref.shape, xe_ref.dtype)
        gs_ref[...] = jnp.zeros(gs_ref.shape, gs_ref.dtype)

    def place(u, tb, width):
        base = jnp.minimum((off_s[e, tb] // BF16_ROWS) * BF16_ROWS, cap - width)
        base = pl.multiple_of(base, BF16_ROWS)
        slot = base + lax.broadcasted_iota(I32, (width, LANES), 0)
        hit = pos_ref[u] == slot
        rows = _dot(hit.astype(BF16), x_ref[pl.ds(u * LANES, LANES), :])
        win = pl.ds(base, width)
        xe_ref[0, win, :] = xe_ref[0, win, :] + rows.astype(xe_ref.dtype)
        g = jnp.sum(jnp.where(hit, gate_ref[u], 0.0), axis=1, keepdims=True)
        gs_ref[0, win, :] = gs_ref[0, win, :] + jnp.broadcast_to(g, (width, LANES))

    for u in range(tbs):
        tb = t * tbs + u
        cnt = cnt_s[e, tb]

        @pl.when((cnt > 0) & (cnt <= GATHER_SMALL))
        def _():
            place(u, tb, GATHER_SMALL + BF16_ROWS)

        @pl.when(cnt > GATHER_SMALL)
        def _():
            place(u, tb, GATHER_WIN)


def _gather(off_s, cnt_s, x_bf, pos, gate, cap, *, tbs=4):
    S, D = x_bf.shape
    E, nb = off_s.shape
    assert cap >= GATHER_WIN and cap % BF16_ROWS == 0 and nb % tbs == 0
    pos3 = pos.reshape(E * nb, 1, LANES)
    gate3 = gate.reshape(E * nb, 1, LANES)
    nt = nb // tbs
    return pl.pallas_call(
        functools.partial(_gather_kernel, cap=cap, tbs=tbs),
        grid_spec=pltpu.PrefetchScalarGridSpec(
            num_scalar_prefetch=2,
            grid=(E, nt),
            in_specs=[
                pl.BlockSpec((tbs * LANES, D), lambda e, t, o, c: (t, 0)),
                pl.BlockSpec((tbs, 1, LANES), lambda e, t, o, c: (e * nt + t, 0, 0)),
                pl.BlockSpec((tbs, 1, LANES), lambda e, t, o, c: (e * nt + t, 0, 0)),
            ],
            out_specs=[pl.BlockSpec((1, cap, D), lambda e, t, o, c: (e, 0, 0)),
                       pl.BlockSpec((1, cap, LANES), lambda e, t, o, c: (e, 0, 0))],
        ),
        out_shape=[jax.ShapeDtypeStruct((E, cap, D), BF16),
                   jax.ShapeDtypeStruct((E, cap, LANES), F32)],
        compiler_params=_cparams(("parallel", "arbitrary")),
        name="gather",
    )(off_s, cnt_s, x_bf, pos3, gate3)


def _ffn_kernel(xe_ref, wg_ref, wu_ref, wd_ref, gs_ref, ye_ref, acc_sc, *, sub):
    fc = pl.program_id(2)
    rows = xe_ref.shape[1]
    @pl.when(fc == 0)
    def _():
        acc_sc[...] = jnp.zeros(acc_sc.shape, F32)

    wg = wg_ref[...].astype(BF16)
    wu = wu_ref[...].astype(BF16)
    wd = wd_ref[...].astype(BF16)
    for r in range(rows // sub):
        sl = pl.ds(r * sub, sub)
        xt = xe_ref[0, sl, :]
        hg = _dot(xt, wg)
        hu = _dot(xt, wu)
        hid = (hg * jax.nn.sigmoid(hg) * hu).astype(BF16)
        acc_sc[sl, :] = acc_sc[sl, :] + _dot(hid, wd)

    @pl.when(fc == pl.num_programs(2) - 1)
    def _():
        ye_ref[0] = (acc_sc[...] * gs_ref[0][:, :1]).astype(ye_ref.dtype)


def _ffn(xe, gs, w_gate, w_up, w_down, layer, *, halves=2, fcw=256):
    E, cap, D = xe.shape
    F = w_gate.shape[-1]
    rows = cap // halves
    sub = min(256, rows)
    return pl.pallas_call(
        functools.partial(_ffn_kernel, sub=sub),
        grid=(E, halves, F // fcw),
        in_specs=[
            pl.BlockSpec((1, rows, D), lambda e, t, f: (e, t, 0)),
            pl.BlockSpec((None, None, D, fcw), lambda e, t, f: (layer, e, 0, f)),
            pl.BlockSpec((None, None, D, fcw), lambda e, t, f: (layer, e, 0, f)),
            pl.BlockSpec((None, None, fcw, D), lambda e, t, f: (layer, e, f, 0)),
            pl.BlockSpec((1, rows, LANES), lambda e, t, f: (e, t, 0)),
        ],
        out_specs=pl.BlockSpec((1, rows, D), lambda e, t, f: (e, t, 0)),
        out_shape=jax.ShapeDtypeStruct((E, cap, D), BF16),
        scratch_shapes=[pltpu.VMEM((rows, D), F32)],
        compiler_params=_cparams(("parallel", "parallel", "arbitrary")),
        name="expert_ffn",
    )(xe, w_gate, w_up, w_down, gs)


def _combine_kernel(win_s, *refs, eg):
    y_refs = refs[:eg]
    post_ref, x_ref, g_ref, b_ref, xo_ref, xb_ref, acc_sc = refs[eg:]
    tb = pl.program_id(0)
    grp = pl.program_id(1)

    @pl.when(grp == 0)
    def _():
        acc_sc[...] = jnp.zeros(acc_sc.shape, F32)

    pt = post_ref[...].astype(F32)
    lane = lax.broadcasted_iota(I32, pt.shape, 1)
    col = lax.broadcasted_iota(I32, (LANES, GATHER_WIN), 1)
    total = None
    for k in range(eg):
        e = grp * eg + k
        pcol = jnp.sum(jnp.where(lane == e, pt, 0.0), axis=1, keepdims=True).astype(I32)
        hit = ((pcol - win_s[e, tb]) == col).astype(BF16)
        part = _dot(hit, y_refs[k][0])
        total = part if total is None else total + part
    acc_sc[...] = acc_sc[...] + total

    @pl.when(grp == pl.num_programs(1) - 1)
    def _():
        y = _layer_norm(DEEPNORM_ALPHA * x_ref[...] + acc_sc[...], g_ref[...], b_ref[...])
        xo_ref[...] = y
        xb_ref[...] = y.astype(BF16)


def _combine_ln(win_s, ye, pos_t, x, g, b, *, eg=8):
    S, D = x.shape
    E, cap, _ = ye.shape
    nb = S // LANES
    assert E % eg == 0

    def y_spec(k):
        return pl.BlockSpec((pl.Element(1), pl.Element(GATHER_WIN), pl.Element(D)),
                            lambda t, gr, ws: (gr * eg + k, (ws[gr * eg + k, t] // BF16_ROWS) * BF16_ROWS, 0))

    row = lambda t, gr, ws: (t, 0)
    fixed = lambda t, gr, ws: (0, 0)
    return pl.pallas_call(
        functools.partial(_combine_kernel, eg=eg),
        grid_spec=pltpu.PrefetchScalarGridSpec(
            num_scalar_prefetch=1,
            grid=(nb, E // eg),
            in_specs=([y_spec(k) for k in range(eg)]
                      + [pl.BlockSpec((LANES, E), row), pl.BlockSpec((LANES, D), row),
                         pl.BlockSpec((1, D), fixed), pl.BlockSpec((1, D), fixed)]),
            out_specs=[pl.BlockSpec((LANES, D), row), pl.BlockSpec((LANES, D), row)],
            scratch_shapes=[pltpu.VMEM((LANES, D), F32)],
        ),
        out_shape=[jax.ShapeDtypeStruct((S, D), F32), jax.ShapeDtypeStruct((S, D), BF16)],
        compiler_params=_cparams(("parallel", "arbitrary")),
        name="combine_ln",
    )(win_s, *([ye] * eg), pos_t, x, g, b)


def _moe(x, x_bf, logits_t, w_gate, w_up, w_down, layer, g, b):
    S, D = x.shape
    E = logits_t.shape[0]
    nb = S // LANES
    cap = EC_FACTOR * S // E
    pos, gate, off, cnt = _select(logits_t, cap)
    off_s = off[:, 0].reshape(E, nb)
    cnt_s = cnt[:, 0].reshape(E, nb)
    xe, gs = _gather(off_s, cnt_s, x_bf, pos, gate, cap)
    ye = _ffn(xe, gs, w_gate, w_up, w_down, layer)
    pos_t = pos.reshape(E, S).T
    win_s = jnp.minimum((off_s // BF16_ROWS) * BF16_ROWS, cap - GATHER_WIN)
    return _combine_ln(win_s, ye, pos_t, x, g, b)


def _mla_in_kernel(x_ref, w_ref, cos_ref, sin_ref, qn_ref, kvn_ref, cq_ref, ckv_ref, kr_ref):
    h = _dot(x_ref[...], w_ref[...])
    cq_ref[...] = _rms(h[:, :MLA_Q_RANK], qn_ref[...]).astype(BF16)
    ckv_ref[...] = _rms(h[:, MLA_Q_RANK:MLA_Q_RANK + MLA_KV_RANK], kvn_ref[...]).astype(BF16)
    kr = h[:, MLA_Q_RANK + MLA_KV_RANK:]
    kr_ref[...] = _rope(kr, cos_ref[...], sin_ref[...], MLA_ROPE // 4).astype(BF16)


def _mla_in(x_bf, w_bf, cos, sin, qn, kvn, *, tm=512):
    S, K = x_bf.shape
    N = w_bf.shape[1]
    row = lambda i: (i, 0)
    fixed = lambda i: (0, 0)
    return pl.pallas_call(
        _mla_in_kernel,
        grid=(S // tm,),
        in_specs=[pl.BlockSpec((tm, K), row), pl.BlockSpec((K, N), fixed),
                  pl.BlockSpec((tm, LANES), row), pl.BlockSpec((tm, LANES), row),
                  pl.BlockSpec((1, MLA_Q_RANK), fixed), pl.BlockSpec((1, MLA_KV_RANK), fixed)],
        out_specs=[pl.BlockSpec((tm, MLA_Q_RANK), row), pl.BlockSpec((tm, MLA_KV_RANK), row),
                   pl.BlockSpec((tm, LANES), row)],
        out_shape=[jax.ShapeDtypeStruct((S, MLA_Q_RANK), BF16),
                   jax.ShapeDtypeStruct((S, MLA_KV_RANK), BF16),
                   jax.ShapeDtypeStruct((S, LANES), BF16)],
        compiler_params=_cparams(("parallel",)),
        name="mla_in",
    )(x_bf, w_bf, cos, sin, qn, kvn)


def _mla_q_kernel(c_ref, w_ref, cos_ref, sin_ref, o_ref, *, heads):
    h = _dot(c_ref[...], w_ref[...])
    scale = (MLA_NOPE + MLA_ROPE) ** -0.5 * LOG2E
    for c in range(heads):
        base = c * MLA_QK_PAD
        o_ref[c, :, :MLA_NOPE] = (h[:, base:base + MLA_NOPE] * scale).astype(BF16)
        rp = _rope(h[:, base + MLA_NOPE:base + MLA_QK_PAD], cos_ref[...], sin_ref[...], MLA_ROPE // 4)
        o_ref[c, :, MLA_NOPE:] = (rp * scale).astype(BF16)


def _mla_q(cq, w_bf, cos, sin, *, tm=512, heads=2):
    S, K = cq.shape
    tn = heads * MLA_QK_PAD
    return pl.pallas_call(
        functools.partial(_mla_q_kernel, heads=heads),
        grid=(S // tm, MLA_HEADS // heads),
        in_specs=[pl.BlockSpec((tm, K), lambda i, j: (i, 0)),
                  pl.BlockSpec((K, tn), lambda i, j: (0, j)),
                  pl.BlockSpec((tm, LANES), lambda i, j: (i, 0)),
                  pl.BlockSpec((tm, LANES), lambda i, j: (i, 0))],
        out_specs=pl.BlockSpec((heads, tm, MLA_QK_PAD), lambda i, j: (j, i, 0)),
        out_shape=jax.ShapeDtypeStruct((MLA_HEADS, S, MLA_QK_PAD), BF16),
        compiler_params=_cparams(("parallel", "parallel")),
        name="mla_q",
    )(cq, w_bf, cos, sin)


def _mla_k_kernel(c_ref, wk_ref, kr_ref, k_ref, *, heads):
    kn = _dot(c_ref[...], wk_ref[...])
    for hh in range(heads):
        k_ref[hh, :, :MLA_NOPE] = kn[:, hh * MLA_NOPE:(hh + 1) * MLA_NOPE].astype(BF16)
        k_ref[hh, :, MLA_NOPE:] = kr_ref[...]


def _mla_k(ckv, wk_bf, kr, *, tm=512, heads=4):
    S, K = ckv.shape
    return pl.pallas_call(
        functools.partial(_mla_k_kernel, heads=heads),
        grid=(S // tm, MLA_HEADS // heads),
        in_specs=[pl.BlockSpec((tm, K), lambda i, j: (i, 0)),
                  pl.BlockSpec((K, heads * MLA_NOPE), lambda i, j: (0, j)),
                  pl.BlockSpec((tm, LANES), lambda i, j: (i, 0))],
        out_specs=pl.BlockSpec((heads, tm, MLA_QK_PAD), lambda i, j: (j, i, 0)),
        out_shape=jax.ShapeDtypeStruct((MLA_HEADS, S, MLA_QK_PAD), BF16),
        compiler_params=_cparams(("parallel", "parallel")),
        name="mla_k",
    )(ckv, wk_bf, kr)


def kernel(x, ab_w_in, ab_q_norm, ab_k_norm, ab_lambda_q1, ab_lambda_k1, ab_lambda_q2, ab_lambda_k2, ab_subln, ab_w_out, mla_w_in, mla_q_norm, mla_kv_norm, mla_w_uq, mla_w_ukv, mla_w_out, ln_mix_g, ln_mix_b, moe_w_router, moe_w_gate, moe_w_up, moe_w_down, ln_ffn_g, ln_ffn_b):
    B, S, D = x.shape
    assert B == 1 and D == D_MODEL and S % 512 == 0
    xf = x.reshape(S, D)
    x_bf = xf.astype(BF16)
    cos_a, sin_a = _rope_tables(S, HEAD_DIM // 4, HEAD_DIM)
    cos_c, sin_c = _rope_tables(S, MLA_ROPE // 4, LANES)
    row2 = lambda v: v.reshape(1, -1)

    lam_init = 0.8 - 0.6 * math.exp(-0.3 * 0)
    w_in = ab_w_in[0].astype(BF16)
    c_av = A_Q + A_KV
    c_bq = c_av + A_KV
    c_bv = c_bq + 2 * B_QK
    w_qk = jnp.concatenate([w_in[:, :c_av], w_in[:, c_bq:c_bv]], axis=1)
    w_v_t = jnp.concatenate([w_in[:, c_av:c_bq], w_in[:, c_bv:]], axis=1).T
    hq = _proj0(x_bf, w_qk, cos_a, sin_a, row2(ab_q_norm[0]), row2(ab_k_norm[0]))
    vt = _mm_nt(w_v_t, x_bf, name="proj0_vt")
    t_ak = A_Q // HEAD_DIM
    t_bq = t_ak + GQA_KV_HEADS
    t_bk = t_bq + 2 * DIFF_HEADS
    a_out = _flash(hq, hq, vt, n_kv=GQA_KV_HEADS, rep=GQA_Q_HEADS // GQA_KV_HEADS,
                   q_tile0=0, k_tile0=t_ak, dv=HEAD_DIM, bq=256, kc=512, name="gqa_attention")
    b_out = _diff_attention(hq, vt, row2(ab_lambda_q1[0]), row2(ab_lambda_k1[0]),
                            row2(ab_lambda_q2[0]), row2(ab_lambda_k2[0]), row2(ab_subln[0]),
                            q_tile0=t_bq, k_tile0=t_bk, vt_blk0=A_KV // (2 * HEAD_DIM),
                            lam_init=lam_init)
    w_out = ab_w_out[0].astype(BF16)
    xf, x_bf, logits_t = _wout_ln_router(
        [a_out, b_out], [w_out[:A_Q], w_out[A_Q:]], xf, row2(ln_mix_g[0]), row2(ln_mix_b[0]),
        moe_w_router[0].T)
    xf, x_bf = _moe(xf, x_bf, logits_t, moe_w_gate, moe_w_up, moe_w_down, 0,
                    row2(ln_ffn_g[0]), row2(ln_ffn_b[0]))

    w1 = jnp.pad(mla_w_in[0], ((0, 0), (0, LANES - MLA_ROPE))).astype(BF16)
    cq, ckv, kr = _mla_in(x_bf, w1, cos_c, sin_c, row2(mla_q_norm[0]), row2(mla_kv_norm[0]))
    w_uq = mla_w_uq[0].reshape(MLA_Q_RANK, MLA_HEADS, MLA_NOPE + MLA_ROPE)
    w_uq = jnp.pad(w_uq, ((0, 0), (0, 0), (0, MLA_QK_PAD - MLA_NOPE - MLA_ROPE)))
    w_uq = w_uq.reshape(MLA_Q_RANK, MLA_HEADS * MLA_QK_PAD).astype(BF16)
    w_ukv = mla_w_ukv[0].reshape(MLA_KV_RANK, MLA_HEADS, MLA_NOPE + MLA_V)
    w_uk = w_ukv[:, :, :MLA_NOPE].reshape(MLA_KV_RANK, MLA_HEADS * MLA_NOPE).astype(BF16)
    w_uv_t = w_ukv[:, :, MLA_NOPE:].reshape(MLA_KV_RANK, MLA_HEADS * MLA_V).T.astype(BF16)
    q_pad = _mla_q(cq, w_uq, cos_c, sin_c)
    k_pad = _mla_k(ckv, w_uk, kr)
    vt1 = _mm_nt(w_uv_t, ckv, name="mla_vt")
    c_out = _flash(q_pad, k_pad, vt1, n_kv=MLA_HEADS, rep=1, q_tile0=0, k_tile0=0, dv=MLA_V,
                   bq=512, kc=512, name="mla_attention")
    xf, x_bf, logits_t = _wout_ln_router(
        [c_out], [mla_w_out[0].astype(BF16)], xf, row2(ln_mix_g[1]), row2(ln_mix_b[1]),
        moe_w_router[1].T)
    xf, x_bf = _moe(xf, x_bf, logits_t, moe_w_gate, moe_w_up, moe_w_down, 1,
                    row2(ln_ffn_g[1]), row2(ln_ffn_b[1]))
    return xf.reshape(B, S, D)
```

```python
import functools
import math

import jax
import jax.numpy as jnp
from jax import lax
from jax.experimental import pallas as pl
from jax.experimental.pallas import tpu as pltpu

F32 = jnp.float32
BF16 = jnp.bfloat16
I32 = jnp.int32

D_MODEL = 2048
DEPTH = 2
GRID_W = 64
ROPE_THETA = 10000.0
NORM_EPS = 1e-6
LN_EPS = 1e-5

HEAD_DIM = 128
GQA_Q_HEADS = 8
GQA_KV_HEADS = 2
DIFF_HEADS = 4
A_Q = GQA_Q_HEADS * HEAD_DIM
A_KV = GQA_KV_HEADS * HEAD_DIM
B_QK = DIFF_HEADS * 2 * HEAD_DIM
B_V = DIFF_HEADS * 2 * HEAD_DIM

MLA_HEADS = 16
MLA_Q_RANK = 512
MLA_KV_RANK = 512
MLA_NOPE = 128
MLA_ROPE = 64
MLA_V = 128
MLA_QK_PAD = 256

N_EXPERTS = 16
EXPERT_FF = 2048
EC_FACTOR = 2

DEEPNORM_ALPHA = (2 * DEPTH) ** 0.25
LOG2E = math.log2(math.e)

LANES = 128
BF16_ROWS = 16
GATHER_WIN = LANES + BF16_ROWS
GATHER_SMALL = 32
VMEM_LIMIT = 56 * 1024 * 1024


def _cparams(sem, vmem=VMEM_LIMIT):
    return pltpu.CompilerParams(dimension_semantics=sem, vmem_limit_bytes=vmem)


def _dot(a, b):
    return jnp.dot(a, b, preferred_element_type=F32)


def _dot_nt(a, b):
    return lax.dot_general(a, b, (((1,), (1,)), ((), ())), preferred_element_type=F32)


def _rope(y, cos, sin, half):
    n = y.shape[-1]
    lane = lax.broadcasted_iota(I32, y.shape, 1)
    up = pltpu.roll(y, n - half, 1)
    dn = pltpu.roll(y, half, 1)
    partner = jnp.where((lane % (2 * half)) < half, up, dn)
    return y * cos + partner * sin


def _rms(y, w):
    return y * lax.rsqrt(jnp.mean(y * y, axis=-1, keepdims=True) + NORM_EPS) * w


def _layer_norm(y, g, b):
    mu = jnp.mean(y, axis=-1, keepdims=True)
    d = y - mu
    var = jnp.mean(d * d, axis=-1, keepdims=True)
    return d * lax.rsqrt(var + LN_EPS) * g + b


def _rope_tables(S, half, width):
    t = jnp.arange(S, dtype=I32)
    row = (t // GRID_W).astype(F32)[:, None]
    col = (t % GRID_W).astype(F32)[:, None]
    n = 2 * half
    freqs = ROPE_THETA ** (-jnp.arange(0, n, 2, dtype=F32) / n)[None, :]
    parts_c, parts_s = [], []
    for pos in (row, col):
        ang = pos * freqs
        c, s = jnp.cos(ang), jnp.sin(ang)
        parts_c += [c, c]
        parts_s += [-s, s]
    pad = width - 4 * half
    if pad:
        parts_c.append(jnp.ones((S, pad), F32))
        parts_s.append(jnp.zeros((S, pad), F32))
    return jnp.concatenate(parts_c, axis=1), jnp.concatenate(parts_s, axis=1)


def _proj0_kernel(x_ref, w_ref, cos_ref, sin_ref, qn_ref, kn_ref, o_ref, *, tn):
    j = pl.program_id(1)
    h = _dot(x_ref[...], w_ref[...])
    nsub = tn // HEAD_DIM
    scale = HEAD_DIM ** -0.5 * LOG2E
    k_tile = A_Q // tn
    bq_lo = (A_Q + A_KV) // tn
    bq_hi = bq_lo + B_QK // tn

    def normed(c, w):
        y = _rms(h[:, c * HEAD_DIM:(c + 1) * HEAD_DIM], w)
        return _rope(y, cos_ref[...], sin_ref[...], HEAD_DIM // 4)

    @pl.when(j < k_tile)
    def _():
        for c in range(nsub):
            o_ref[c] = (normed(c, qn_ref[...]) * scale).astype(o_ref.dtype)

    @pl.when(j == k_tile)
    def _():
        for c in range(nsub):
            o_ref[c] = normed(c, kn_ref[...]).astype(o_ref.dtype)

    @pl.when(j > k_tile)
    def _():
        mul = jnp.where((j >= bq_lo) & (j < bq_hi), scale, 1.0).astype(F32)
        for c in range(nsub):
            o_ref[c] = (h[:, c * HEAD_DIM:(c + 1) * HEAD_DIM] * mul).astype(o_ref.dtype)


def _proj0(x_bf, w_bf, cos, sin, qn, kn, *, tm=512, tn=256):
    S, K = x_bf.shape
    N = w_bf.shape[1]
    assert A_Q % tn == 0 and tn == A_KV and N % tn == 0
    return pl.pallas_call(
        functools.partial(_proj0_kernel, tn=tn),
        grid=(S // tm, N // tn),
        in_specs=[
            pl.BlockSpec((tm, K), lambda i, j: (i, 0)),
            pl.BlockSpec((K, tn), lambda i, j: (0, j)),
            pl.BlockSpec((tm, HEAD_DIM), lambda i, j: (i, 0)),
            pl.BlockSpec((tm, HEAD_DIM), lambda i, j: (i, 0)),
            pl.BlockSpec((1, HEAD_DIM), lambda i, j: (0, 0)),
            pl.BlockSpec((1, HEAD_DIM), lambda i, j: (0, 0)),
        ],
        out_specs=pl.BlockSpec((tn // HEAD_DIM, tm, HEAD_DIM), lambda i, j: (j, i, 0)),
        out_shape=jax.ShapeDtypeStruct((N // HEAD_DIM, S, HEAD_DIM), BF16),
        compiler_params=_cparams(("parallel", "parallel")),
        name="proj0",
    )(x_bf, w_bf, cos, sin, qn, kn)


def _values_t_kernel(wt_ref, x_ref, o_ref):
    o_ref[...] = _dot_nt(wt_ref[...], x_ref[...]).astype(o_ref.dtype)


def _values_t(w, x_bf, *, tm=512, tn=256, name="values_t"):
    S, K = x_bf.shape
    N = w.shape[1]
    return pl.pallas_call(
        _values_t_kernel,
        grid=(S // tm, N // tn),
        in_specs=[pl.BlockSpec((tn, K), lambda i, j: (j, 0)),
                  pl.BlockSpec((tm, K), lambda i, j: (i, 0))],
        out_specs=pl.BlockSpec((tn, tm), lambda i, j: (j, i)),
        out_shape=jax.ShapeDtypeStruct((N, S), BF16),
        compiler_params=_cparams(("parallel", "parallel")),
        name=name,
    )(w.T.astype(BF16), x_bf)


def _online_softmax_pv(s, vt, m_ref, l_ref, acc_ref, shift=None):
    m_prev = m_ref[...]
    m_chunk = jnp.max(s, axis=0, keepdims=True)
    if shift is None:
        m_new = jnp.maximum(m_prev, m_chunk)
        p = jnp.exp2(s - m_new)
    else:
        m_new = jnp.maximum(m_prev, m_chunk + shift)
        p = jnp.exp2(s - (m_new - shift))
    alpha = jnp.exp2(m_prev - m_new)
    l_ref[...] = alpha * l_ref[...] + jnp.sum(p, axis=0, keepdims=True)
    acc_ref[...] = alpha * acc_ref[...] + _dot(vt, p.astype(BF16))
    m_ref[...] = m_new


def _flash_kernel(q_ref, k_ref, vt_ref, o_ref, qt_sc, s_sc, m_sc, l_sc, acc_sc, *, kc):
    R, bq, D = q_ref.shape
    S = k_ref.shape[1]
    Dv = vt_ref.shape[0]
    n = S // kc
    assert n % 2 == 0
    qt_sc[...] = q_ref[...].reshape(R * bq, D).astype(F32).T.astype(BF16)
    m_sc[...] = jnp.full(m_sc.shape, -jnp.inf, F32)
    l_sc[...] = jnp.zeros(l_sc.shape, F32)
    acc_sc[...] = jnp.zeros(acc_sc.shape, F32)

    def scores(c):
        off = pl.multiple_of(c * kc, kc)
        return _dot(k_ref[0, pl.ds(off, kc), :], qt_sc[...])

    def values_t(c):
        return vt_ref[:, pl.ds(pl.multiple_of(c * kc, kc), kc)]

    s_sc[0] = scores(0)

    def body(i, carry):
        c = 2 * i
        s_sc[1] = scores(c + 1)
        _online_softmax_pv(s_sc[0], values_t(c), m_sc, l_sc, acc_sc)
        s_sc[0] = scores(jnp.minimum(c + 2, n - 1))
        _online_softmax_pv(s_sc[1], values_t(c + 1), m_sc, l_sc, acc_sc)
        return carry

    lax.fori_loop(0, n // 2, body, 0)
    out = (acc_sc[...] / l_sc[...]).T
    for r in range(R):
        o_ref[:, r * Dv:(r + 1) * Dv] = out[r * bq:(r + 1) * bq].astype(o_ref.dtype)


def _flash(q_arr, k_arr, vt_arr, *, n_kv, rep, q_tile0, k_tile0, dv, bq, kc, name):
    _, S, D = q_arr.shape
    kc = min(kc, S)
    bq = min(bq, S)
    assert q_tile0 % rep == 0
    N = rep * bq
    rows = dv
    return pl.pallas_call(
        functools.partial(_flash_kernel, kc=kc),
        grid=(n_kv, S // bq),
        in_specs=[
            pl.BlockSpec((rep, bq, D), lambda g, i: (q_tile0 // rep + g, i, 0)),
            pl.BlockSpec((1, S, D), lambda g, i: (k_tile0 + g, 0, 0)),
            pl.BlockSpec((rows, S), lambda g, i: (g, 0)),
        ],
        out_specs=pl.BlockSpec((bq, rep * dv), lambda g, i: (i, g)),
        out_shape=jax.ShapeDtypeStruct((S, n_kv * rep * dv), BF16),
        scratch_shapes=[pltpu.VMEM((D, N), BF16),
                        pltpu.VMEM((2, kc, N), F32),
                        pltpu.VMEM((1, N), F32), pltpu.VMEM((1, N), F32), pltpu.VMEM((rows, N), F32)],
        compiler_params=_cparams(("parallel", "parallel")),
        name=name,
    )(q_arr, k_arr, vt_arr)


def _diff_kernel(q_ref, k_ref, vt_ref, lq1_ref, lk1_ref, lq2_ref, lk2_ref, subln_ref, o_ref,
                 qt_sc, s_sc, m_sc, l_sc, acc_sc, *, kc, lam_init):
    _, bq, D = q_ref.shape
    S = k_ref.shape[1]
    h = pl.program_id(0)
    q0 = pl.program_id(1) * bq
    slope = jnp.float32(2.0 ** (-8.0 * DIFF_HEADS / DIFF_HEADS))
    for hh in range(DIFF_HEADS - 1):
        slope = jnp.where(h == hh, jnp.float32(2.0 ** (-8.0 * (hh + 1) / DIFF_HEADS)), slope)
    slope2 = slope * LOG2E
    rel = (lax.broadcasted_iota(I32, (kc, bq), 0) - lax.broadcasted_iota(I32, (kc, bq), 1)).astype(F32)
    rel_s = rel * slope2
    for j in range(2):
        qt_sc[j] = q_ref[j].astype(F32).T.astype(BF16)
    m_sc[...] = jnp.full(m_sc.shape, -jnp.inf, F32)
    l_sc[...] = jnp.zeros(l_sc.shape, F32)
    acc_sc[...] = jnp.zeros(acc_sc.shape, F32)

    n = S // kc

    def scores(j, c):
        off = pl.multiple_of(c * kc, kc)
        return _dot(k_ref[j, pl.ds(off, kc), :], qt_sc[j])

    s_sc[0] = scores(0, 0)

    def chunk_step(c, penalty):
        off = pl.multiple_of(c * kc, kc)
        delta = (off - q0).astype(F32)
        vt = vt_ref[:, pl.ds(off, kc)]
        s_sc[1] = scores(1, c)
        s0, shift = penalty(s_sc[0], delta)
        _online_softmax_pv(s0, vt, m_sc.at[0], l_sc.at[0], acc_sc.at[0], shift)
        s_sc[0] = scores(0, jnp.minimum(c + 1, n - 1))
        s1, shift = penalty(s_sc[1], delta)
        _online_softmax_pv(s1, vt, m_sc.at[1], l_sc.at[1], acc_sc.at[1], shift)

    def keys_before(c, carry):
        chunk_step(c, lambda s, delta: (s + rel_s, slope2 * delta))
        return carry

    def keys_overlap(c, carry):
        chunk_step(c, lambda s, delta: (s - jnp.abs(rel + delta) * slope2, None))
        return carry

    def keys_after(c, carry):
        chunk_step(c, lambda s, delta: (s - rel_s, -(slope2 * delta)))
        return carry

    c_lo = q0 // kc
    c_hi = (q0 + bq + kc - 1) // kc
    lax.fori_loop(0, c_lo, keys_before, 0)
    lax.fori_loop(c_lo, c_hi, keys_overlap, 0)
    lax.fori_loop(c_hi, n, keys_after, 0)
    lam = (jnp.exp(jnp.sum(lq1_ref[...] * lk1_ref[...], axis=1, keepdims=True))
           - jnp.exp(jnp.sum(lq2_ref[...] * lk2_ref[...], axis=1, keepdims=True)) + lam_init)
    out = (acc_sc[0] / l_sc[0] - lam * (acc_sc[1] / l_sc[1])).T
    o_ref[...] = (_rms(out, subln_ref[...]) * (1.0 - lam_init)).astype(o_ref.dtype)


def _diff_attention(hq, vt_arr, lq1, lk1, lq2, lk2, subln, *, q_tile0, k_tile0, lam_init,
                    bq=512, kc=1024):
    _, S, D = hq.shape
    kc = min(kc, S)
    bq = min(bq, S)
    rows = 2 * D
    vec = pl.BlockSpec((1, D), lambda h, i: (0, 0))
    return pl.pallas_call(
        functools.partial(_diff_kernel, kc=kc, lam_init=lam_init),
        grid=(DIFF_HEADS, S // bq),
        in_specs=[
            pl.BlockSpec((2, bq, D), lambda h, i: (q_tile0 // 2 + h, i, 0)),
            pl.BlockSpec((2, S, D), lambda h, i: (k_tile0 // 2 + h, 0, 0)),
            pl.BlockSpec((rows, S), lambda h, i: (h, 0)),
            vec, vec, vec, vec,
            pl.BlockSpec((1, 2 * D), lambda h, i: (0, 0)),
        ],
        out_specs=pl.BlockSpec((bq, 2 * D), lambda h, i: (i, h)),
        out_shape=jax.ShapeDtypeStruct((S, DIFF_HEADS * 2 * D), BF16),
        scratch_shapes=[pltpu.VMEM((2, D, bq), BF16), pltpu.VMEM((2, kc, bq), F32),
                        pltpu.VMEM((2, 1, bq), F32), pltpu.VMEM((2, 1, bq), F32),
                        pltpu.VMEM((2, rows, bq), F32)],
        compiler_params=_cparams(("parallel", "parallel")),
        name="diff_attention",
    )(hq, hq, vt_arr, lq1, lk1, lq2, lk2, subln)


def _wout_kernel(*refs, n_in):
    a_refs = refs[:n_in]
    w_refs = refs[n_in:2 * n_in]
    x_ref, g_ref, b_ref, wr_ref, xo_ref, xb_ref, lg_ref = refs[2 * n_in:]
    mix = _dot(a_refs[0][...], w_refs[0][...])
    for a, w in zip(a_refs[1:], w_refs[1:]):
        mix = mix + _dot(a[...], w[...])
    y = _layer_norm(DEEPNORM_ALPHA * x_ref[...] + mix, g_ref[...], b_ref[...])
    xo_ref[...] = y
    y_hi = y.astype(BF16)
    xb_ref[...] = y_hi
    y_lo = (y - y_hi.astype(F32)).astype(BF16)
    wr = wr_ref[...]
    w_hi = wr.astype(BF16)
    w_lo = (wr - w_hi.astype(F32)).astype(BF16)
    E = wr.shape[0]
    both = _dot_nt(jnp.concatenate([w_hi, w_lo], axis=0), y_hi)
    lg_ref[...] = both[:E] + both[E:] + _dot_nt(w_hi, y_lo)


def _wout_ln_router(a_list, w_list, x, g, b, wr_t, *, tm=256):
    S, D = x.shape
    n_in = len(a_list)
    E = wr_t.shape[0]
    in_specs = ([pl.BlockSpec((tm, a.shape[1]), lambda i: (i, 0)) for a in a_list]
                + [pl.BlockSpec(w.shape, lambda i: (0, 0)) for w in w_list]
                + [pl.BlockSpec((tm, D), lambda i: (i, 0)),
                   pl.BlockSpec((1, D), lambda i: (0, 0)),
                   pl.BlockSpec((1, D), lambda i: (0, 0)),
                   pl.BlockSpec((E, D), lambda i: (0, 0))])
    return pl.pallas_call(
        functools.partial(_wout_kernel, n_in=n_in),
        grid=(S // tm,),
        in_specs=in_specs,
        out_specs=[pl.BlockSpec((tm, D), lambda i: (i, 0)),
                   pl.BlockSpec((tm, D), lambda i: (i, 0)),
                   pl.BlockSpec((E, tm), lambda i: (0, i))],
        out_shape=[jax.ShapeDtypeStruct((S, D), F32),
                   jax.ShapeDtypeStruct((S, D), BF16),
                   jax.ShapeDtypeStruct((E, S), F32)],
        compiler_params=_cparams(("parallel",)),
        name="wout_ln_router",
    )(*a_list, *w_list, x, g, b, wr_t)


def _select_kernel(lg_ref, pos_ref, gate_ref, off_ref, cnt_ref, *, E, nb, cap):
    lg = lg_ref[...].reshape(E, nb, LANES)
    mx = jnp.max(lg, axis=0, keepdims=True)
    ex = jnp.exp(lg - mx)
    aff = ex / jnp.sum(ex, axis=0, keepdims=True)
    bits = lax.bitcast_convert_type(aff, I32)

    def count(msk):
        c = jnp.sum(msk.astype(F32), axis=1, keepdims=True)
        return jnp.sum(c, axis=2, keepdims=True)

    def search(i, thr):
        cand = thr | lax.shift_left(jnp.int32(1), 30 - i)
        return jnp.where(count(bits >= cand) >= cap, cand, thr)

    thr = lax.fori_loop(0, 31, search, jnp.zeros((E, 1, 1), I32))
    gt = bits > thr
    eq = bits == thr
    need = cap - count(gt)

    col = lax.broadcasted_iota(I32, (LANES, LANES), 1)
    rw = lax.broadcasted_iota(I32, (LANES, LANES), 0)
    upper = (rw < col).astype(BF16)
    ones = jnp.ones((LANES, LANES), BF16)
    lower = (lax.broadcasted_iota(I32, (nb, nb), 1) < lax.broadcasted_iota(I32, (nb, nb), 0)).astype(BF16)

    def prefix(msk):
        m2 = jnp.where(msk, 1.0, 0.0).reshape(E * nb, LANES).astype(BF16)
        within = _dot(m2, upper).reshape(E, nb, LANES)
        tot = _dot(m2, ones).reshape(E, nb, LANES)
        offs = jnp.stack([_dot(lower, tot[e].astype(BF16)) for e in range(E)], axis=0)
        return within + offs, offs, tot

    eq_rank, _, _ = prefix(eq)
    sel = gt | (eq & (eq_rank < need))
    pos, offs, tot = prefix(sel)
    pos_ref[...] = jnp.where(sel, pos, -1.0).astype(I32).reshape(E * nb, LANES)
    gate_ref[...] = jnp.where(sel, aff, 0.0).reshape(E * nb, LANES)
    off_ref[...] = offs.astype(I32).reshape(E * nb, LANES)
    cnt_ref[...] = tot.astype(I32).reshape(E * nb, LANES)


def _select(logits_t, cap):
    E, S = logits_t.shape
    nb = S // LANES
    shp = (E * nb, LANES)
    full = pl.BlockSpec(shp, lambda: (0, 0))
    return pl.pallas_call(
        functools.partial(_select_kernel, E=E, nb=nb, cap=cap),
        in_specs=[full],
        out_specs=[full, full, full, full],
        out_shape=[jax.ShapeDtypeStruct(shp, I32), jax.ShapeDtypeStruct(shp, F32),
                   jax.ShapeDtypeStruct(shp, I32), jax.ShapeDtypeStruct(shp, I32)],
        compiler_params=pltpu.CompilerParams(vmem_limit_bytes=VMEM_LIMIT),
        name="select",
    )(logits_t.reshape(shp))


def _gather_kernel(off_s, cnt_s, x_ref, pos_ref, gate_ref, xe_ref, gs_ref, *, cap, tbs):
    e = pl.program_id(0)
    t = pl.program_id(1)

    @pl.when(t == 0)
    def _():
        xe_ref[...] = jnp.zeros(xe_ref.shape, xe_ref.dtype)
        gs_ref[...] = jnp.zeros(gs_ref.shape, gs_ref.dtype)

    def place(u, tb, width):
        base = jnp.minimum((off_s[e, tb] // BF16_ROWS) * BF16_ROWS, cap - width)
        base = pl.multiple_of(base, BF16_ROWS)
        slot = base + lax.broadcasted_iota(I32, (width, LANES), 0)
        hit = pos_ref[u] == slot
        rows = _dot(hit.astype(BF16), x_ref[pl.ds(u * LANES, LANES), :])
        win = pl.ds(base, width)
        xe_ref[0, win, :] = xe_ref[0, win, :] + rows.astype(xe_ref.dtype)
        g = jnp.sum(jnp.where(hit, gate_ref[u], 0.0), axis=1, keepdims=True)
        gs_ref[0, win, :] = gs_ref[0, win, :] + jnp.broadcast_to(g, (width, LANES))

    for u in range(tbs):
        tb = t * tbs + u
        cnt = cnt_s[e, tb]

        @pl.when((cnt > 0) & (cnt <= GATHER_SMALL))
        def _():
            place(u, tb, GATHER_SMALL + BF16_ROWS)

        @pl.when(cnt > GATHER_SMALL)
        def _():
            place(u, tb, GATHER_WIN)


def _gather(off_s, cnt_s, x_bf, pos, gate, cap, *, tbs=4):
    S, D = x_bf.shape
    E, nb = off_s.shape
    assert cap >= GATHER_WIN and cap % BF16_ROWS == 0 and nb % tbs == 0
    pos3 = pos.reshape(E * nb, 1, LANES)
    gate3 = gate.reshape(E * nb, 1, LANES)
    nt = nb // tbs
    return pl.pallas_call(
        functools.partial(_gather_kernel, cap=cap, tbs=tbs),
        grid_spec=pltpu.PrefetchScalarGridSpec(
            num_scalar_prefetch=2,
            grid=(E, nt),
            in_specs=[
                pl.BlockSpec((tbs * LANES, D), lambda e, t, o, c: (t, 0)),
                pl.BlockSpec((tbs, 1, LANES), lambda e, t, o, c: (e * nt + t, 0, 0)),
                pl.BlockSpec((tbs, 1, LANES), lambda e, t, o, c: (e * nt + t, 0, 0)),
            ],
            out_specs=[pl.BlockSpec((1, cap, D), lambda e, t, o, c: (e, 0, 0)),
                       pl.BlockSpec((1, cap, LANES), lambda e, t, o, c: (e, 0, 0))],
        ),
        out_shape=[jax.ShapeDtypeStruct((E, cap, D), BF16),
                   jax.ShapeDtypeStruct((E, cap, LANES), F32)],
        compiler_params=_cparams(("parallel", "arbitrary")),
        name="gather",
    )(off_s, cnt_s, x_bf, pos3, gate3)


def _ffn_kernel(xe_ref, wg_ref, wu_ref, wd_ref, gs_ref, ye_ref, acc_sc, *, sub):
    fc = pl.program_id(2)
    rows = xe_ref.shape[1]
    @pl.when(fc == 0)
    def _():
        acc_sc[...] = jnp.zeros(acc_sc.shape, F32)

    wg = wg_ref[...].astype(BF16)
    wu = wu_ref[...].astype(BF16)
    wd = wd_ref[...].astype(BF16)
    for r in range(rows // sub):
        sl = pl.ds(r * sub, sub)
        xt = xe_ref[0, sl, :]
        hg = _dot(xt, wg)
        hu = _dot(xt, wu)
        hid = (hg * jax.nn.sigmoid(hg) * hu).astype(BF16)
        acc_sc[sl, :] = acc_sc[sl, :] + _dot(hid, wd)

    @pl.when(fc == pl.num_programs(2) - 1)
    def _():
        ye_ref[0] = (acc_sc[...] * gs_ref[0][:, :1]).astype(ye_ref.dtype)


def _ffn(xe, gs, w_gate, w_up, w_down, layer, *, halves=2, fcw=256):
    E, cap, D = xe.shape
    F = w_gate.shape[-1]
    rows = cap // halves
    sub = min(256, rows)
    return pl.pallas_call(
        functools.partial(_ffn_kernel, sub=sub),
        grid=(E, halves, F // fcw),
        in_specs=[
            pl.BlockSpec((1, rows, D), lambda e, t, f: (e, t, 0)),
            pl.BlockSpec((None, None, D, fcw), lambda e, t, f: (layer, e, 0, f)),
            pl.BlockSpec((None, None, D, fcw), lambda e, t, f: (layer, e, 0, f)),
            pl.BlockSpec((None, None, fcw, D), lambda e, t, f: (layer, e, f, 0)),
            pl.BlockSpec((1, rows, LANES), lambda e, t, f: (e, t, 0)),
        ],
        out_specs=pl.BlockSpec((1, rows, D), lambda e, t, f: (e, t, 0)),
        out_shape=jax.ShapeDtypeStruct((E, cap, D), BF16),
        scratch_shapes=[pltpu.VMEM((rows, D), F32)],
        compiler_params=_cparams(("parallel", "parallel", "arbitrary")),
        name="expert_ffn",
    )(xe, w_gate, w_up, w_down, gs)


def _combine_kernel(win_s, *refs, eg):
    y_refs = refs[:eg]
    post_ref, x_ref, g_ref, b_ref, xo_ref, xb_ref, acc_sc = refs[eg:]
    tb = pl.program_id(0)
    grp = pl.program_id(1)

    @pl.when(grp == 0)
    def _():
        acc_sc[...] = jnp.zeros(acc_sc.shape, F32)

    pt = post_ref[...].astype(F32)
    lane = lax.broadcasted_iota(I32, pt.shape, 1)
    col = lax.broadcasted_iota(I32, (LANES, GATHER_WIN), 1)
    total = None
    for k in range(eg):
        e = grp * eg + k
        pcol = jnp.sum(jnp.where(lane == e, pt, 0.0), axis=1, keepdims=True).astype(I32)
        hit = ((pcol - win_s[e, tb]) == col).astype(BF16)
        part = _dot(hit, y_refs[k][0])
        total = part if total is None else total + part
    acc_sc[...] = acc_sc[...] + total

    @pl.when(grp == pl.num_programs(1) - 1)
    def _():
        y = _layer_norm(DEEPNORM_ALPHA * x_ref[...] + acc_sc[...], g_ref[...], b_ref[...])
        xo_ref[...] = y
        xb_ref[...] = y.astype(BF16)


def _combine_ln(win_s, ye, pos_t, x, g, b, *, eg=16):
    S, D = x.shape
    E, cap, _ = ye.shape
    nb = S // LANES
    assert E % eg == 0

    def y_spec(k):
        return pl.BlockSpec((pl.Element(1), pl.Element(GATHER_WIN), pl.Element(D)),
                            lambda t, gr, ws: (gr * eg + k, (ws[gr * eg + k, t] // BF16_ROWS) * BF16_ROWS, 0))

    row = lambda t, gr, ws: (t, 0)
    fixed = lambda t, gr, ws: (0, 0)
    return pl.pallas_call(
        functools.partial(_combine_kernel, eg=eg),
        grid_spec=pltpu.PrefetchScalarGridSpec(
            num_scalar_prefetch=1,
            grid=(nb, E // eg),
            in_specs=([y_spec(k) for k in range(eg)]
                      + [pl.BlockSpec((LANES, E), row), pl.BlockSpec((LANES, D), row),
                         pl.BlockSpec((1, D), fixed), pl.BlockSpec((1, D), fixed)]),
            out_specs=[pl.BlockSpec((LANES, D), row), pl.BlockSpec((LANES, D), row)],
            scratch_shapes=[pltpu.VMEM((LANES, D), F32)],
        ),
        out_shape=[jax.ShapeDtypeStruct((S, D), F32), jax.ShapeDtypeStruct((S, D), BF16)],
        compiler_params=_cparams(("parallel", "arbitrary")),
        name="combine_ln",
    )(win_s, *([ye] * eg), pos_t, x, g, b)


def _moe(x, x_bf, logits_t, w_gate, w_up, w_down, layer, g, b):
    S, D = x.shape
    E = logits_t.shape[0]
    nb = S // LANES
    cap = EC_FACTOR * S // E
    pos, gate, off, cnt = _select(logits_t, cap)
    off_s = off[:, 0].reshape(E, nb)
    cnt_s = cnt[:, 0].reshape(E, nb)
    xe, gs = _gather(off_s, cnt_s, x_bf, pos, gate, cap)
    ye = _ffn(xe, gs, w_gate, w_up, w_down, layer)
    pos_t = pos.reshape(E, S).T
    win_s = jnp.minimum((off_s // BF16_ROWS) * BF16_ROWS, cap - GATHER_WIN)
    return _combine_ln(win_s, ye, pos_t, x, g, b)


def _mla_in_kernel(x_ref, w_ref, cos_ref, sin_ref, qn_ref, kvn_ref, cq_ref, ckv_ref, kr_ref):
    h = _dot(x_ref[...], w_ref[...])
    cq_ref[...] = _rms(h[:, :MLA_Q_RANK], qn_ref[...]).astype(BF16)
    ckv_ref[...] = _rms(h[:, MLA_Q_RANK:MLA_Q_RANK + MLA_KV_RANK], kvn_ref[...]).astype(BF16)
    kr = h[:, MLA_Q_RANK + MLA_KV_RANK:]
    kr_ref[...] = _rope(kr, cos_ref[...], sin_ref[...], MLA_ROPE // 4).astype(BF16)


def _mla_in(x_bf, w_bf, cos, sin, qn, kvn, *, tm=512):
    S, K = x_bf.shape
    N = w_bf.shape[1]
    row = lambda i: (i, 0)
    fixed = lambda i: (0, 0)
    return pl.pallas_call(
        _mla_in_kernel,
        grid=(S // tm,),
        in_specs=[pl.BlockSpec((tm, K), row), pl.BlockSpec((K, N), fixed),
                  pl.BlockSpec((tm, LANES), row), pl.BlockSpec((tm, LANES), row),
                  pl.BlockSpec((1, MLA_Q_RANK), fixed), pl.BlockSpec((1, MLA_KV_RANK), fixed)],
        out_specs=[pl.BlockSpec((tm, MLA_Q_RANK), row), pl.BlockSpec((tm, MLA_KV_RANK), row),
                   pl.BlockSpec((tm, LANES), row)],
        out_shape=[jax.ShapeDtypeStruct((S, MLA_Q_RANK), BF16),
                   jax.ShapeDtypeStruct((S, MLA_KV_RANK), BF16),
                   jax.ShapeDtypeStruct((S, LANES), BF16)],
        compiler_params=_cparams(("parallel",)),
        name="mla_in",
    )(x_bf, w_bf, cos, sin, qn, kvn)


def _mla_q_kernel(c_ref, w_ref, cos_ref, sin_ref, o_ref, *, heads):
    h = _dot(c_ref[...], w_ref[...])
    scale = (MLA_NOPE + MLA_ROPE) ** -0.5 * LOG2E
    for c in range(heads):
        base = c * MLA_QK_PAD
        o_ref[c, :, :MLA_NOPE] = (h[:, base:base + MLA_NOPE] * scale).astype(BF16)
        rp = _rope(h[:, base + MLA_NOPE:base + MLA_QK_PAD], cos_ref[...], sin_ref[...], MLA_ROPE // 4)
        o_ref[c, :, MLA_NOPE:] = (rp * scale).astype(BF16)


def _mla_q(cq, w_bf, cos, sin, *, tm=512, heads=2):
    S, K = cq.shape
    tn = heads * MLA_QK_PAD
    return pl.pallas_call(
        functools.partial(_mla_q_kernel, heads=heads),
        grid=(S // tm, MLA_HEADS // heads),
        in_specs=[pl.BlockSpec((tm, K), lambda i, j: (i, 0)),
                  pl.BlockSpec((K, tn), lambda i, j: (0, j)),
                  pl.BlockSpec((tm, LANES), lambda i, j: (i, 0)),
                  pl.BlockSpec((tm, LANES), lambda i, j: (i, 0))],
        out_specs=pl.BlockSpec((heads, tm, MLA_QK_PAD), lambda i, j: (j, i, 0)),
        out_shape=jax.ShapeDtypeStruct((MLA_HEADS, S, MLA_QK_PAD), BF16),
        compiler_params=_cparams(("parallel", "parallel")),
        name="mla_q",
    )(cq, w_bf, cos, sin)


def _mla_k_kernel(c_ref, wk_ref, kr_ref, k_ref, *, heads):
    kn = _dot(c_ref[...], wk_ref[...])
    for hh in range(heads):
        k_ref[hh, :, :MLA_NOPE] = kn[:, hh * MLA_NOPE:(hh + 1) * MLA_NOPE].astype(BF16)
        k_ref[hh, :, MLA_NOPE:] = kr_ref[...]


def _mla_k(ckv, wk_bf, kr, *, tm=512, heads=4):
    S, K = ckv.shape
    return pl.pallas_call(
        functools.partial(_mla_k_kernel, heads=heads),
        grid=(S // tm, MLA_HEADS // heads),
        in_specs=[pl.BlockSpec((tm, K), lambda i, j: (i, 0)),
                  pl.BlockSpec((K, heads * MLA_NOPE), lambda i, j: (0, j)),
                  pl.BlockSpec((tm, LANES), lambda i, j: (i, 0))],
        out_specs=pl.BlockSpec((heads, tm, MLA_QK_PAD), lambda i, j: (j, i, 0)),
        out_shape=jax.ShapeDtypeStruct((MLA_HEADS, S, MLA_QK_PAD), BF16),
        compiler_params=_cparams(("parallel", "parallel")),
        name="mla_k",
    )(ckv, wk_bf, kr)


def kernel(x, ab_w_in, ab_q_norm, ab_k_norm, ab_lambda_q1, ab_lambda_k1, ab_lambda_q2, ab_lambda_k2, ab_subln, ab_w_out, mla_w_in, mla_q_norm, mla_kv_norm, mla_w_uq, mla_w_ukv, mla_w_out, ln_mix_g, ln_mix_b, moe_w_router, moe_w_gate, moe_w_up, moe_w_down, ln_ffn_g, ln_ffn_b):
    B, S, D = x.shape
    assert B == 1 and D == D_MODEL and S % 512 == 0
    xf = x.reshape(S, D)
    x_bf = xf.astype(BF16)
    cos_a, sin_a = _rope_tables(S, HEAD_DIM // 4, HEAD_DIM)
    cos_c, sin_c = _rope_tables(S, MLA_ROPE // 4, LANES)
    row2 = lambda v: v.reshape(1, -1)

    lam_init = 0.8 - 0.6 * math.exp(-0.3 * 0)
    w_in = ab_w_in[0].astype(BF16)
    c_av = A_Q + A_KV
    c_bq = c_av + A_KV
    c_bv = c_bq + 2 * B_QK
    w_qk = jnp.concatenate([w_in[:, :c_av], w_in[:, c_bq:c_bv]], axis=1)
    hq = _proj0(x_bf, w_qk, cos_a, sin_a, row2(ab_q_norm[0]), row2(ab_k_norm[0]))
    vt_a = _values_t(w_in[:, c_av:c_bq], x_bf, name="values_t_gqa")
    vt_b = _values_t(w_in[:, c_bv:], x_bf, name="values_t_diff")
    t_ak = A_Q // HEAD_DIM
    t_bq = t_ak + GQA_KV_HEADS
    t_bk = t_bq + 2 * DIFF_HEADS
    a_out = _flash(hq, hq, vt_a, n_kv=GQA_KV_HEADS, rep=GQA_Q_HEADS // GQA_KV_HEADS,
                   q_tile0=0, k_tile0=t_ak, dv=HEAD_DIM, bq=256, kc=512, name="gqa_attention")
    b_out = _diff_attention(hq, vt_b, row2(ab_lambda_q1[0]), row2(ab_lambda_k1[0]),
                            row2(ab_lambda_q2[0]), row2(ab_lambda_k2[0]), row2(ab_subln[0]),
                            q_tile0=t_bq, k_tile0=t_bk, lam_init=lam_init)
    w_out = ab_w_out[0].astype(BF16)
    xf, x_bf, logits_t = _wout_ln_router(
        [a_out, b_out], [w_out[:A_Q], w_out[A_Q:]], xf, row2(ln_mix_g[0]), row2(ln_mix_b[0]),
        moe_w_router[0].T)
    xf, x_bf = _moe(xf, x_bf, logits_t, moe_w_gate, moe_w_up, moe_w_down, 0,
                    row2(ln_ffn_g[0]), row2(ln_ffn_b[0]))

    w1 = jnp.pad(mla_w_in[0], ((0, 0), (0, LANES - MLA_ROPE))).astype(BF16)
    cq, ckv, kr = _mla_in(x_bf, w1, cos_c, sin_c, row2(mla_q_norm[0]), row2(mla_kv_norm[0]))
    w_uq = mla_w_uq[0].reshape(MLA_Q_RANK, MLA_HEADS, MLA_NOPE + MLA_ROPE)
    w_uq = jnp.pad(w_uq, ((0, 0), (0, 0), (0, MLA_QK_PAD - MLA_NOPE - MLA_ROPE)))
    w_uq = w_uq.reshape(MLA_Q_RANK, MLA_HEADS * MLA_QK_PAD).astype(BF16)
    w_ukv = mla_w_ukv[0].reshape(MLA_KV_RANK, MLA_HEADS, MLA_NOPE + MLA_V)
    w_uk = w_ukv[:, :, :MLA_NOPE].reshape(MLA_KV_RANK, MLA_HEADS * MLA_NOPE).astype(BF16)
    w_uv = w_ukv[:, :, MLA_NOPE:].reshape(MLA_KV_RANK, MLA_HEADS * MLA_V)
    q_pad = _mla_q(cq, w_uq, cos_c, sin_c)
    k_pad = _mla_k(ckv, w_uk, kr)
    vt1 = _values_t(w_uv, ckv, name="values_t_mla")
    c_out = _flash(q_pad, k_pad, vt1, n_kv=MLA_HEADS, rep=1, q_tile0=0, k_tile0=0, dv=MLA_V,
                   bq=512, kc=1024, name="mla_attention")
    xf, x_bf, logits_t = _wout_ln_router(
        [c_out], [mla_w_out[0].astype(BF16)], xf, row2(ln_mix_g[1]), row2(ln_mix_b[1]),
        moe_w_router[1].T)
    xf, x_bf = _moe(xf, x_bf, logits_t, moe_w_gate, moe_w_up, moe_w_down, 1,
                    row2(ln_ffn_g[1]), row2(ln_ffn_b[1]))
    return xf.reshape(B, S, D)
```

```python
import functools
import math

import jax
import jax.numpy as jnp
from jax import lax
from jax.experimental import pallas as pl
from jax.experimental.pallas import tpu as pltpu

F32 = jnp.float32
BF16 = jnp.bfloat16
I32 = jnp.int32

D_MODEL = 2048
DEPTH = 2
GRID_W = 64
ROPE_THETA = 10000.0
NORM_EPS = 1e-6
LN_EPS = 1e-5

HEAD_DIM = 128
GQA_Q_HEADS = 8
GQA_KV_HEADS = 2
DIFF_HEADS = 4
A_Q = GQA_Q_HEADS * HEAD_DIM
A_KV = GQA_KV_HEADS * HEAD_DIM
B_QK = DIFF_HEADS * 2 * HEAD_DIM
B_V = DIFF_HEADS * 2 * HEAD_DIM

MLA_HEADS = 16
MLA_Q_RANK = 512
MLA_KV_RANK = 512
MLA_NOPE = 128
MLA_ROPE = 64
MLA_V = 128
MLA_QK_PAD = 256

N_EXPERTS = 16
EXPERT_FF = 2048
EC_FACTOR = 2

DEEPNORM_ALPHA = (2 * DEPTH) ** 0.25
LOG2E = math.log2(math.e)

LANES = 128
BF16_ROWS = 16
GATHER_WIN = LANES + BF16_ROWS
GATHER_SMALL = 32
VMEM_LIMIT = 56 * 1024 * 1024


def _cparams(sem, vmem=VMEM_LIMIT):
    return pltpu.CompilerParams(dimension_semantics=sem, vmem_limit_bytes=vmem)


def _dot(a, b):
    return jnp.dot(a, b, preferred_element_type=F32)


def _dot_nt(a, b):
    return lax.dot_general(a, b, (((1,), (1,)), ((), ())), preferred_element_type=F32)


def _rope(y, cos, sin, half):
    n = y.shape[-1]
    lane = lax.broadcasted_iota(I32, y.shape, 1)
    up = pltpu.roll(y, n - half, 1)
    dn = pltpu.roll(y, half, 1)
    partner = jnp.where((lane % (2 * half)) < half, up, dn)
    return y * cos + partner * sin


def _rms(y, w):
    return y * lax.rsqrt(jnp.mean(y * y, axis=-1, keepdims=True) + NORM_EPS) * w


def _layer_norm(y, g, b):
    mu = jnp.mean(y, axis=-1, keepdims=True)
    d = y - mu
    var = jnp.mean(d * d, axis=-1, keepdims=True)
    return d * lax.rsqrt(var + LN_EPS) * g + b


def _rope_tables(S, half, width):
    n_rows = S // GRID_W
    n = 2 * half
    freqs = ROPE_THETA ** (-jnp.arange(0, n, 2, dtype=F32) / n)[None, :]
    parts_c, parts_s = [], []
    for count, along_rows in ((n_rows, True), (GRID_W, False)):
        ang = jnp.arange(count, dtype=F32)[:, None] * freqs
        c = jnp.concatenate([jnp.cos(ang)] * 2, axis=1)
        s = jnp.concatenate([-jnp.sin(ang), jnp.sin(ang)], axis=1)
        shape = (n_rows, GRID_W, n)
        expand = (lambda a: a[:, None, :]) if along_rows else (lambda a: a[None, :, :])
        parts_c.append(jnp.broadcast_to(expand(c), shape).reshape(S, n))
        parts_s.append(jnp.broadcast_to(expand(s), shape).reshape(S, n))
    pad = width - 4 * half
    if pad:
        parts_c.append(jnp.ones((S, pad), F32))
        parts_s.append(jnp.zeros((S, pad), F32))
    return jnp.concatenate(parts_c, axis=1), jnp.concatenate(parts_s, axis=1)


def _proj0_kernel(x_ref, w_ref, cos_ref, sin_ref, qn_ref, kn_ref, o_ref, *, tn):
    j = pl.program_id(1)
    h = _dot(x_ref[...], w_ref[...])
    nsub = tn // HEAD_DIM
    scale = HEAD_DIM ** -0.5 * LOG2E
    k_tile = A_Q // tn
    bq_lo = (A_Q + A_KV) // tn
    bq_hi = bq_lo + B_QK // tn

    def normed(c, w):
        y = _rms(h[:, c * HEAD_DIM:(c + 1) * HEAD_DIM], w)
        return _rope(y, cos_ref[...], sin_ref[...], HEAD_DIM // 4)

    @pl.when(j < k_tile)
    def _():
        for c in range(nsub):
            o_ref[c] = (normed(c, qn_ref[...]) * scale).astype(o_ref.dtype)

    @pl.when(j == k_tile)
    def _():
        for c in range(nsub):
            o_ref[c] = normed(c, kn_ref[...]).astype(o_ref.dtype)

    @pl.when(j > k_tile)
    def _():
        mul = jnp.where((j >= bq_lo) & (j < bq_hi), scale, 1.0).astype(F32)
        for c in range(nsub):
            o_ref[c] = (h[:, c * HEAD_DIM:(c + 1) * HEAD_DIM] * mul).astype(o_ref.dtype)


def _proj0(x_bf, w_bf, cos, sin, qn, kn, *, tm=512, tn=256):
    S, K = x_bf.shape
    N = w_bf.shape[1]
    assert A_Q % tn == 0 and tn == A_KV and N % tn == 0
    return pl.pallas_call(
        functools.partial(_proj0_kernel, tn=tn),
        grid=(S // tm, N // tn),
        in_specs=[
            pl.BlockSpec((tm, K), lambda i, j: (i, 0)),
            pl.BlockSpec((K, tn), lambda i, j: (0, j)),
            pl.BlockSpec((tm, HEAD_DIM), lambda i, j: (i, 0)),
            pl.BlockSpec((tm, HEAD_DIM), lambda i, j: (i, 0)),
            pl.BlockSpec((1, HEAD_DIM), lambda i, j: (0, 0)),
            pl.BlockSpec((1, HEAD_DIM), lambda i, j: (0, 0)),
        ],
        out_specs=pl.BlockSpec((tn // HEAD_DIM, tm, HEAD_DIM), lambda i, j: (j, i, 0)),
        out_shape=jax.ShapeDtypeStruct((N // HEAD_DIM, S, HEAD_DIM), BF16),
        compiler_params=_cparams(("parallel", "parallel")),
        name="proj0",
    )(x_bf, w_bf, cos, sin, qn, kn)


def _values_t_kernel(wt_ref, x_ref, o_ref):
    o_ref[...] = _dot_nt(wt_ref[...], x_ref[...]).astype(o_ref.dtype)


def _values_t(w, x_bf, *, tm=512, tn=256, name="values_t"):
    S, K = x_bf.shape
    N = w.shape[1]
    return pl.pallas_call(
        _values_t_kernel,
        grid=(S // tm, N // tn),
        in_specs=[pl.BlockSpec((tn, K), lambda i, j: (j, 0)),
                  pl.BlockSpec((tm, K), lambda i, j: (i, 0))],
        out_specs=pl.BlockSpec((tn, tm), lambda i, j: (j, i)),
        out_shape=jax.ShapeDtypeStruct((N, S), BF16),
        compiler_params=_cparams(("parallel", "parallel")),
        name=name,
    )(w.T.astype(BF16), x_bf)


def _online_softmax_pv(s, vt, m_ref, l_ref, acc_ref, shift=None):
    m_prev = m_ref[...]
    m_chunk = jnp.max(s, axis=0, keepdims=True)
    if shift is None:
        m_new = jnp.maximum(m_prev, m_chunk)
        p = jnp.exp2(s - m_new)
    else:
        m_new = jnp.maximum(m_prev, m_chunk + shift)
        p = jnp.exp2(s - (m_new - shift))
    alpha = jnp.exp2(m_prev - m_new)
    l_ref[...] = alpha * l_ref[...] + jnp.sum(p, axis=0, keepdims=True)
    acc_ref[...] = alpha * acc_ref[...] + _dot(vt, p.astype(BF16))
    m_ref[...] = m_new


def _flash_kernel(q_ref, k_ref, vt_ref, o_ref, qt_sc, s_sc, m_sc, l_sc, acc_sc, *, kc):
    R, bq, D = q_ref.shape
    S = k_ref.shape[1]
    Dv = vt_ref.shape[0]
    n = S // kc
    assert n % 2 == 0
    qt_sc[...] = q_ref[...].reshape(R * bq, D).astype(F32).T.astype(BF16)
    m_sc[...] = jnp.full(m_sc.shape, -jnp.inf, F32)
    l_sc[...] = jnp.zeros(l_sc.shape, F32)
    acc_sc[...] = jnp.zeros(acc_sc.shape, F32)

    def scores(c):
        off = pl.multiple_of(c * kc, kc)
        return _dot(k_ref[0, pl.ds(off, kc), :], qt_sc[...])

    def values_t(c):
        return vt_ref[:, pl.ds(pl.multiple_of(c * kc, kc), kc)]

    s_sc[0] = scores(0)

    def body(i, carry):
        c = 2 * i
        s_sc[1] = scores(c + 1)
        _online_softmax_pv(s_sc[0], values_t(c), m_sc, l_sc, acc_sc)
        s_sc[0] = scores(jnp.minimum(c + 2, n - 1))
        _online_softmax_pv(s_sc[1], values_t(c + 1), m_sc, l_sc, acc_sc)
        return carry

    lax.fori_loop(0, n // 2, body, 0)
    out = (acc_sc[...] / l_sc[...]).T
    for r in range(R):
        o_ref[:, r * Dv:(r + 1) * Dv] = out[r * bq:(r + 1) * bq].astype(o_ref.dtype)


def _flash(q_arr, k_arr, vt_arr, *, n_kv, rep, q_tile0, k_tile0, dv, bq, kc, name):
    _, S, D = q_arr.shape
    kc = min(kc, S // 2)
    bq = min(bq, S)
    assert q_tile0 % rep == 0
    N = rep * bq
    rows = dv
    return pl.pallas_call(
        functools.partial(_flash_kernel, kc=kc),
        grid=(n_kv, S // bq),
        in_specs=[
            pl.BlockSpec((rep, bq, D), lambda g, i: (q_tile0 // rep + g, i, 0)),
            pl.BlockSpec((1, S, D), lambda g, i: (k_tile0 + g, 0, 0)),
            pl.BlockSpec((rows, vt_arr.shape[1]), lambda g, i: (g, 0)),
        ],
        out_specs=pl.BlockSpec((bq, rep * dv), lambda g, i: (i, g)),
        out_shape=jax.ShapeDtypeStruct((S, n_kv * rep * dv), BF16),
        scratch_shapes=[pltpu.VMEM((D, N), BF16),
                        pltpu.VMEM((2, kc, N), F32),
                        pltpu.VMEM((1, N), F32), pltpu.VMEM((1, N), F32), pltpu.VMEM((rows, N), F32)],
        compiler_params=_cparams(("parallel", "parallel")),
        name=name,
    )(q_arr, k_arr, vt_arr)


def _diff_kernel(q_ref, k_ref, vt_ref, lq1_ref, lk1_ref, lq2_ref, lk2_ref, subln_ref, o_ref,
                 qt_sc, s_sc, m_sc, l_sc, acc_sc, *, kc, lam_init):
    _, bq, D = q_ref.shape
    S = k_ref.shape[1]
    h = pl.program_id(0)
    q0 = pl.program_id(1) * bq
    slope = jnp.float32(2.0 ** (-8.0 * DIFF_HEADS / DIFF_HEADS))
    for hh in range(DIFF_HEADS - 1):
        slope = jnp.where(h == hh, jnp.float32(2.0 ** (-8.0 * (hh + 1) / DIFF_HEADS)), slope)
    slope2 = slope * LOG2E
    rel = (lax.broadcasted_iota(I32, (kc, bq), 0) - lax.broadcasted_iota(I32, (kc, bq), 1)).astype(F32)
    rel_s = rel * slope2
    for j in range(2):
        qt_sc[j] = q_ref[j].astype(F32).T.astype(BF16)
    m_sc[...] = jnp.full(m_sc.shape, -jnp.inf, F32)
    l_sc[...] = jnp.zeros(l_sc.shape, F32)
    acc_sc[...] = jnp.zeros(acc_sc.shape, F32)

    n = S // kc

    def scores(j, c):
        off = pl.multiple_of(c * kc, kc)
        return _dot(k_ref[j, pl.ds(off, kc), :], qt_sc[j])

    s_sc[0] = scores(0, 0)

    def chunk_step(c, penalty):
        off = pl.multiple_of(c * kc, kc)
        delta = (off - q0).astype(F32)
        vt = vt_ref[:, pl.ds(off, kc)]
        s_sc[1] = scores(1, c)
        s0, shift = penalty(s_sc[0], delta)
        _online_softmax_pv(s0, vt, m_sc.at[0], l_sc.at[0], acc_sc.at[0], shift)
        s_sc[0] = scores(0, jnp.minimum(c + 1, n - 1))
        s1, shift = penalty(s_sc[1], delta)
        _online_softmax_pv(s1, vt, m_sc.at[1], l_sc.at[1], acc_sc.at[1], shift)

    def keys_before(c, carry):
        chunk_step(c, lambda s, delta: (s + rel_s, slope2 * delta))
        return carry

    def keys_overlap(c, carry):
        chunk_step(c, lambda s, delta: (s - jnp.abs(rel + delta) * slope2, None))
        return carry

    def keys_after(c, carry):
        chunk_step(c, lambda s, delta: (s - rel_s, -(slope2 * delta)))
        return carry

    c_lo = q0 // kc
    c_hi = (q0 + bq + kc - 1) // kc
    lax.fori_loop(0, c_lo, keys_before, 0)
    lax.fori_loop(c_lo, c_hi, keys_overlap, 0)
    lax.fori_loop(c_hi, n, keys_after, 0)
    lam = (jnp.exp(jnp.sum(lq1_ref[...] * lk1_ref[...], axis=1, keepdims=True))
           - jnp.exp(jnp.sum(lq2_ref[...] * lk2_ref[...], axis=1, keepdims=True)) + lam_init)
    out = (acc_sc[0] / l_sc[0] - lam * (acc_sc[1] / l_sc[1])).T
    o_ref[...] = (_rms(out, subln_ref[...]) * (1.0 - lam_init)).astype(o_ref.dtype)


def _diff_attention(hq, vt_arr, lq1, lk1, lq2, lk2, subln, *, q_tile0, k_tile0, lam_init,
                    bq=512, kc=1024):
    _, S, D = hq.shape
    kc = min(kc, S)
    bq = min(bq, S)
    rows = 2 * D
    vec = pl.BlockSpec((1, D), lambda h, i: (0, 0))
    return pl.pallas_call(
        functools.partial(_diff_kernel, kc=kc, lam_init=lam_init),
        grid=(DIFF_HEADS, S // bq),
        in_specs=[
            pl.BlockSpec((2, bq, D), lambda h, i: (q_tile0 // 2 + h, i, 0)),
            pl.BlockSpec((2, S, D), lambda h, i: (k_tile0 // 2 + h, 0, 0)),
            pl.BlockSpec((rows, vt_arr.shape[1]), lambda h, i: (h, 0)),
            vec, vec, vec, vec,
            pl.BlockSpec((1, 2 * D), lambda h, i: (0, 0)),
        ],
        out_specs=pl.BlockSpec((bq, 2 * D), lambda h, i: (i, h)),
        out_shape=jax.ShapeDtypeStruct((S, DIFF_HEADS * 2 * D), BF16),
        scratch_shapes=[pltpu.VMEM((2, D, bq), BF16), pltpu.VMEM((2, kc, bq), F32),
                        pltpu.VMEM((2, 1, bq), F32), pltpu.VMEM((2, 1, bq), F32),
                        pltpu.VMEM((2, rows, bq), F32)],
        compiler_params=_cparams(("parallel", "parallel")),
        name="diff_attention",
    )(hq, hq, vt_arr, lq1, lk1, lq2, lk2, subln)


def _wout_kernel(*refs, n_in):
    a_refs = refs[:n_in]
    w_refs = refs[n_in:2 * n_in]
    x_ref, g_ref, b_ref, wr_ref, xo_ref, xb_ref, lg_ref = refs[2 * n_in:]
    mix = _dot(a_refs[0][...], w_refs[0][...])
    for a, w in zip(a_refs[1:], w_refs[1:]):
        mix = mix + _dot(a[...], w[...])
    y = _layer_norm(DEEPNORM_ALPHA * x_ref[...] + mix, g_ref[...], b_ref[...])
    xo_ref[...] = y
    y_hi = y.astype(BF16)
    xb_ref[...] = y_hi
    y_lo = (y - y_hi.astype(F32)).astype(BF16)
    wr = wr_ref[...]
    w_hi = wr.astype(BF16)
    w_lo = (wr - w_hi.astype(F32)).astype(BF16)
    E = wr.shape[0]
    both = _dot_nt(jnp.concatenate([w_hi, w_lo], axis=0), y_hi)
    lg_ref[...] = both[:E] + both[E:] + _dot_nt(w_hi, y_lo)


def _wout_ln_router(a_list, w_list, x, g, b, wr_t, *, tm=256):
    S, D = x.shape
    n_in = len(a_list)
    E = wr_t.shape[0]
    in_specs = ([pl.BlockSpec((tm, a.shape[1]), lambda i: (i, 0)) for a in a_list]
                + [pl.BlockSpec(w.shape, lambda i: (0, 0)) for w in w_list]
                + [pl.BlockSpec((tm, D), lambda i: (i, 0)),
                   pl.BlockSpec((1, D), lambda i: (0, 0)),
                   pl.BlockSpec((1, D), lambda i: (0, 0)),
                   pl.BlockSpec((E, D), lambda i: (0, 0))])
    return pl.pallas_call(
        functools.partial(_wout_kernel, n_in=n_in),
        grid=(S // tm,),
        in_specs=in_specs,
        out_specs=[pl.BlockSpec((tm, D), lambda i: (i, 0)),
                   pl.BlockSpec((tm, D), lambda i: (i, 0)),
                   pl.BlockSpec((E, tm), lambda i: (0, i))],
        out_shape=[jax.ShapeDtypeStruct((S, D), F32),
                   jax.ShapeDtypeStruct((S, D), BF16),
                   jax.ShapeDtypeStruct((E, S), F32)],
        compiler_params=_cparams(("parallel",)),
        name="wout_ln_router",
    )(*a_list, *w_list, x, g, b, wr_t)


def _select_kernel(lg_ref, pos_ref, gate_ref, off_ref, cnt_ref, *, E, nb, cap):
    lg = lg_ref[...].reshape(E, nb, LANES)
    mx = jnp.max(lg, axis=0, keepdims=True)
    ex = jnp.exp(lg - mx)
    aff = ex / jnp.sum(ex, axis=0, keepdims=True)
    bits = lax.bitcast_convert_type(aff, I32)

    def count(msk):
        c = jnp.sum(msk.astype(F32), axis=1, keepdims=True)
        return jnp.sum(c, axis=2, keepdims=True)

    def search(i, thr):
        cand = thr | lax.shift_left(jnp.int32(1), 30 - i)
        return jnp.where(count(bits >= cand) >= cap, cand, thr)

    thr = lax.fori_loop(0, 31, search, jnp.zeros((E, 1, 1), I32))
    gt = bits > thr
    eq = bits == thr
    need = cap - count(gt)

    col = lax.broadcasted_iota(I32, (LANES, LANES), 1)
    rw = lax.broadcasted_iota(I32, (LANES, LANES), 0)
    upper = (rw < col).astype(BF16)
    ones = jnp.ones((LANES, LANES), BF16)
    lower = (lax.broadcasted_iota(I32, (nb, nb), 1) < lax.broadcasted_iota(I32, (nb, nb), 0)).astype(BF16)

    def prefix(msk):
        m2 = jnp.where(msk, 1.0, 0.0).reshape(E * nb, LANES).astype(BF16)
        within = _dot(m2, upper).reshape(E, nb, LANES)
        tot = _dot(m2, ones).reshape(E, nb, LANES)
        offs = jnp.stack([_dot(lower, tot[e].astype(BF16)) for e in range(E)], axis=0)
        return within + offs, offs, tot

    eq_rank, _, _ = prefix(eq)
    sel = gt | (eq & (eq_rank < need))
    pos, offs, tot = prefix(sel)
    pos_ref[...] = jnp.where(sel, pos, -1.0).astype(I32).reshape(E * nb, LANES)
    gate_ref[...] = jnp.where(sel, aff, 0.0).reshape(E * nb, LANES)
    off_ref[...] = offs.astype(I32).reshape(E * nb, LANES)
    cnt_ref[...] = tot.astype(I32).reshape(E * nb, LANES)


def _select(logits_t, cap):
    E, S = logits_t.shape
    nb = S // LANES
    shp = (E * nb, LANES)
    full = pl.BlockSpec(shp, lambda: (0, 0))
    return pl.pallas_call(
        functools.partial(_select_kernel, E=E, nb=nb, cap=cap),
        in_specs=[full],
        out_specs=[full, full, full, full],
        out_shape=[jax.ShapeDtypeStruct(shp, I32), jax.ShapeDtypeStruct(shp, F32),
                   jax.ShapeDtypeStruct(shp, I32), jax.ShapeDtypeStruct(shp, I32)],
        compiler_params=pltpu.CompilerParams(vmem_limit_bytes=VMEM_LIMIT),
        name="select",
    )(logits_t.reshape(shp))


def _gather_kernel(off_s, cnt_s, x_ref, pos_ref, gate_ref, xe_ref, gs_ref, *, cap, tbs, ep):
    grp = pl.program_id(0)
    t = pl.program_id(1)

    @pl.when(t == 0)
    def _():
        xe_ref[...] = jnp.zeros(xe_ref.shape, xe_ref.dtype)
        gs_ref[...] = jnp.zeros(gs_ref.shape, gs_ref.dtype)

    def place(k, u, e, tb, width):
        base = jnp.minimum((off_s[e, tb] // BF16_ROWS) * BF16_ROWS, cap - width)
        base = pl.multiple_of(base, BF16_ROWS)
        slot = base + lax.broadcasted_iota(I32, (width, LANES), 0)
        hit = pos_ref[k, u] == slot
        rows = _dot(hit.astype(BF16), x_ref[pl.ds(u * LANES, LANES), :])
        win = pl.ds(base, width)
        xe_ref[k, win, :] = xe_ref[k, win, :] + rows.astype(xe_ref.dtype)
        g = jnp.sum(jnp.where(hit, gate_ref[k, u], 0.0), axis=1, keepdims=True)
        gs_ref[k, win, :] = gs_ref[k, win, :] + jnp.broadcast_to(g, (width, LANES))

    for u in range(tbs):
        for k in range(ep):
            e = grp * ep + k
            tb = t * tbs + u
            cnt = cnt_s[e, tb]

            @pl.when((cnt > 0) & (cnt <= GATHER_SMALL))
            def _():
                place(k, u, e, tb, GATHER_SMALL + BF16_ROWS)

            @pl.when(cnt > GATHER_SMALL)
            def _():
                place(k, u, e, tb, GATHER_WIN)


def _gather(off_s, cnt_s, x_bf, pos, gate, cap, *, tbs=8, ep=2):
    S, D = x_bf.shape
    E, nb = off_s.shape
    assert cap >= GATHER_WIN and cap % BF16_ROWS == 0 and nb % tbs == 0 and E % ep == 0
    pos4 = pos.reshape(E, nb, 1, LANES)
    gate4 = gate.reshape(E, nb, 1, LANES)
    return pl.pallas_call(
        functools.partial(_gather_kernel, cap=cap, tbs=tbs, ep=ep),
        grid_spec=pltpu.PrefetchScalarGridSpec(
            num_scalar_prefetch=2,
            grid=(E // ep, nb // tbs),
            in_specs=[
                pl.BlockSpec((tbs * LANES, D), lambda g, t, o, c: (t, 0)),
                pl.BlockSpec((ep, tbs, 1, LANES), lambda g, t, o, c: (g, t, 0, 0)),
                pl.BlockSpec((ep, tbs, 1, LANES), lambda g, t, o, c: (g, t, 0, 0)),
            ],
            out_specs=[pl.BlockSpec((ep, cap, D), lambda g, t, o, c: (g, 0, 0)),
                       pl.BlockSpec((ep, cap, LANES), lambda g, t, o, c: (g, 0, 0))],
        ),
        out_shape=[jax.ShapeDtypeStruct((E, cap, D), BF16),
                   jax.ShapeDtypeStruct((E, cap, LANES), F32)],
        compiler_params=_cparams(("parallel", "arbitrary")),
        name="gather",
    )(off_s, cnt_s, x_bf, pos4, gate4)


def _ffn_kernel(xe_ref, wg_ref, wu_ref, wd_ref, gs_ref, ye_ref, acc_sc, *, sub):
    fc = pl.program_id(2)
    rows = xe_ref.shape[1]
    @pl.when(fc == 0)
    def _():
        acc_sc[...] = jnp.zeros(acc_sc.shape, F32)

    wg = wg_ref[...].astype(BF16)
    wu = wu_ref[...].astype(BF16)
    wd = wd_ref[...].astype(BF16)
    for r in range(rows // sub):
        sl = pl.ds(r * sub, sub)
        xt = xe_ref[0, sl, :]
        hg = _dot(xt, wg)
        hu = _dot(xt, wu)
        hid = (hg * jax.nn.sigmoid(hg) * hu).astype(BF16)
        acc_sc[sl, :] = acc_sc[sl, :] + _dot(hid, wd)

    @pl.when(fc == pl.num_programs(2) - 1)
    def _():
        ye_ref[0] = (acc_sc[...] * gs_ref[0][:, :1]).astype(ye_ref.dtype)


def _ffn(xe, gs, w_gate, w_up, w_down, layer, *, halves=2, fcw=256):
    E, cap, D = xe.shape
    F = w_gate.shape[-1]
    rows = cap // halves
    sub = min(256, rows)
    return pl.pallas_call(
        functools.partial(_ffn_kernel, sub=sub),
        grid=(E, halves, F // fcw),
        in_specs=[
            pl.BlockSpec((1, rows, D), lambda e, t, f: (e, t, 0)),
            pl.BlockSpec((None, None, D, fcw), lambda e, t, f: (layer, e, 0, f)),
            pl.BlockSpec((None, None, D, fcw), lambda e, t, f: (layer, e, 0, f)),
            pl.BlockSpec((None, None, fcw, D), lambda e, t, f: (layer, e, f, 0)),
            pl.BlockSpec((1, rows, LANES), lambda e, t, f: (e, t, 0)),
        ],
        out_specs=pl.BlockSpec((1, rows, D), lambda e, t, f: (e, t, 0)),
        out_shape=jax.ShapeDtypeStruct((E, cap, D), BF16),
        scratch_shapes=[pltpu.VMEM((rows, D), F32)],
        compiler_params=_cparams(("parallel", "parallel", "arbitrary")),
        name="expert_ffn",
    )(xe, w_gate, w_up, w_down, gs)


def _combine_kernel(win_s, *refs, eg):
    y_refs = refs[:eg]
    post_ref, x_ref, g_ref, b_ref, xo_ref, xb_ref, acc_sc = refs[eg:]
    tb = pl.program_id(0)
    grp = pl.program_id(1)

    @pl.when(grp == 0)
    def _():
        acc_sc[...] = jnp.zeros(acc_sc.shape, F32)

    pt = post_ref[...].astype(F32)
    lane = lax.broadcasted_iota(I32, pt.shape, 1)
    col = lax.broadcasted_iota(I32, (LANES, GATHER_WIN), 1)
    total = None
    for k in range(eg):
        e = grp * eg + k
        pcol = jnp.sum(jnp.where(lane == e, pt, 0.0), axis=1, keepdims=True).astype(I32)
        hit = ((pcol - win_s[e, tb]) == col).astype(BF16)
        part = _dot(hit, y_refs[k][0])
        total = part if total is None else total + part
    acc_sc[...] = acc_sc[...] + total

    @pl.when(grp == pl.num_programs(1) - 1)
    def _():
        y = _layer_norm(DEEPNORM_ALPHA * x_ref[...] + acc_sc[...], g_ref[...], b_ref[...])
        xo_ref[...] = y
        xb_ref[...] = y.astype(BF16)


def _combine_ln(win_s, ye, pos_t, x, g, b, *, eg=16):
    S, D = x.shape
    E, cap, _ = ye.shape
    nb = S // LANES
    assert E % eg == 0

    def y_spec(k):
        return pl.BlockSpec((pl.Element(1), pl.Element(GATHER_WIN), pl.Element(D)),
                            lambda t, gr, ws: (gr * eg + k, (ws[gr * eg + k, t] // BF16_ROWS) * BF16_ROWS, 0))

    row = lambda t, gr, ws: (t, 0)
    fixed = lambda t, gr, ws: (0, 0)
    return pl.pallas_call(
        functools.partial(_combine_kernel, eg=eg),
        grid_spec=pltpu.PrefetchScalarGridSpec(
            num_scalar_prefetch=1,
            grid=(nb, E // eg),
            in_specs=([y_spec(k) for k in range(eg)]
                      + [pl.BlockSpec((LANES, E), row), pl.BlockSpec((LANES, D), row),
                         pl.BlockSpec((1, D), fixed), pl.BlockSpec((1, D), fixed)]),
            out_specs=[pl.BlockSpec((LANES, D), row), pl.BlockSpec((LANES, D), row)],
            scratch_shapes=[pltpu.VMEM((LANES, D), F32)],
        ),
        out_shape=[jax.ShapeDtypeStruct((S, D), F32), jax.ShapeDtypeStruct((S, D), BF16)],
        compiler_params=_cparams(("parallel", "arbitrary")),
        name="combine_ln",
    )(win_s, *([ye] * eg), pos_t, x, g, b)


def _moe(x, x_bf, logits_t, w_gate, w_up, w_down, layer, g, b):
    S, D = x.shape
    E = logits_t.shape[0]
    nb = S // LANES
    cap = EC_FACTOR * S // E
    pos, gate, off, cnt = _select(logits_t, cap)
    off_s = off[:, 0].reshape(E, nb)
    cnt_s = cnt[:, 0].reshape(E, nb)
    xe, gs = _gather(off_s, cnt_s, x_bf, pos, gate, cap)
    ye = _ffn(xe, gs, w_gate, w_up, w_down, layer)
    pos_t = pos.reshape(E, S).T
    win_s = jnp.minimum((off_s // BF16_ROWS) * BF16_ROWS, cap - GATHER_WIN)
    return _combine_ln(win_s, ye, pos_t, x, g, b)


def _mla_in_kernel(x_ref, w_ref, cos_ref, sin_ref, qn_ref, kvn_ref, cq_ref, ckv_ref, kr_ref):
    h = _dot(x_ref[...], w_ref[...])
    cq_ref[...] = _rms(h[:, :MLA_Q_RANK], qn_ref[...]).astype(BF16)
    ckv_ref[...] = _rms(h[:, MLA_Q_RANK:MLA_Q_RANK + MLA_KV_RANK], kvn_ref[...]).astype(BF16)
    kr = h[:, MLA_Q_RANK + MLA_KV_RANK:]
    kr_ref[...] = _rope(kr, cos_ref[...], sin_ref[...], MLA_ROPE // 4).astype(BF16)


def _mla_in(x_bf, w_bf, cos, sin, qn, kvn, *, tm=512):
    S, K = x_bf.shape
    N = w_bf.shape[1]
    row = lambda i: (i, 0)
    fixed = lambda i: (0, 0)
    return pl.pallas_call(
        _mla_in_kernel,
        grid=(S // tm,),
        in_specs=[pl.BlockSpec((tm, K), row), pl.BlockSpec((K, N), fixed),
                  pl.BlockSpec((tm, LANES), row), pl.BlockSpec((tm, LANES), row),
                  pl.BlockSpec((1, MLA_Q_RANK), fixed), pl.BlockSpec((1, MLA_KV_RANK), fixed)],
        out_specs=[pl.BlockSpec((tm, MLA_Q_RANK), row), pl.BlockSpec((tm, MLA_KV_RANK), row),
                   pl.BlockSpec((tm, LANES), row)],
        out_shape=[jax.ShapeDtypeStruct((S, MLA_Q_RANK), BF16),
                   jax.ShapeDtypeStruct((S, MLA_KV_RANK), BF16),
                   jax.ShapeDtypeStruct((S, LANES), BF16)],
        compiler_params=_cparams(("parallel",)),
        name="mla_in",
    )(x_bf, w_bf, cos, sin, qn, kvn)


def _mla_q_kernel(c_ref, w_ref, cos_ref, sin_ref, o_ref, *, heads):
    h = _dot(c_ref[...], w_ref[...])
    scale = (MLA_NOPE + MLA_ROPE) ** -0.5 * LOG2E
    for c in range(heads):
        base = c * MLA_QK_PAD
        o_ref[c, :, :MLA_NOPE] = (h[:, base:base + MLA_NOPE] * scale).astype(BF16)
        rp = _rope(h[:, base + MLA_NOPE:base + MLA_QK_PAD], cos_ref[...], sin_ref[...], MLA_ROPE // 4)
        o_ref[c, :, MLA_NOPE:] = (rp * scale).astype(BF16)


def _mla_q(cq, w_bf, cos, sin, *, tm=512, heads=2):
    S, K = cq.shape
    tn = heads * MLA_QK_PAD
    return pl.pallas_call(
        functools.partial(_mla_q_kernel, heads=heads),
        grid=(S // tm, MLA_HEADS // heads),
        in_specs=[pl.BlockSpec((tm, K), lambda i, j: (i, 0)),
                  pl.BlockSpec((K, tn), lambda i, j: (0, j)),
                  pl.BlockSpec((tm, LANES), lambda i, j: (i, 0)),
                  pl.BlockSpec((tm, LANES), lambda i, j: (i, 0))],
        out_specs=pl.BlockSpec((heads, tm, MLA_QK_PAD), lambda i, j: (j, i, 0)),
        out_shape=jax.ShapeDtypeStruct((MLA_HEADS, S, MLA_QK_PAD), BF16),
        compiler_params=_cparams(("parallel", "parallel")),
        name="mla_q",
    )(cq, w_bf, cos, sin)


def _mla_k_kernel(c_ref, wk_ref, kr_ref, k_ref, *, heads):
    kn = _dot(c_ref[...], wk_ref[...])
    for hh in range(heads):
        k_ref[hh, :, :MLA_NOPE] = kn[:, hh * MLA_NOPE:(hh + 1) * MLA_NOPE].astype(BF16)
        k_ref[hh, :, MLA_NOPE:] = kr_ref[...]


def _mla_k(ckv, wk_bf, kr, *, tm=512, heads=4):
    S, K = ckv.shape
    return pl.pallas_call(
        functools.partial(_mla_k_kernel, heads=heads),
        grid=(S // tm, MLA_HEADS // heads),
        in_specs=[pl.BlockSpec((tm, K), lambda i, j: (i, 0)),
                  pl.BlockSpec((K, heads * MLA_NOPE), lambda i, j: (0, j)),
                  pl.BlockSpec((tm, LANES), lambda i, j: (i, 0))],
        out_specs=pl.BlockSpec((heads, tm, MLA_QK_PAD), lambda i, j: (j, i, 0)),
        out_shape=jax.ShapeDtypeStruct((MLA_HEADS, S, MLA_QK_PAD), BF16),
        compiler_params=_cparams(("parallel", "parallel")),
        name="mla_k",
    )(ckv, wk_bf, kr)


def kernel(x, ab_w_in, ab_q_norm, ab_k_norm, ab_lambda_q1, ab_lambda_k1, ab_lambda_q2, ab_lambda_k2, ab_subln, ab_w_out, mla_w_in, mla_q_norm, mla_kv_norm, mla_w_uq, mla_w_ukv, mla_w_out, ln_mix_g, ln_mix_b, moe_w_router, moe_w_gate, moe_w_up, moe_w_down, ln_ffn_g, ln_ffn_b):
    B, S, D = x.shape
    assert B == 1 and D == D_MODEL and S % 512 == 0
    xf = x.reshape(S, D)
    x_bf = xf.astype(BF16)
    cos_a, sin_a = _rope_tables(S, HEAD_DIM // 4, HEAD_DIM)
    cos_c, sin_c = _rope_tables(S, MLA_ROPE // 4, LANES)
    row2 = lambda v: v.reshape(1, -1)

    lam_init = 0.8 - 0.6 * math.exp(-0.3 * 0)
    w_in = ab_w_in[0].astype(BF16)
    c_av = A_Q + A_KV
    c_bq = c_av + A_KV
    c_bv = c_bq + 2 * B_QK
    w_qk = jnp.concatenate([w_in[:, :c_av], w_in[:, c_bq:c_bv]], axis=1)
    hq = _proj0(x_bf, w_qk, cos_a, sin_a, row2(ab_q_norm[0]), row2(ab_k_norm[0]))
    vt_a = _values_t(w_in[:, c_av:c_bq], x_bf, name="values_t_gqa")
    vt_b = _values_t(w_in[:, c_bv:], x_bf, name="values_t_diff")
    t_ak = A_Q // HEAD_DIM
    t_bq = t_ak + GQA_KV_HEADS
    t_bk = t_bq + 2 * DIFF_HEADS
    a_out = _flash(hq, hq, vt_a, n_kv=GQA_KV_HEADS, rep=GQA_Q_HEADS // GQA_KV_HEADS,
                   q_tile0=0, k_tile0=t_ak, dv=HEAD_DIM, bq=256, kc=2048, name="gqa_attention")
    b_out = _diff_attention(hq, vt_b, row2(ab_lambda_q1[0]), row2(ab_lambda_k1[0]),
                            row2(ab_lambda_q2[0]), row2(ab_lambda_k2[0]), row2(ab_subln[0]),
                            q_tile0=t_bq, k_tile0=t_bk, lam_init=lam_init)
    w_out = ab_w_out[0].astype(BF16)
    xf, x_bf, logits_t = _wout_ln_router(
        [a_out, b_out], [w_out[:A_Q], w_out[A_Q:]], xf, row2(ln_mix_g[0]), row2(ln_mix_b[0]),
        moe_w_router[0].T)
    xf, x_bf = _moe(xf, x_bf, logits_t, moe_w_gate, moe_w_up, moe_w_down, 0,
                    row2(ln_ffn_g[0]), row2(ln_ffn_b[0]))

    w1 = jnp.pad(mla_w_in[0], ((0, 0), (0, LANES - MLA_ROPE))).astype(BF16)
    cq, ckv, kr = _mla_in(x_bf, w1, cos_c, sin_c, row2(mla_q_norm[0]), row2(mla_kv_norm[0]))
    w_uq = mla_w_uq[0].reshape(MLA_Q_RANK, MLA_HEADS, MLA_NOPE + MLA_ROPE)
    w_uq = jnp.pad(w_uq, ((0, 0), (0, 0), (0, MLA_QK_PAD - MLA_NOPE - MLA_ROPE)))
    w_uq = w_uq.reshape(MLA_Q_RANK, MLA_HEADS * MLA_QK_PAD).astype(BF16)
    w_ukv = mla_w_ukv[0].reshape(MLA_KV_RANK, MLA_HEADS, MLA_NOPE + MLA_V)
    w_uk = w_ukv[:, :, :MLA_NOPE].reshape(MLA_KV_RANK, MLA_HEADS * MLA_NOPE).astype(BF16)
    w_uv = w_ukv[:, :, MLA_NOPE:].reshape(MLA_KV_RANK, MLA_HEADS * MLA_V)
    q_pad = _mla_q(cq, w_uq, cos_c, sin_c)
    k_pad = _mla_k(ckv, w_uk, kr)
    vt1 = _values_t(w_uv, ckv, name="values_t_mla")
    c_out = _flash(q_pad, k_pad, vt1, n_kv=MLA_HEADS, rep=1, q_tile0=0, k_tile0=0, dv=MLA_V,
                   bq=512, kc=2048, name="mla_attention")
    xf, x_bf, logits_t = _wout_ln_router(
        [c_out], [mla_w_out[0].astype(BF16)], xf, row2(ln_mix_g[1]), row2(ln_mix_b[1]),
        moe_w_router[1].T)
    xf, x_bf = _moe(xf, x_bf, logits_t, moe_w_gate, moe_w_up, moe_w_down, 1,
                    row2(ln_ffn_g[1]), row2(ln_ffn_b[1]))
    return xf.reshape(B, S, D)
```

```python
import functools
import math

import jax
import jax.numpy as jnp
from jax import lax
from jax.experimental import pallas as pl
from jax.experimental.pallas import tpu as pltpu

F32 = jnp.float32
BF16 = jnp.bfloat16
I32 = jnp.int32

D_MODEL = 2048
DEPTH = 2
GRID_W = 64
ROPE_THETA = 10000.0
NORM_EPS = 1e-6
LN_EPS = 1e-5

HEAD_DIM = 128
GQA_Q_HEADS = 8
GQA_KV_HEADS = 2
DIFF_HEADS = 4
A_Q = GQA_Q_HEADS * HEAD_DIM
A_KV = GQA_KV_HEADS * HEAD_DIM
B_QK = DIFF_HEADS * 2 * HEAD_DIM
B_V = DIFF_HEADS * 2 * HEAD_DIM

MLA_HEADS = 16
MLA_Q_RANK = 512
MLA_KV_RANK = 512
MLA_NOPE = 128
MLA_ROPE = 64
MLA_V = 128
MLA_QK_PAD = 256

N_EXPERTS = 16
EXPERT_FF = 2048
EC_FACTOR = 2

DEEPNORM_ALPHA = (2 * DEPTH) ** 0.25
LOG2E = math.log2(math.e)

LANES = 128
BF16_ROWS = 16
GATHER_WIN = LANES + BF16_ROWS
GATHER_SMALL = 32
VMEM_LIMIT = 56 * 1024 * 1024


def _cparams(sem, vmem=VMEM_LIMIT, flags=None):
    return pltpu.CompilerParams(dimension_semantics=sem, vmem_limit_bytes=vmem, flags=flags)


def _dot(a, b):
    return jnp.dot(a, b, preferred_element_type=F32)


def _dot_nt(a, b):
    return lax.dot_general(a, b, (((1,), (1,)), ((), ())), preferred_element_type=F32)


def _rope(y, cos, sin, half):
    n = y.shape[-1]
    lane = lax.broadcasted_iota(I32, y.shape, 1)
    up = pltpu.roll(y, n - half, 1)
    dn = pltpu.roll(y, half, 1)
    partner = jnp.where((lane % (2 * half)) < half, up, dn)
    return y * cos + partner * sin


def _rms(y, w):
    return y * lax.rsqrt(jnp.mean(y * y, axis=-1, keepdims=True) + NORM_EPS) * w


def _layer_norm(y, g, b):
    mu = jnp.mean(y, axis=-1, keepdims=True)
    d = y - mu
    var = jnp.mean(d * d, axis=-1, keepdims=True)
    return d * lax.rsqrt(var + LN_EPS) * g + b


def _rope_tables(S, half, width):
    n_rows = S // GRID_W
    n = 2 * half
    freqs = ROPE_THETA ** (-jnp.arange(0, n, 2, dtype=F32) / n)[None, :]
    parts_c, parts_s = [], []
    for count, along_rows in ((n_rows, True), (GRID_W, False)):
        ang = jnp.arange(count, dtype=F32)[:, None] * freqs
        c = jnp.concatenate([jnp.cos(ang)] * 2, axis=1)
        s = jnp.concatenate([-jnp.sin(ang), jnp.sin(ang)], axis=1)
        shape = (n_rows, GRID_W, n)
        expand = (lambda a: a[:, None, :]) if along_rows else (lambda a: a[None, :, :])
        parts_c.append(jnp.broadcast_to(expand(c), shape).reshape(S, n))
        parts_s.append(jnp.broadcast_to(expand(s), shape).reshape(S, n))
    pad = width - 4 * half
    if pad:
        parts_c.append(jnp.ones((S, pad), F32))
        parts_s.append(jnp.zeros((S, pad), F32))
    return jnp.concatenate(parts_c, axis=1), jnp.concatenate(parts_s, axis=1)


def _proj0_kernel(x_ref, w_ref, cos_ref, sin_ref, qn_ref, kn_ref, o_ref, *, tn):
    j = pl.program_id(1)
    h = _dot(x_ref[...], w_ref[...])
    nsub = tn // HEAD_DIM
    scale = HEAD_DIM ** -0.5 * LOG2E
    k_tile = A_Q // tn
    bq_lo = (A_Q + A_KV) // tn
    bq_hi = bq_lo + B_QK // tn

    def normed(c, w):
        y = _rms(h[:, c * HEAD_DIM:(c + 1) * HEAD_DIM], w)
        return _rope(y, cos_ref[...], sin_ref[...], HEAD_DIM // 4)

    @pl.when(j < k_tile)
    def _():
        for c in range(nsub):
            o_ref[c] = (normed(c, qn_ref[...]) * scale).astype(o_ref.dtype)

    @pl.when(j == k_tile)
    def _():
        for c in range(nsub):
            o_ref[c] = normed(c, kn_ref[...]).astype(o_ref.dtype)

    @pl.when(j > k_tile)
    def _():
        mul = jnp.where((j >= bq_lo) & (j < bq_hi), scale, 1.0).astype(F32)
        for c in range(nsub):
            o_ref[c] = (h[:, c * HEAD_DIM:(c + 1) * HEAD_DIM] * mul).astype(o_ref.dtype)


def _proj0(x_bf, w_bf, cos, sin, qn, kn, *, tm=1024, tn=256):
    S, K = x_bf.shape
    N = w_bf.shape[1]
    assert A_Q % tn == 0 and tn == A_KV and N % tn == 0
    return pl.pallas_call(
        functools.partial(_proj0_kernel, tn=tn),
        grid=(S // tm, N // tn),
        in_specs=[
            pl.BlockSpec((tm, K), lambda i, j: (i, 0)),
            pl.BlockSpec((K, tn), lambda i, j: (0, j)),
            pl.BlockSpec((tm, HEAD_DIM), lambda i, j: (i, 0)),
            pl.BlockSpec((tm, HEAD_DIM), lambda i, j: (i, 0)),
            pl.BlockSpec((1, HEAD_DIM), lambda i, j: (0, 0)),
            pl.BlockSpec((1, HEAD_DIM), lambda i, j: (0, 0)),
        ],
        out_specs=pl.BlockSpec((tn // HEAD_DIM, tm, HEAD_DIM), lambda i, j: (j, i, 0)),
        out_shape=jax.ShapeDtypeStruct((N // HEAD_DIM, S, HEAD_DIM), BF16),
        compiler_params=_cparams(("parallel", "parallel")),
        name="proj0",
    )(x_bf, w_bf, cos, sin, qn, kn)


def _values_t_kernel(wt_ref, x_ref, o_ref):
    o_ref[...] = _dot_nt(wt_ref[...], x_ref[...]).astype(o_ref.dtype)


def _values_t(w, x_bf, *, tm=1024, tn=256, name="values_t"):
    S, K = x_bf.shape
    N = w.shape[1]
    return pl.pallas_call(
        _values_t_kernel,
        grid=(S // tm, N // tn),
        in_specs=[pl.BlockSpec((tn, K), lambda i, j: (j, 0)),
                  pl.BlockSpec((tm, K), lambda i, j: (i, 0))],
        out_specs=pl.BlockSpec((tn, tm), lambda i, j: (j, i)),
        out_shape=jax.ShapeDtypeStruct((N, S), BF16),
        compiler_params=_cparams(("parallel", "parallel")),
        name=name,
    )(w.T.astype(BF16), x_bf)


def _online_softmax_pv(s, vt, m_ref, l_ref, acc_ref, shift=None):
    m_prev = m_ref[...]
    m_chunk = jnp.max(s, axis=0, keepdims=True)
    if shift is None:
        m_new = jnp.maximum(m_prev, m_chunk)
        p = jnp.exp2(s - m_new)
    else:
        m_new = jnp.maximum(m_prev, m_chunk + shift)
        p = jnp.exp2(s - (m_new - shift))
    alpha = jnp.exp2(m_prev - m_new)
    l_ref[...] = alpha * l_ref[...] + jnp.sum(p, axis=0, keepdims=True)
    acc_ref[...] = alpha * acc_ref[...] + _dot(vt, p.astype(BF16))
    m_ref[...] = m_new


def _flash_kernel(q_ref, k_ref, vt_ref, o_ref, qt_sc, s_sc, m_sc, l_sc, acc_sc, *, kc):
    R, bq, D = q_ref.shape
    S = k_ref.shape[1]
    Dv = vt_ref.shape[0]
    n = S // kc
    assert n % 2 == 0
    qt_sc[...] = q_ref[...].reshape(R * bq, D).astype(F32).T.astype(BF16)
    m_sc[...] = jnp.full(m_sc.shape, -jnp.inf, F32)
    l_sc[...] = jnp.zeros(l_sc.shape, F32)
    acc_sc[...] = jnp.zeros(acc_sc.shape, F32)

    def scores(c):
        off = pl.multiple_of(c * kc, kc)
        return _dot(k_ref[0, pl.ds(off, kc), :], qt_sc[...])

    def values_t(c):
        return vt_ref[:, pl.ds(pl.multiple_of(c * kc, kc), kc)]

    s_sc[0] = scores(0)

    def body(i, carry):
        c = 2 * i
        s_sc[1] = scores(c + 1)
        _online_softmax_pv(s_sc[0], values_t(c), m_sc, l_sc, acc_sc)
        s_sc[0] = scores(jnp.minimum(c + 2, n - 1))
        _online_softmax_pv(s_sc[1], values_t(c + 1), m_sc, l_sc, acc_sc)
        return carry

    lax.fori_loop(0, n // 2, body, 0)
    out = (acc_sc[...] / l_sc[...]).T
    for r in range(R):
        o_ref[:, r * Dv:(r + 1) * Dv] = out[r * bq:(r + 1) * bq].astype(o_ref.dtype)


def _flash(q_arr, k_arr, vt_arr, *, n_kv, rep, q_tile0, k_tile0, dv, bq, kc, name):
    _, S, D = q_arr.shape
    kc = min(kc, S // 2)
    bq = min(bq, S)
    assert q_tile0 % rep == 0
    N = rep * bq
    rows = dv
    return pl.pallas_call(
        functools.partial(_flash_kernel, kc=kc),
        grid=(n_kv, S // bq),
        in_specs=[
            pl.BlockSpec((rep, bq, D), lambda g, i: (q_tile0 // rep + g, i, 0)),
            pl.BlockSpec((1, S, D), lambda g, i: (k_tile0 + g, 0, 0)),
            pl.BlockSpec((rows, vt_arr.shape[1]), lambda g, i: (g, 0)),
        ],
        out_specs=pl.BlockSpec((bq, rep * dv), lambda g, i: (i, g)),
        out_shape=jax.ShapeDtypeStruct((S, n_kv * rep * dv), BF16),
        scratch_shapes=[pltpu.VMEM((D, N), BF16),
                        pltpu.VMEM((2, kc, N), F32),
                        pltpu.VMEM((1, N), F32), pltpu.VMEM((1, N), F32), pltpu.VMEM((rows, N), F32)],
        compiler_params=_cparams(("parallel", "parallel")),
        name=name,
    )(q_arr, k_arr, vt_arr)


def _diff_kernel(q_ref, k_ref, vt_ref, lq1_ref, lk1_ref, lq2_ref, lk2_ref, subln_ref, o_ref,
                 qt_sc, s_sc, m_sc, l_sc, acc_sc, *, kc, lam_init):
    _, bq, D = q_ref.shape
    S = k_ref.shape[1]
    h = pl.program_id(0)
    q0 = pl.program_id(1) * bq
    slope = jnp.float32(2.0 ** (-8.0 * DIFF_HEADS / DIFF_HEADS))
    for hh in range(DIFF_HEADS - 1):
        slope = jnp.where(h == hh, jnp.float32(2.0 ** (-8.0 * (hh + 1) / DIFF_HEADS)), slope)
    slope2 = slope * LOG2E
    rel = (lax.broadcasted_iota(I32, (kc, bq), 0) - lax.broadcasted_iota(I32, (kc, bq), 1)).astype(F32)
    rel_s = rel * slope2
    for j in range(2):
        qt_sc[j] = q_ref[j].astype(F32).T.astype(BF16)
    m_sc[...] = jnp.full(m_sc.shape, -jnp.inf, F32)
    l_sc[...] = jnp.zeros(l_sc.shape, F32)
    acc_sc[...] = jnp.zeros(acc_sc.shape, F32)

    n = S // kc

    def scores(j, c):
        off = pl.multiple_of(c * kc, kc)
        return _dot(k_ref[j, pl.ds(off, kc), :], qt_sc[j])

    s_sc[0] = scores(0, 0)

    def chunk_step(c, penalty):
        off = pl.multiple_of(c * kc, kc)
        delta = (off - q0).astype(F32)
        vt = vt_ref[:, pl.ds(off, kc)]
        s_sc[1] = scores(1, c)
        s0, shift = penalty(s_sc[0], delta)
        _online_softmax_pv(s0, vt, m_sc.at[0], l_sc.at[0], acc_sc.at[0], shift)
        s_sc[0] = scores(0, jnp.minimum(c + 1, n - 1))
        s1, shift = penalty(s_sc[1], delta)
        _online_softmax_pv(s1, vt, m_sc.at[1], l_sc.at[1], acc_sc.at[1], shift)

    def keys_before(c, carry):
        chunk_step(c, lambda s, delta: (s + rel_s, slope2 * delta))
        return carry

    def keys_overlap(c, carry):
        chunk_step(c, lambda s, delta: (s - jnp.abs(rel + delta) * slope2, None))
        return carry

    def keys_after(c, carry):
        chunk_step(c, lambda s, delta: (s - rel_s, -(slope2 * delta)))
        return carry

    c_lo = q0 // kc
    c_hi = (q0 + bq + kc - 1) // kc
    lax.fori_loop(0, c_lo, keys_before, 0)
    lax.fori_loop(c_lo, c_hi, keys_overlap, 0)
    lax.fori_loop(c_hi, n, keys_after, 0)
    lam = (jnp.exp(jnp.sum(lq1_ref[...] * lk1_ref[...], axis=1, keepdims=True))
           - jnp.exp(jnp.sum(lq2_ref[...] * lk2_ref[...], axis=1, keepdims=True)) + lam_init)
    out = (acc_sc[0] / l_sc[0] - lam * (acc_sc[1] / l_sc[1])).T
    o_ref[...] = (_rms(out, subln_ref[...]) * (1.0 - lam_init)).astype(o_ref.dtype)


def _diff_attention(hq, vt_arr, lq1, lk1, lq2, lk2, subln, *, q_tile0, k_tile0, lam_init,
                    bq=512, kc=1024):
    _, S, D = hq.shape
    kc = min(kc, S)
    bq = min(bq, S)
    rows = 2 * D
    vec = pl.BlockSpec((1, D), lambda h, i: (0, 0))
    return pl.pallas_call(
        functools.partial(_diff_kernel, kc=kc, lam_init=lam_init),
        grid=(DIFF_HEADS, S // bq),
        in_specs=[
            pl.BlockSpec((2, bq, D), lambda h, i: (q_tile0 // 2 + h, i, 0)),
            pl.BlockSpec((2, S, D), lambda h, i: (k_tile0 // 2 + h, 0, 0)),
            pl.BlockSpec((rows, vt_arr.shape[1]), lambda h, i: (h, 0)),
            vec, vec, vec, vec,
            pl.BlockSpec((1, 2 * D), lambda h, i: (0, 0)),
        ],
        out_specs=pl.BlockSpec((bq, 2 * D), lambda h, i: (i, h)),
        out_shape=jax.ShapeDtypeStruct((S, DIFF_HEADS * 2 * D), BF16),
        scratch_shapes=[pltpu.VMEM((2, D, bq), BF16), pltpu.VMEM((2, kc, bq), F32),
                        pltpu.VMEM((2, 1, bq), F32), pltpu.VMEM((2, 1, bq), F32),
                        pltpu.VMEM((2, rows, bq), F32)],
        compiler_params=_cparams(("parallel", "parallel")),
        name="diff_attention",
    )(hq, hq, vt_arr, lq1, lk1, lq2, lk2, subln)


def _wout_kernel(*refs, n_in):
    a_refs = refs[:n_in]
    w_refs = refs[n_in:2 * n_in]
    x_ref, g_ref, b_ref, wr_ref, xo_ref, xb_ref, lg_ref = refs[2 * n_in:]
    mix = _dot(a_refs[0][...], w_refs[0][...])
    for a, w in zip(a_refs[1:], w_refs[1:]):
        mix = mix + _dot(a[...], w[...])
    y = _layer_norm(DEEPNORM_ALPHA * x_ref[...] + mix, g_ref[...], b_ref[...])
    xo_ref[...] = y
    y_hi = y.astype(BF16)
    xb_ref[...] = y_hi
    y_lo = (y - y_hi.astype(F32)).astype(BF16)
    wr = wr_ref[...]
    w_hi = wr.astype(BF16)
    w_lo = (wr - w_hi.astype(F32)).astype(BF16)
    E = wr.shape[0]
    both = _dot_nt(jnp.concatenate([w_hi, w_lo], axis=0), y_hi)
    lg_ref[...] = both[:E] + both[E:] + _dot_nt(w_hi, y_lo)


def _wout_ln_router(a_list, w_list, x, g, b, wr_t, *, tm=512):
    S, D = x.shape
    n_in = len(a_list)
    E = wr_t.shape[0]
    in_specs = ([pl.BlockSpec((tm, a.shape[1]), lambda i: (i, 0)) for a in a_list]
                + [pl.BlockSpec(w.shape, lambda i: (0, 0)) for w in w_list]
                + [pl.BlockSpec((tm, D), lambda i: (i, 0)),
                   pl.BlockSpec((1, D), lambda i: (0, 0)),
                   pl.BlockSpec((1, D), lambda i: (0, 0)),
                   pl.BlockSpec((E, D), lambda i: (0, 0))])
    return pl.pallas_call(
        functools.partial(_wout_kernel, n_in=n_in),
        grid=(S // tm,),
        in_specs=in_specs,
        out_specs=[pl.BlockSpec((tm, D), lambda i: (i, 0)),
                   pl.BlockSpec((tm, D), lambda i: (i, 0)),
                   pl.BlockSpec((E, tm), lambda i: (0, i))],
        out_shape=[jax.ShapeDtypeStruct((S, D), F32),
                   jax.ShapeDtypeStruct((S, D), BF16),
                   jax.ShapeDtypeStruct((E, S), F32)],
        compiler_params=_cparams(("parallel",)),
        name="wout_ln_router",
    )(*a_list, *w_list, x, g, b, wr_t)


def _select_kernel(lg_ref, pos_ref, gate_ref, off_ref, cnt_ref, *, E, nb, cap):
    lg = lg_ref[...].reshape(E, nb, LANES)
    mx = jnp.max(lg, axis=0, keepdims=True)
    ex = jnp.exp(lg - mx)
    aff = ex / jnp.sum(ex, axis=0, keepdims=True)
    bits = lax.bitcast_convert_type(aff, I32)

    def count(msk):
        c = jnp.sum(msk.astype(F32), axis=1, keepdims=True)
        return jnp.sum(c, axis=2, keepdims=True)

    def search(i, thr):
        cand = thr | lax.shift_left(jnp.int32(1), 30 - i)
        return jnp.where(count(bits >= cand) >= cap, cand, thr)

    thr = lax.fori_loop(0, 31, search, jnp.zeros((E, 1, 1), I32))
    gt = bits > thr
    eq = bits == thr
    need = cap - count(gt)

    col = lax.broadcasted_iota(I32, (LANES, LANES), 1)
    rw = lax.broadcasted_iota(I32, (LANES, LANES), 0)
    upper = (rw < col).astype(BF16)
    ones = jnp.ones((LANES, LANES), BF16)
    lower = (lax.broadcasted_iota(I32, (nb, nb), 1) < lax.broadcasted_iota(I32, (nb, nb), 0)).astype(BF16)

    def prefix(msk):
        m2 = jnp.where(msk, 1.0, 0.0).reshape(E * nb, LANES).astype(BF16)
        within = _dot(m2, upper).reshape(E, nb, LANES)
        tot = _dot(m2, ones).reshape(E, nb, LANES)
        offs = jnp.stack([_dot(lower, tot[e].astype(BF16)) for e in range(E)], axis=0)
        return within + offs, offs, tot

    eq_rank, _, _ = prefix(eq)
    sel = gt | (eq & (eq_rank < need))
    pos, offs, tot = prefix(sel)
    pos_ref[...] = jnp.where(sel, pos, -1.0).astype(I32).reshape(E * nb, LANES)
    gate_ref[...] = jnp.where(sel, aff, 0.0).reshape(E * nb, LANES)
    off_ref[...] = offs.astype(I32).reshape(E * nb, LANES)
    cnt_ref[...] = tot.astype(I32).reshape(E * nb, LANES)


def _select(logits_t, cap):
    E, S = logits_t.shape
    nb = S // LANES
    shp = (E * nb, LANES)
    full = pl.BlockSpec(shp, lambda: (0, 0))
    return pl.pallas_call(
        functools.partial(_select_kernel, E=E, nb=nb, cap=cap),
        in_specs=[full],
        out_specs=[full, full, full, full],
        out_shape=[jax.ShapeDtypeStruct(shp, I32), jax.ShapeDtypeStruct(shp, F32),
                   jax.ShapeDtypeStruct(shp, I32), jax.ShapeDtypeStruct(shp, I32)],
        compiler_params=pltpu.CompilerParams(vmem_limit_bytes=VMEM_LIMIT),
        name="select",
    )(logits_t.reshape(shp))


def _gather_kernel(off_s, cnt_s, x_ref, pos_ref, gate_ref, xe_ref, gs_ref, *, cap, tbs, ep):
    grp = pl.program_id(0)
    t = pl.program_id(1)

    @pl.when(t == 0)
    def _():
        xe_ref[...] = jnp.zeros(xe_ref.shape, xe_ref.dtype)
        gs_ref[...] = jnp.zeros(gs_ref.shape, gs_ref.dtype)

    def place(k, u, e, tb, width):
        base = jnp.minimum((off_s[e, tb] // BF16_ROWS) * BF16_ROWS, cap - width)
        base = pl.multiple_of(base, BF16_ROWS)
        slot = base + lax.broadcasted_iota(I32, (width, LANES), 0)
        hit = pos_ref[k, u] == slot
        rows = _dot(hit.astype(BF16), x_ref[pl.ds(u * LANES, LANES), :])
        win = pl.ds(base, width)
        xe_ref[k, win, :] = xe_ref[k, win, :] + rows.astype(xe_ref.dtype)
        g = jnp.sum(jnp.where(hit, gate_ref[k, u], 0.0), axis=1, keepdims=True)
        gs_ref[k, win, :] = gs_ref[k, win, :] + jnp.broadcast_to(g, (width, LANES))

    for u in range(tbs):
        for k in range(ep):
            e = grp * ep + k
            tb = t * tbs + u
            cnt = cnt_s[e, tb]

            @pl.when((cnt > 0) & (cnt <= GATHER_SMALL))
            def _():
                place(k, u, e, tb, GATHER_SMALL + BF16_ROWS)

            @pl.when(cnt > GATHER_SMALL)
            def _():
                place(k, u, e, tb, GATHER_WIN)


def _gather(off_s, cnt_s, x_bf, pos, gate, cap, *, tbs=8, ep=2):
    S, D = x_bf.shape
    E, nb = off_s.shape
    assert cap >= GATHER_WIN and cap % BF16_ROWS == 0 and nb % tbs == 0 and E % ep == 0
    pos4 = pos.reshape(E, nb, 1, LANES)
    gate4 = gate.reshape(E, nb, 1, LANES)
    return pl.pallas_call(
        functools.partial(_gather_kernel, cap=cap, tbs=tbs, ep=ep),
        grid_spec=pltpu.PrefetchScalarGridSpec(
            num_scalar_prefetch=2,
            grid=(E // ep, nb // tbs),
            in_specs=[
                pl.BlockSpec((tbs * LANES, D), lambda g, t, o, c: (t, 0)),
                pl.BlockSpec((ep, tbs, 1, LANES), lambda g, t, o, c: (g, t, 0, 0)),
                pl.BlockSpec((ep, tbs, 1, LANES), lambda g, t, o, c: (g, t, 0, 0)),
            ],
            out_specs=[pl.BlockSpec((ep, cap, D), lambda g, t, o, c: (g, 0, 0)),
                       pl.BlockSpec((ep, cap, LANES), lambda g, t, o, c: (g, 0, 0))],
        ),
        out_shape=[jax.ShapeDtypeStruct((E, cap, D), BF16),
                   jax.ShapeDtypeStruct((E, cap, LANES), F32)],
        compiler_params=_cparams(("parallel", "arbitrary")),
        name="gather",
    )(off_s, cnt_s, x_bf, pos4, gate4)


def _ffn_kernel(xe_ref, wg_ref, wu_ref, wd_ref, gs_ref, ye_ref, acc_sc, *, sub):
    fc = pl.program_id(2)
    rows = xe_ref.shape[1]
    @pl.when(fc == 0)
    def _():
        acc_sc[...] = jnp.zeros(acc_sc.shape, F32)

    wg = wg_ref[...].astype(BF16)
    wu = wu_ref[...].astype(BF16)
    wd = wd_ref[...].astype(BF16)
    for r in range(rows // sub):
        sl = pl.ds(r * sub, sub)
        xt = xe_ref[0, sl, :]
        hg = _dot(xt, wg)
        hu = _dot(xt, wu)
        hid = (hg * jax.nn.sigmoid(hg) * hu).astype(BF16)
        acc_sc[sl, :] = acc_sc[sl, :] + _dot(hid, wd)

    @pl.when(fc == pl.num_programs(2) - 1)
    def _():
        ye_ref[0] = (acc_sc[...] * gs_ref[0][:, :1]).astype(ye_ref.dtype)


def _ffn(xe, gs, w_gate, w_up, w_down, layer, *, halves=2, fcw=256):
    E, cap, D = xe.shape
    F = w_gate.shape[-1]
    rows = cap // halves
    sub = min(256, rows)
    return pl.pallas_call(
        functools.partial(_ffn_kernel, sub=sub),
        grid=(E, halves, F // fcw),
        in_specs=[
            pl.BlockSpec((1, rows, D), lambda e, t, f: (e, t, 0)),
            pl.BlockSpec((None, None, D, fcw), lambda e, t, f: (layer, e, 0, f)),
            pl.BlockSpec((None, None, D, fcw), lambda e, t, f: (layer, e, 0, f)),
            pl.BlockSpec((None, None, fcw, D), lambda e, t, f: (layer, e, f, 0)),
            pl.BlockSpec((1, rows, LANES), lambda e, t, f: (e, t, 0)),
        ],
        out_specs=pl.BlockSpec((1, rows, D), lambda e, t, f: (e, t, 0)),
        out_shape=jax.ShapeDtypeStruct((E, cap, D), BF16),
        scratch_shapes=[pltpu.VMEM((rows, D), F32)],
        compiler_params=_cparams(("parallel", "parallel", "arbitrary")),
        name="expert_ffn",
    )(xe, w_gate, w_up, w_down, gs)


def _combine_kernel(win_s, *refs, eg):
    y_refs = refs[:eg]
    post_ref, x_ref, g_ref, b_ref, xo_ref, xb_ref, acc_sc = refs[eg:]
    tb = pl.program_id(0)
    grp = pl.program_id(1)

    @pl.when(grp == 0)
    def _():
        acc_sc[...] = jnp.zeros(acc_sc.shape, F32)

    pt = post_ref[...].astype(F32)
    lane = lax.broadcasted_iota(I32, pt.shape, 1)
    col = lax.broadcasted_iota(I32, (LANES, GATHER_WIN), 1)
    total = None
    for k in range(eg):
        e = grp * eg + k
        pcol = jnp.sum(jnp.where(lane == e, pt, 0.0), axis=1, keepdims=True).astype(I32)
        hit = ((pcol - win_s[e, tb]) == col).astype(BF16)
        part = _dot(hit, y_refs[k][0])
        total = part if total is None else total + part
    acc_sc[...] = acc_sc[...] + total

    @pl.when(grp == pl.num_programs(1) - 1)
    def _():
        y = _layer_norm(DEEPNORM_ALPHA * x_ref[...] + acc_sc[...], g_ref[...], b_ref[...])
        xo_ref[...] = y
        xb_ref[...] = y.astype(BF16)


def _combine_ln(win_s, ye, pos_t, x, g, b, *, eg=16):
    S, D = x.shape
    E, cap, _ = ye.shape
    nb = S // LANES
    assert E % eg == 0

    def y_spec(k):
        return pl.BlockSpec((pl.Element(1), pl.Element(GATHER_WIN), pl.Element(D)),
                            lambda t, gr, ws: (gr * eg + k, (ws[gr * eg + k, t] // BF16_ROWS) * BF16_ROWS, 0))

    row = lambda t, gr, ws: (t, 0)
    fixed = lambda t, gr, ws: (0, 0)
    return pl.pallas_call(
        functools.partial(_combine_kernel, eg=eg),
        grid_spec=pltpu.PrefetchScalarGridSpec(
            num_scalar_prefetch=1,
            grid=(nb, E // eg),
            in_specs=([y_spec(k) for k in range(eg)]
                      + [pl.BlockSpec((LANES, E), row), pl.BlockSpec((LANES, D), row),
                         pl.BlockSpec((1, D), fixed), pl.BlockSpec((1, D), fixed)]),
            out_specs=[pl.BlockSpec((LANES, D), row), pl.BlockSpec((LANES, D), row)],
            scratch_shapes=[pltpu.VMEM((LANES, D), F32)],
        ),
        out_shape=[jax.ShapeDtypeStruct((S, D), F32), jax.ShapeDtypeStruct((S, D), BF16)],
        compiler_params=_cparams(("parallel", "arbitrary")),
        name="combine_ln",
    )(win_s, *([ye] * eg), pos_t, x, g, b)


def _moe(x, x_bf, logits_t, w_gate, w_up, w_down, layer, g, b):
    S, D = x.shape
    E = logits_t.shape[0]
    nb = S // LANES
    cap = EC_FACTOR * S // E
    pos, gate, off, cnt = _select(logits_t, cap)
    off_s = off[:, 0].reshape(E, nb)
    cnt_s = cnt[:, 0].reshape(E, nb)
    xe, gs = _gather(off_s, cnt_s, x_bf, pos, gate, cap)
    ye = _ffn(xe, gs, w_gate, w_up, w_down, layer)
    pos_t = pos.reshape(E, S).T
    win_s = jnp.minimum((off_s // BF16_ROWS) * BF16_ROWS, cap - GATHER_WIN)
    return _combine_ln(win_s, ye, pos_t, x, g, b)


def _mla_in_kernel(x_ref, w_ref, cos_ref, sin_ref, qn_ref, kvn_ref, cq_ref, ckv_ref, kr_ref):
    h = _dot(x_ref[...], w_ref[...])
    cq_ref[...] = _rms(h[:, :MLA_Q_RANK], qn_ref[...]).astype(BF16)
    ckv_ref[...] = _rms(h[:, MLA_Q_RANK:MLA_Q_RANK + MLA_KV_RANK], kvn_ref[...]).astype(BF16)
    kr = h[:, MLA_Q_RANK + MLA_KV_RANK:]
    kr_ref[...] = _rope(kr, cos_ref[...], sin_ref[...], MLA_ROPE // 4).astype(BF16)


def _mla_in(x_bf, w_bf, cos, sin, qn, kvn, *, tm=1024):
    S, K = x_bf.shape
    N = w_bf.shape[1]
    row = lambda i: (i, 0)
    fixed = lambda i: (0, 0)
    return pl.pallas_call(
        _mla_in_kernel,
        grid=(S // tm,),
        in_specs=[pl.BlockSpec((tm, K), row), pl.BlockSpec((K, N), fixed),
                  pl.BlockSpec((tm, LANES), row), pl.BlockSpec((tm, LANES), row),
                  pl.BlockSpec((1, MLA_Q_RANK), fixed), pl.BlockSpec((1, MLA_KV_RANK), fixed)],
        out_specs=[pl.BlockSpec((tm, MLA_Q_RANK), row), pl.BlockSpec((tm, MLA_KV_RANK), row),
                   pl.BlockSpec((tm, LANES), row)],
        out_shape=[jax.ShapeDtypeStruct((S, MLA_Q_RANK), BF16),
                   jax.ShapeDtypeStruct((S, MLA_KV_RANK), BF16),
                   jax.ShapeDtypeStruct((S, LANES), BF16)],
        compiler_params=_cparams(("parallel",)),
        name="mla_in",
    )(x_bf, w_bf, cos, sin, qn, kvn)


def _mla_q_kernel(c_ref, w_ref, cos_ref, sin_ref, o_ref, *, heads):
    h = _dot(c_ref[...], w_ref[...])
    scale = (MLA_NOPE + MLA_ROPE) ** -0.5 * LOG2E
    for c in range(heads):
        base = c * MLA_QK_PAD
        o_ref[c, :, :MLA_NOPE] = (h[:, base:base + MLA_NOPE] * scale).astype(BF16)
        rp = _rope(h[:, base + MLA_NOPE:base + MLA_QK_PAD], cos_ref[...], sin_ref[...], MLA_ROPE // 4)
        o_ref[c, :, MLA_NOPE:] = (rp * scale).astype(BF16)


def _mla_q(cq, w_bf, cos, sin, *, tm=1024, heads=2):
    S, K = cq.shape
    tn = heads * MLA_QK_PAD
    return pl.pallas_call(
        functools.partial(_mla_q_kernel, heads=heads),
        grid=(S // tm, MLA_HEADS // heads),
        in_specs=[pl.BlockSpec((tm, K), lambda i, j: (i, 0)),
                  pl.BlockSpec((K, tn), lambda i, j: (0, j)),
                  pl.BlockSpec((tm, LANES), lambda i, j: (i, 0)),
                  pl.BlockSpec((tm, LANES), lambda i, j: (i, 0))],
        out_specs=pl.BlockSpec((heads, tm, MLA_QK_PAD), lambda i, j: (j, i, 0)),
        out_shape=jax.ShapeDtypeStruct((MLA_HEADS, S, MLA_QK_PAD), BF16),
        compiler_params=_cparams(("parallel", "parallel")),
        name="mla_q",
    )(cq, w_bf, cos, sin)


def _mla_k_kernel(c_ref, wk_ref, kr_ref, k_ref, *, heads):
    kn = _dot(c_ref[...], wk_ref[...])
    for hh in range(heads):
        k_ref[hh, :, :MLA_NOPE] = kn[:, hh * MLA_NOPE:(hh + 1) * MLA_NOPE].astype(BF16)
        k_ref[hh, :, MLA_NOPE:] = kr_ref[...]


def _mla_k(ckv, wk_bf, kr, *, tm=1024, heads=4):
    S, K = ckv.shape
    return pl.pallas_call(
        functools.partial(_mla_k_kernel, heads=heads),
        grid=(S // tm, MLA_HEADS // heads),
        in_specs=[pl.BlockSpec((tm, K), lambda i, j: (i, 0)),
                  pl.BlockSpec((K, heads * MLA_NOPE), lambda i, j: (0, j)),
                  pl.BlockSpec((tm, LANES), lambda i, j: (i, 0))],
        out_specs=pl.BlockSpec((heads, tm, MLA_QK_PAD), lambda i, j: (j, i, 0)),
        out_shape=jax.ShapeDtypeStruct((MLA_HEADS, S, MLA_QK_PAD), BF16),
        compiler_params=_cparams(("parallel", "parallel")),
        name="mla_k",
    )(ckv, wk_bf, kr)


def kernel(x, ab_w_in, ab_q_norm, ab_k_norm, ab_lambda_q1, ab_lambda_k1, ab_lambda_q2, ab_lambda_k2, ab_subln, ab_w_out, mla_w_in, mla_q_norm, mla_kv_norm, mla_w_uq, mla_w_ukv, mla_w_out, ln_mix_g, ln_mix_b, moe_w_router, moe_w_gate, moe_w_up, moe_w_down, ln_ffn_g, ln_ffn_b):
    B, S, D = x.shape
    assert B == 1 and D == D_MODEL and S % 512 == 0
    xf = x.reshape(S, D)
    x_bf = xf.astype(BF16)
    cos_a, sin_a = _rope_tables(S, HEAD_DIM // 4, HEAD_DIM)
    cos_c, sin_c = _rope_tables(S, MLA_ROPE // 4, LANES)
    row2 = lambda v: v.reshape(1, -1)

    lam_init = 0.8 - 0.6 * math.exp(-0.3 * 0)
    w_in = ab_w_in[0].astype(BF16)
    c_av = A_Q + A_KV
    c_bq = c_av + A_KV
    c_bv = c_bq + 2 * B_QK
    w_qk = jnp.concatenate([w_in[:, :c_av], w_in[:, c_bq:c_bv]], axis=1)
    hq = _proj0(x_bf, w_qk, cos_a, sin_a, row2(ab_q_norm[0]), row2(ab_k_norm[0]))
    vt_a = _values_t(w_in[:, c_av:c_bq], x_bf, name="values_t_gqa")
    vt_b = _values_t(w_in[:, c_bv:], x_bf, name="values_t_diff")
    t_ak = A_Q // HEAD_DIM
    t_bq = t_ak + GQA_KV_HEADS
    t_bk = t_bq + 2 * DIFF_HEADS
    a_out = _flash(hq, hq, vt_a, n_kv=GQA_KV_HEADS, rep=GQA_Q_HEADS // GQA_KV_HEADS,
                   q_tile0=0, k_tile0=t_ak, dv=HEAD_DIM, bq=256, kc=2048, name="gqa_attention")
    b_out = _diff_attention(hq, vt_b, row2(ab_lambda_q1[0]), row2(ab_lambda_k1[0]),
                            row2(ab_lambda_q2[0]), row2(ab_lambda_k2[0]), row2(ab_subln[0]),
                            q_tile0=t_bq, k_tile0=t_bk, lam_init=lam_init)
    w_out = ab_w_out[0].astype(BF16)
    xf, x_bf, logits_t = _wout_ln_router(
        [a_out, b_out], [w_out[:A_Q], w_out[A_Q:]], xf, row2(ln_mix_g[0]), row2(ln_mix_b[0]),
        moe_w_router[0].T)
    xf, x_bf = _moe(xf, x_bf, logits_t, moe_w_gate, moe_w_up, moe_w_down, 0,
                    row2(ln_ffn_g[0]), row2(ln_ffn_b[0]))

    w1 = jnp.pad(mla_w_in[0], ((0, 0), (0, LANES - MLA_ROPE))).astype(BF16)
    cq, ckv, kr = _mla_in(x_bf, w1, cos_c, sin_c, row2(mla_q_norm[0]), row2(mla_kv_norm[0]))
    w_uq = mla_w_uq[0].reshape(MLA_Q_RANK, MLA_HEADS, MLA_NOPE + MLA_ROPE)
    w_uq = jnp.pad(w_uq, ((0, 0), (0, 0), (0, MLA_QK_PAD - MLA_NOPE - MLA_ROPE)))
    w_uq = w_uq.reshape(MLA_Q_RANK, MLA_HEADS * MLA_QK_PAD).astype(BF16)
    w_ukv = mla_w_ukv[0].reshape(MLA_KV_RANK, MLA_HEADS, MLA_NOPE + MLA_V)
    w_uk = w_ukv[:, :, :MLA_NOPE].reshape(MLA_KV_RANK, MLA_HEADS * MLA_NOPE).astype(BF16)
    w_uv = w_ukv[:, :, MLA_NOPE:].reshape(MLA_KV_RANK, MLA_HEADS * MLA_V)
    q_pad = _mla_q(cq, w_uq, cos_c, sin_c)
    k_pad = _mla_k(ckv, w_uk, kr)
    vt1 = _values_t(w_uv, ckv, name="values_t_mla")
    c_out = _flash(q_pad, k_pad, vt1, n_kv=MLA_HEADS, rep=1, q_tile0=0, k_tile0=0, dv=MLA_V,
                   bq=512, kc=2048, name="mla_attention")
    xf, x_bf, logits_t = _wout_ln_router(
        [c_out], [mla_w_out[0].astype(BF16)], xf, row2(ln_mix_g[1]), row2(ln_mix_b[1]),
        moe_w_router[1].T)
    xf, x_bf = _moe(xf, x_bf, logits_t, moe_w_gate, moe_w_up, moe_w_down, 1,
                    row2(ln_ffn_g[1]), row2(ln_ffn_b[1]))
    return xf.reshape(B, S, D)
```

```python
import functools
import math

import jax
import jax.numpy as jnp
from jax import lax
from jax.experimental import pallas as pl
from jax.experimental.pallas import tpu as pltpu

F32 = jnp.float32
BF16 = jnp.bfloat16
I32 = jnp.int32

D_MODEL = 2048
DEPTH = 2
GRID_W = 64
ROPE_THETA = 10000.0
NORM_EPS = 1e-6
LN_EPS = 1e-5

HEAD_DIM = 128
GQA_Q_HEADS = 8
GQA_KV_HEADS = 2
DIFF_HEADS = 4
A_Q = GQA_Q_HEADS * HEAD_DIM
A_KV = GQA_KV_HEADS * HEAD_DIM
B_QK = DIFF_HEADS * 2 * HEAD_DIM
B_V = DIFF_HEADS * 2 * HEAD_DIM

MLA_HEADS = 16
MLA_Q_RANK = 512
MLA_KV_RANK = 512
MLA_NOPE = 128
MLA_ROPE = 64
MLA_V = 128
MLA_QK_PAD = 256

N_EXPERTS = 16
EXPERT_FF = 2048
EC_FACTOR = 2

DEEPNORM_ALPHA = (2 * DEPTH) ** 0.25
LOG2E = math.log2(math.e)

LANES = 128
BF16_ROWS = 16
GATHER_WIN = LANES + BF16_ROWS
GATHER_SMALL = 32
VMEM_LIMIT = 56 * 1024 * 1024


def _cparams(sem, vmem=VMEM_LIMIT, flags=None):
    return pltpu.CompilerParams(dimension_semantics=sem, vmem_limit_bytes=vmem, flags=flags)


def _dot(a, b):
    return jnp.dot(a, b, preferred_element_type=F32)


def _dot_nt(a, b):
    return lax.dot_general(a, b, (((1,), (1,)), ((), ())), preferred_element_type=F32)


def _rope(y, cos, sin, half):
    n = y.shape[-1]
    lane = lax.broadcasted_iota(I32, y.shape, 1)
    up = pltpu.roll(y, n - half, 1)
    dn = pltpu.roll(y, half, 1)
    partner = jnp.where((lane % (2 * half)) < half, up, dn)
    return y * cos + partner * sin


def _rms(y, w):
    return y * lax.rsqrt(jnp.mean(y * y, axis=-1, keepdims=True) + NORM_EPS) * w


def _layer_norm(y, g, b):
    mu = jnp.mean(y, axis=-1, keepdims=True)
    d = y - mu
    var = jnp.mean(d * d, axis=-1, keepdims=True)
    return d * lax.rsqrt(var + LN_EPS) * g + b


def _rope_tables(S, half, width):
    n_rows = S // GRID_W
    n = 2 * half
    freqs = ROPE_THETA ** (-jnp.arange(0, n, 2, dtype=F32) / n)[None, :]
    parts_c, parts_s = [], []
    for count, along_rows in ((n_rows, True), (GRID_W, False)):
        ang = jnp.arange(count, dtype=F32)[:, None] * freqs
        c = jnp.concatenate([jnp.cos(ang)] * 2, axis=1)
        s = jnp.concatenate([-jnp.sin(ang), jnp.sin(ang)], axis=1)
        shape = (n_rows, GRID_W, n)
        expand = (lambda a: a[:, None, :]) if along_rows else (lambda a: a[None, :, :])
        parts_c.append(jnp.broadcast_to(expand(c), shape).reshape(S, n))
        parts_s.append(jnp.broadcast_to(expand(s), shape).reshape(S, n))
    pad = width - 4 * half
    if pad:
        parts_c.append(jnp.ones((S, pad), F32))
        parts_s.append(jnp.zeros((S, pad), F32))
    return jnp.concatenate(parts_c, axis=1), jnp.concatenate(parts_s, axis=1)


def _proj0_kernel(x_ref, w_ref, cos_ref, sin_ref, qn_ref, kn_ref, o_ref, *, tn):
    j = pl.program_id(1)
    h = _dot(x_ref[...], w_ref[...])
    nsub = tn // HEAD_DIM
    scale = HEAD_DIM ** -0.5 * LOG2E
    k_tile = A_Q // tn
    bq_lo = (A_Q + A_KV) // tn
    bq_hi = bq_lo + B_QK // tn

    def normed(c, w):
        y = _rms(h[:, c * HEAD_DIM:(c + 1) * HEAD_DIM], w)
        return _rope(y, cos_ref[...], sin_ref[...], HEAD_DIM // 4)

    @pl.when(j < k_tile)
    def _():
        for c in range(nsub):
            o_ref[c] = (normed(c, qn_ref[...]) * scale).astype(o_ref.dtype)

    @pl.when(j == k_tile)
    def _():
        for c in range(nsub):
            o_ref[c] = normed(c, kn_ref[...]).astype(o_ref.dtype)

    @pl.when(j > k_tile)
    def _():
        mul = jnp.where((j >= bq_lo) & (j < bq_hi), scale, 1.0).astype(F32)
        for c in range(nsub):
            o_ref[c] = (h[:, c * HEAD_DIM:(c + 1) * HEAD_DIM] * mul).astype(o_ref.dtype)


def _proj0(x_bf, w_bf, cos, sin, qn, kn, *, tm=1024, tn=256):
    S, K = x_bf.shape
    N = w_bf.shape[1]
    assert A_Q % tn == 0 and tn == A_KV and N % tn == 0
    return pl.pallas_call(
        functools.partial(_proj0_kernel, tn=tn),
        grid=(S // tm, N // tn),
        in_specs=[
            pl.BlockSpec((tm, K), lambda i, j: (i, 0)),
            pl.BlockSpec((K, tn), lambda i, j: (0, j)),
            pl.BlockSpec((tm, HEAD_DIM), lambda i, j: (i, 0)),
            pl.BlockSpec((tm, HEAD_DIM), lambda i, j: (i, 0)),
            pl.BlockSpec((1, HEAD_DIM), lambda i, j: (0, 0)),
            pl.BlockSpec((1, HEAD_DIM), lambda i, j: (0, 0)),
        ],
        out_specs=pl.BlockSpec((tn // HEAD_DIM, tm, HEAD_DIM), lambda i, j: (j, i, 0)),
        out_shape=jax.ShapeDtypeStruct((N // HEAD_DIM, S, HEAD_DIM), BF16),
        compiler_params=_cparams(("parallel", "parallel")),
        name="proj0",
    )(x_bf, w_bf, cos, sin, qn, kn)


def _values_t_kernel(wt_ref, x_ref, o_ref):
    o_ref[...] = _dot_nt(wt_ref[...], x_ref[...]).astype(o_ref.dtype)


def _values_t(w, x_bf, *, tm=1024, tn=256, name="values_t"):
    S, K = x_bf.shape
    N = w.shape[1]
    return pl.pallas_call(
        _values_t_kernel,
        grid=(S // tm, N // tn),
        in_specs=[pl.BlockSpec((tn, K), lambda i, j: (j, 0)),
                  pl.BlockSpec((tm, K), lambda i, j: (i, 0))],
        out_specs=pl.BlockSpec((tn, tm), lambda i, j: (j, i)),
        out_shape=jax.ShapeDtypeStruct((N, S), BF16),
        compiler_params=_cparams(("parallel", "parallel")),
        name=name,
    )(w.T.astype(BF16), x_bf)


def _online_softmax_pv(s, vt, m_ref, l_ref, acc_ref, shift=None):
    m_prev = m_ref[...]
    m_chunk = jnp.max(s, axis=0, keepdims=True)
    if shift is None:
        m_new = jnp.maximum(m_prev, m_chunk)
        p = jnp.exp2(s - m_new)
    else:
        m_new = jnp.maximum(m_prev, m_chunk + shift)
        p = jnp.exp2(s - (m_new - shift))
    alpha = jnp.exp2(m_prev - m_new)
    l_ref[...] = alpha * l_ref[...] + jnp.sum(p, axis=0, keepdims=True)
    acc_ref[...] = alpha * acc_ref[...] + _dot(vt, p.astype(BF16))
    m_ref[...] = m_new


def _flash_kernel(q_ref, qnext_ref, k_ref, vt_ref, o_ref, qt_sc, s_sc, m_sc, l_sc, acc_sc, *, kc):
    R, bq, D = q_ref.shape
    S = k_ref.shape[1]
    Dv = vt_ref.shape[0]
    n = S // kc
    assert n % 2 == 0
    cur = pl.program_id(1) % 2
    nxt = 1 - cur

    def scores(c):
        wrap = c >= n
        off = pl.multiple_of(jnp.where(wrap, 0, c) * kc, kc)
        return _dot(k_ref[0, pl.ds(off, kc), :], qt_sc[jnp.where(wrap, nxt, cur)])

    def values_t(c):
        return vt_ref[:, pl.ds(pl.multiple_of(c * kc, kc), kc)]

    @pl.when(pl.program_id(1) == 0)
    def _():
        qt_sc[0] = q_ref[...].reshape(R * bq, D).T
        s_sc[0] = scores(0)

    qt_sc[nxt] = qnext_ref[...].reshape(R * bq, D).T
    m_sc[...] = jnp.full(m_sc.shape, -jnp.inf, F32)
    l_sc[...] = jnp.zeros(l_sc.shape, F32)
    acc_sc[...] = jnp.zeros(acc_sc.shape, F32)

    def body(i, carry):
        c = 2 * i
        s_sc[1] = scores(c + 1)
        _online_softmax_pv(s_sc[0], values_t(c), m_sc, l_sc, acc_sc)
        s_sc[0] = scores(c + 2)
        _online_softmax_pv(s_sc[1], values_t(c + 1), m_sc, l_sc, acc_sc)
        return carry

    lax.fori_loop(0, n // 2, body, 0)
    out = (acc_sc[...] / l_sc[...]).T
    for r in range(R):
        o_ref[:, r * Dv:(r + 1) * Dv] = out[r * bq:(r + 1) * bq].astype(o_ref.dtype)


def _flash(q_arr, k_arr, vt_arr, *, n_kv, rep, q_tile0, k_tile0, dv, bq, kc, name):
    _, S, D = q_arr.shape
    kc = min(kc, S // 2)
    bq = min(bq, S)
    assert q_tile0 % rep == 0
    N = rep * bq
    rows = dv
    nq = S // bq
    return pl.pallas_call(
        functools.partial(_flash_kernel, kc=kc),
        grid=(n_kv, nq),
        in_specs=[
            pl.BlockSpec((rep, bq, D), lambda g, i: (q_tile0 // rep + g, i, 0)),
            pl.BlockSpec((rep, bq, D), lambda g, i: (q_tile0 // rep + g, jnp.minimum(i + 1, nq - 1), 0)),
            pl.BlockSpec((1, S, D), lambda g, i: (k_tile0 + g, 0, 0)),
            pl.BlockSpec((rows, vt_arr.shape[1]), lambda g, i: (g, 0)),
        ],
        out_specs=pl.BlockSpec((bq, rep * dv), lambda g, i: (i, g)),
        out_shape=jax.ShapeDtypeStruct((S, n_kv * rep * dv), BF16),
        scratch_shapes=[pltpu.VMEM((2, D, N), BF16),
                        pltpu.VMEM((2, kc, N), F32),
                        pltpu.VMEM((1, N), F32), pltpu.VMEM((1, N), F32), pltpu.VMEM((rows, N), F32)],
        compiler_params=_cparams(("parallel", "arbitrary")),
        name=name,
    )(q_arr, q_arr, k_arr, vt_arr)


def _diff_kernel(q_ref, qnext_ref, k_ref, vt_ref, lq1_ref, lk1_ref, lq2_ref, lk2_ref, subln_ref, o_ref,
                 qt_sc, s_sc, m_sc, l_sc, acc_sc, *, kc, lam_init):
    _, bq, D = q_ref.shape
    S = k_ref.shape[1]
    h = pl.program_id(0)
    q0 = pl.program_id(1) * bq
    slope = jnp.float32(2.0 ** (-8.0 * DIFF_HEADS / DIFF_HEADS))
    for hh in range(DIFF_HEADS - 1):
        slope = jnp.where(h == hh, jnp.float32(2.0 ** (-8.0 * (hh + 1) / DIFF_HEADS)), slope)
    slope2 = slope * LOG2E
    rel = (lax.broadcasted_iota(I32, (kc, bq), 0) - lax.broadcasted_iota(I32, (kc, bq), 1)).astype(F32)
    rel_s = rel * slope2
    n = S // kc
    cur = pl.program_id(1) % 2
    nxt = 1 - cur

    def scores(j, c):
        wrap = c >= n
        off = pl.multiple_of(jnp.where(wrap, 0, c) * kc, kc)
        return _dot(k_ref[j, pl.ds(off, kc), :], qt_sc[jnp.where(wrap, nxt, cur), j])

    @pl.when(pl.program_id(1) == 0)
    def _():
        for j in range(2):
            qt_sc[0, j] = q_ref[j].T
        s_sc[0] = scores(0, 0)

    for j in range(2):
        qt_sc[nxt, j] = qnext_ref[j].T
    m_sc[...] = jnp.full(m_sc.shape, -jnp.inf, F32)
    l_sc[...] = jnp.zeros(l_sc.shape, F32)
    acc_sc[...] = jnp.zeros(acc_sc.shape, F32)

    def chunk_step(c, penalty):
        off = pl.multiple_of(c * kc, kc)
        delta = (off - q0).astype(F32)
        vt = vt_ref[:, pl.ds(off, kc)]
        s_sc[1] = scores(1, c)
        s0, shift = penalty(s_sc[0], delta)
        _online_softmax_pv(s0, vt, m_sc.at[0], l_sc.at[0], acc_sc.at[0], shift)
        s_sc[0] = scores(0, c + 1)
        s1, shift = penalty(s_sc[1], delta)
        _online_softmax_pv(s1, vt, m_sc.at[1], l_sc.at[1], acc_sc.at[1], shift)

    def keys_before(c, carry):
        chunk_step(c, lambda s, delta: (s + rel_s, slope2 * delta))
        return carry

    def keys_overlap(c, carry):
        chunk_step(c, lambda s, delta: (s - jnp.abs(rel + delta) * slope2, None))
        return carry

    def keys_after(c, carry):
        chunk_step(c, lambda s, delta: (s - rel_s, -(slope2 * delta)))
        return carry

    c_lo = q0 // kc
    c_hi = (q0 + bq + kc - 1) // kc
    lax.fori_loop(0, c_lo, keys_before, 0)
    lax.fori_loop(c_lo, c_hi, keys_overlap, 0)
    lax.fori_loop(c_hi, n, keys_after, 0)
    lam = (jnp.exp(jnp.sum(lq1_ref[...] * lk1_ref[...], axis=1, keepdims=True))
           - jnp.exp(jnp.sum(lq2_ref[...] * lk2_ref[...], axis=1, keepdims=True)) + lam_init)
    out = (acc_sc[0] / l_sc[0] - lam * (acc_sc[1] / l_sc[1])).T
    o_ref[...] = (_rms(out, subln_ref[...]) * (1.0 - lam_init)).astype(o_ref.dtype)


def _diff_attention(hq, vt_arr, lq1, lk1, lq2, lk2, subln, *, q_tile0, k_tile0, lam_init,
                    bq=512, kc=1024):
    _, S, D = hq.shape
    kc = min(kc, S)
    bq = min(bq, S)
    rows = 2 * D
    vec = pl.BlockSpec((1, D), lambda h, i: (0, 0))
    nq = S // bq
    return pl.pallas_call(
        functools.partial(_diff_kernel, kc=kc, lam_init=lam_init),
        grid=(DIFF_HEADS, nq),
        in_specs=[
            pl.BlockSpec((2, bq, D), lambda h, i: (q_tile0 // 2 + h, i, 0)),
            pl.BlockSpec((2, bq, D), lambda h, i: (q_tile0 // 2 + h, jnp.minimum(i + 1, nq - 1), 0)),
            pl.BlockSpec((2, S, D), lambda h, i: (k_tile0 // 2 + h, 0, 0)),
            pl.BlockSpec((rows, vt_arr.shape[1]), lambda h, i: (h, 0)),
            vec, vec, vec, vec,
            pl.BlockSpec((1, 2 * D), lambda h, i: (0, 0)),
        ],
        out_specs=pl.BlockSpec((bq, 2 * D), lambda h, i: (i, h)),
        out_shape=jax.ShapeDtypeStruct((S, DIFF_HEADS * 2 * D), BF16),
        scratch_shapes=[pltpu.VMEM((2, 2, D, bq), BF16), pltpu.VMEM((2, kc, bq), F32),
                        pltpu.VMEM((2, 1, bq), F32), pltpu.VMEM((2, 1, bq), F32),
                        pltpu.VMEM((2, rows, bq), F32)],
        compiler_params=_cparams(("parallel", "arbitrary")),
        name="diff_attention",
    )(hq, hq, hq, vt_arr, lq1, lk1, lq2, lk2, subln)


def _wout_kernel(*refs, n_in):
    a_refs = refs[:n_in]
    w_refs = refs[n_in:2 * n_in]
    x_ref, g_ref, b_ref, wr_ref, xo_ref, xb_ref, lg_ref = refs[2 * n_in:]
    mix = _dot(a_refs[0][...], w_refs[0][...])
    for a, w in zip(a_refs[1:], w_refs[1:]):
        mix = mix + _dot(a[...], w[...])
    y = _layer_norm(DEEPNORM_ALPHA * x_ref[...] + mix, g_ref[...], b_ref[...])
    xo_ref[...] = y
    y_hi = y.astype(BF16)
    xb_ref[...] = y_hi
    y_lo = (y - y_hi.astype(F32)).astype(BF16)
    wr = wr_ref[...]
    w_hi = wr.astype(BF16)
    w_lo = (wr - w_hi.astype(F32)).astype(BF16)
    E = wr.shape[0]
    both = _dot_nt(jnp.concatenate([w_hi, w_lo], axis=0), y_hi)
    lg_ref[...] = both[:E] + both[E:] + _dot_nt(w_hi, y_lo)


def _wout_ln_router(a_list, w_list, x, g, b, wr_t, *, tm=512):
    S, D = x.shape
    n_in = len(a_list)
    E = wr_t.shape[0]
    in_specs = ([pl.BlockSpec((tm, a.shape[1]), lambda i: (i, 0)) for a in a_list]
                + [pl.BlockSpec(w.shape, lambda i: (0, 0)) for w in w_list]
                + [pl.BlockSpec((tm, D), lambda i: (i, 0)),
                   pl.BlockSpec((1, D), lambda i: (0, 0)),
                   pl.BlockSpec((1, D), lambda i: (0, 0)),
                   pl.BlockSpec((E, D), lambda i: (0, 0))])
    return pl.pallas_call(
        functools.partial(_wout_kernel, n_in=n_in),
        grid=(S // tm,),
        in_specs=in_specs,
        out_specs=[pl.BlockSpec((tm, D), lambda i: (i, 0)),
                   pl.BlockSpec((tm, D), lambda i: (i, 0)),
                   pl.BlockSpec((E, tm), lambda i: (0, i))],
        out_shape=[jax.ShapeDtypeStruct((S, D), F32),
                   jax.ShapeDtypeStruct((S, D), BF16),
                   jax.ShapeDtypeStruct((E, S), F32)],
        compiler_params=_cparams(("parallel",)),
        name="wout_ln_router",
    )(*a_list, *w_list, x, g, b, wr_t)


def _select_kernel(lg_ref, pos_ref, gate_ref, off_ref, cnt_ref, *, E, nb, cap):
    lg = lg_ref[...].reshape(E, nb, LANES)
    mx = jnp.max(lg, axis=0, keepdims=True)
    ex = jnp.exp(lg - mx)
    aff = ex / jnp.sum(ex, axis=0, keepdims=True)
    bits = lax.bitcast_convert_type(aff, I32)

    def count(msk):
        c = jnp.sum(msk.astype(F32), axis=1, keepdims=True)
        return jnp.sum(c, axis=2, keepdims=True)

    def search(i, thr):
        cand = thr | lax.shift_left(jnp.int32(1), 30 - i)
        return jnp.where(count(bits >= cand) >= cap, cand, thr)

    thr = lax.fori_loop(0, 31, search, jnp.zeros((E, 1, 1), I32))
    gt = bits > thr
    eq = bits == thr
    need = cap - count(gt)

    col = lax.broadcasted_iota(I32, (LANES, LANES), 1)
    rw = lax.broadcasted_iota(I32, (LANES, LANES), 0)
    upper = (rw < col).astype(BF16)
    ones = jnp.ones((LANES, LANES), BF16)
    lower = (lax.broadcasted_iota(I32, (nb, nb), 1) < lax.broadcasted_iota(I32, (nb, nb), 0)).astype(BF16)

    def prefix(msk):
        m2 = jnp.where(msk, 1.0, 0.0).reshape(E * nb, LANES).astype(BF16)
        within = _dot(m2, upper).reshape(E, nb, LANES)
        tot = _dot(m2, ones).reshape(E, nb, LANES)
        offs = jnp.stack([_dot(lower, tot[e].astype(BF16)) for e in range(E)], axis=0)
        return within + offs, offs, tot

    eq_rank, _, _ = prefix(eq)
    sel = gt | (eq & (eq_rank < need))
    pos, offs, tot = prefix(sel)
    pos_ref[...] = jnp.where(sel, pos, -1.0).astype(I32).reshape(E * nb, LANES)
    gate_ref[...] = jnp.where(sel, aff, 0.0).reshape(E * nb, LANES)
    off_ref[...] = offs.astype(I32).reshape(E * nb, LANES)
    cnt_ref[...] = tot.astype(I32).reshape(E * nb, LANES)


def _select(logits_t, cap):
    E, S = logits_t.shape
    nb = S // LANES
    shp = (E * nb, LANES)
    full = pl.BlockSpec(shp, lambda: (0, 0))
    return pl.pallas_call(
        functools.partial(_select_kernel, E=E, nb=nb, cap=cap),
        in_specs=[full],
        out_specs=[full, full, full, full],
        out_shape=[jax.ShapeDtypeStruct(shp, I32), jax.ShapeDtypeStruct(shp, F32),
                   jax.ShapeDtypeStruct(shp, I32), jax.ShapeDtypeStruct(shp, I32)],
        compiler_params=pltpu.CompilerParams(vmem_limit_bytes=VMEM_LIMIT),
        name="select",
    )(logits_t.reshape(shp))


def _gather_kernel(off_s, cnt_s, x_ref, pos_ref, gate_ref, xe_ref, gs_ref, *, cap, tbs, ep):
    grp = pl.program_id(0)
    t = pl.program_id(1)

    @pl.when(t == 0)
    def _():
        xe_ref[...] = jnp.zeros(xe_ref.shape, xe_ref.dtype)
        gs_ref[...] = jnp.zeros(gs_ref.shape, gs_ref.dtype)

    def place(k, u, e, tb, width):
        base = jnp.minimum((off_s[e, tb] // BF16_ROWS) * BF16_ROWS, cap - width)
        base = pl.multiple_of(base, BF16_ROWS)
        slot = base + lax.broadcasted_iota(I32, (width, LANES), 0)
        hit = pos_ref[k, u] == slot
        rows = _dot(hit.astype(BF16), x_ref[pl.ds(u * LANES, LANES), :])
        win = pl.ds(base, width)
        xe_ref[k, win, :] = xe_ref[k, win, :] + rows.astype(xe_ref.dtype)
        g = jnp.sum(jnp.where(hit, gate_ref[k, u], 0.0), axis=1, keepdims=True)
        gs_ref[k, win, :] = gs_ref[k, win, :] + jnp.broadcast_to(g, (width, LANES))

    for u in range(tbs):
        for k in range(ep):
            e = grp * ep + k
            tb = t * tbs + u
            cnt = cnt_s[e, tb]

            @pl.when((cnt > 0) & (cnt <= GATHER_SMALL))
            def _():
                place(k, u, e, tb, GATHER_SMALL + BF16_ROWS)

            @pl.when(cnt > GATHER_SMALL)
            def _():
                place(k, u, e, tb, GATHER_WIN)


def _gather(off_s, cnt_s, x_bf, pos, gate, cap, *, tbs=8, ep=2):
    S, D = x_bf.shape
    E, nb = off_s.shape
    assert cap >= GATHER_WIN and cap % BF16_ROWS == 0 and nb % tbs == 0 and E % ep == 0
    pos4 = pos.reshape(E, nb, 1, LANES)
    gate4 = gate.reshape(E, nb, 1, LANES)
    return pl.pallas_call(
        functools.partial(_gather_kernel, cap=cap, tbs=tbs, ep=ep),
        grid_spec=pltpu.PrefetchScalarGridSpec(
            num_scalar_prefetch=2,
            grid=(E // ep, nb // tbs),
            in_specs=[
                pl.BlockSpec((tbs * LANES, D), lambda g, t, o, c: (t, 0)),
                pl.BlockSpec((ep, tbs, 1, LANES), lambda g, t, o, c: (g, t, 0, 0)),
                pl.BlockSpec((ep, tbs, 1, LANES), lambda g, t, o, c: (g, t, 0, 0)),
            ],
            out_specs=[pl.BlockSpec((ep, cap, D), lambda g, t, o, c: (g, 0, 0)),
                       pl.BlockSpec((ep, cap, LANES), lambda g, t, o, c: (g, 0, 0))],
        ),
        out_shape=[jax.ShapeDtypeStruct((E, cap, D), BF16),
                   jax.ShapeDtypeStruct((E, cap, LANES), F32)],
        compiler_params=_cparams(("parallel", "arbitrary")),
        name="gather",
    )(off_s, cnt_s, x_bf, pos4, gate4)


def _ffn_kernel(xe_ref, wg_ref, wu_ref, wd_ref, gs_ref, ye_ref, acc_sc, *, sub):
    fc = pl.program_id(2)
    rows = xe_ref.shape[1]
    @pl.when(fc == 0)
    def _():
        acc_sc[...] = jnp.zeros(acc_sc.shape, F32)

    wg = wg_ref[...].astype(BF16)
    wu = wu_ref[...].astype(BF16)
    wd = wd_ref[...].astype(BF16)
    for r in range(rows // sub):
        sl = pl.ds(r * sub, sub)
        xt = xe_ref[0, sl, :]
        hg = _dot(xt, wg)
        hu = _dot(xt, wu)
        hid = (hg * jax.nn.sigmoid(hg) * hu).astype(BF16)
        acc_sc[sl, :] = acc_sc[sl, :] + _dot(hid, wd)

    @pl.when(fc == pl.num_programs(2) - 1)
    def _():
        ye_ref[0] = (acc_sc[...] * gs_ref[0][:, :1]).astype(ye_ref.dtype)


def _ffn(xe, gs, w_gate, w_up, w_down, layer, *, halves=2, fcw=256):
    E, cap, D = xe.shape
    F = w_gate.shape[-1]
    rows = cap // halves
    sub = min(256, rows)
    return pl.pallas_call(
        functools.partial(_ffn_kernel, sub=sub),
        grid=(E, halves, F // fcw),
        in_specs=[
            pl.BlockSpec((1, rows, D), lambda e, t, f: (e, t, 0)),
            pl.BlockSpec((None, None, D, fcw), lambda e, t, f: (layer, e, 0, f)),
            pl.BlockSpec((None, None, D, fcw), lambda e, t, f: (layer, e, 0, f)),
            pl.BlockSpec((None, None, fcw, D), lambda e, t, f: (layer, e, f, 0)),
            pl.BlockSpec((1, rows, LANES), lambda e, t, f: (e, t, 0)),
        ],
        out_specs=pl.BlockSpec((1, rows, D), lambda e, t, f: (e, t, 0)),
        out_shape=jax.ShapeDtypeStruct((E, cap, D), BF16),
        scratch_shapes=[pltpu.VMEM((rows, D), F32)],
        compiler_params=_cparams(("parallel", "parallel", "arbitrary")),
        name="expert_ffn",
    )(xe, w_gate, w_up, w_down, gs)


def _combine_kernel(win_s, *refs, eg):
    y_refs = refs[:eg]
    post_ref, x_ref, g_ref, b_ref, xo_ref, xb_ref, acc_sc = refs[eg:]
    tb = pl.program_id(0)
    grp = pl.program_id(1)

    @pl.when(grp == 0)
    def _():
        acc_sc[...] = jnp.zeros(acc_sc.shape, F32)

    pt = post_ref[...].astype(F32)
    lane = lax.broadcasted_iota(I32, pt.shape, 1)
    col = lax.broadcasted_iota(I32, (LANES, GATHER_WIN), 1)
    total = None
    for k in range(eg):
        e = grp * eg + k
        pcol = jnp.sum(jnp.where(lane == e, pt, 0.0), axis=1, keepdims=True).astype(I32)
        hit = ((pcol - win_s[e, tb]) == col).astype(BF16)
        part = _dot(hit, y_refs[k][0])
        total = part if total is None else total + part
    acc_sc[...] = acc_sc[...] + total

    @pl.when(grp == pl.num_programs(1) - 1)
    def _():
        y = _layer_norm(DEEPNORM_ALPHA * x_ref[...] + acc_sc[...], g_ref[...], b_ref[...])
        xo_ref[...] = y
        xb_ref[...] = y.astype(BF16)


def _combine_ln(win_s, ye, pos_t, x, g, b, *, eg=16):
    S, D = x.shape
    E, cap, _ = ye.shape
    nb = S // LANES
    assert E % eg == 0

    def y_spec(k):
        return pl.BlockSpec((pl.Element(1), pl.Element(GATHER_WIN), pl.Element(D)),
                            lambda t, gr, ws: (gr * eg + k, (ws[gr * eg + k, t] // BF16_ROWS) * BF16_ROWS, 0))

    row = lambda t, gr, ws: (t, 0)
    fixed = lambda t, gr, ws: (0, 0)
    return pl.pallas_call(
        functools.partial(_combine_kernel, eg=eg),
        grid_spec=pltpu.PrefetchScalarGridSpec(
            num_scalar_prefetch=1,
            grid=(nb, E // eg),
            in_specs=([y_spec(k) for k in range(eg)]
                      + [pl.BlockSpec((LANES, E), row), pl.BlockSpec((LANES, D), row),
                         pl.BlockSpec((1, D), fixed), pl.BlockSpec((1, D), fixed)]),
            out_specs=[pl.BlockSpec((LANES, D), row), pl.BlockSpec((LANES, D), row)],
            scratch_shapes=[pltpu.VMEM((LANES, D), F32)],
        ),
        out_shape=[jax.ShapeDtypeStruct((S, D), F32), jax.ShapeDtypeStruct((S, D), BF16)],
        compiler_params=_cparams(("parallel", "arbitrary")),
        name="combine_ln",
    )(win_s, *([ye] * eg), pos_t, x, g, b)


def _moe(x, x_bf, logits_t, w_gate, w_up, w_down, layer, g, b):
    S, D = x.shape
    E = logits_t.shape[0]
    nb = S // LANES
    cap = EC_FACTOR * S // E
    pos, gate, off, cnt = _select(logits_t, cap)
    off_s = off[:, 0].reshape(E, nb)
    cnt_s = cnt[:, 0].reshape(E, nb)
    xe, gs = _gather(off_s, cnt_s, x_bf, pos, gate, cap)
    ye = _ffn(xe, gs, w_gate, w_up, w_down, layer)
    pos_t = pos.reshape(E, S).T
    win_s = jnp.minimum((off_s // BF16_ROWS) * BF16_ROWS, cap - GATHER_WIN)
    return _combine_ln(win_s, ye, pos_t, x, g, b)


def _mla_in_kernel(x_ref, w_ref, cos_ref, sin_ref, qn_ref, kvn_ref, cq_ref, ckv_ref, kr_ref):
    h = _dot(x_ref[...], w_ref[...])
    cq_ref[...] = _rms(h[:, :MLA_Q_RANK], qn_ref[...]).astype(BF16)
    ckv_ref[...] = _rms(h[:, MLA_Q_RANK:MLA_Q_RANK + MLA_KV_RANK], kvn_ref[...]).astype(BF16)
    kr = h[:, MLA_Q_RANK + MLA_KV_RANK:]
    kr_ref[...] = _rope(kr, cos_ref[...], sin_ref[...], MLA_ROPE // 4).astype(BF16)


def _mla_in(x_bf, w_bf, cos, sin, qn, kvn, *, tm=1024):
    S, K = x_bf.shape
    N = w_bf.shape[1]
    row = lambda i: (i, 0)
    fixed = lambda i: (0, 0)
    return pl.pallas_call(
        _mla_in_kernel,
        grid=(S // tm,),
        in_specs=[pl.BlockSpec((tm, K), row), pl.BlockSpec((K, N), fixed),
                  pl.BlockSpec((tm, LANES), row), pl.BlockSpec((tm, LANES), row),
                  pl.BlockSpec((1, MLA_Q_RANK), fixed), pl.BlockSpec((1, MLA_KV_RANK), fixed)],
        out_specs=[pl.BlockSpec((tm, MLA_Q_RANK), row), pl.BlockSpec((tm, MLA_KV_RANK), row),
                   pl.BlockSpec((tm, LANES), row)],
        out_shape=[jax.ShapeDtypeStruct((S, MLA_Q_RANK), BF16),
                   jax.ShapeDtypeStruct((S, MLA_KV_RANK), BF16),
                   jax.ShapeDtypeStruct((S, LANES), BF16)],
        compiler_params=_cparams(("parallel",)),
        name="mla_in",
    )(x_bf, w_bf, cos, sin, qn, kvn)


def _mla_q_kernel(c_ref, w_ref, cos_ref, sin_ref, o_ref, *, heads):
    h = _dot(c_ref[...], w_ref[...])
    scale = (MLA_NOPE + MLA_ROPE) ** -0.5 * LOG2E
    for c in range(heads):
        base = c * MLA_QK_PAD
        o_ref[c, :, :MLA_NOPE] = (h[:, base:base + MLA_NOPE] * scale).astype(BF16)
        rp = _rope(h[:, base + MLA_NOPE:base + MLA_QK_PAD], cos_ref[...], sin_ref[...], MLA_ROPE // 4)
        o_ref[c, :, MLA_NOPE:] = (rp * scale).astype(BF16)


def _mla_q(cq, w_bf, cos, sin, *, tm=1024, heads=2):
    S, K = cq.shape
    tn = heads * MLA_QK_PAD
    return pl.pallas_call(
        functools.partial(_mla_q_kernel, heads=heads),
        grid=(S // tm, MLA_HEADS // heads),
        in_specs=[pl.BlockSpec((tm, K), lambda i, j: (i, 0)),
                  pl.BlockSpec((K, tn), lambda i, j: (0, j)),
                  pl.BlockSpec((tm, LANES), lambda i, j: (i, 0)),
                  pl.BlockSpec((tm, LANES), lambda i, j: (i, 0))],
        out_specs=pl.BlockSpec((heads, tm, MLA_QK_PAD), lambda i, j: (j, i, 0)),
        out_shape=jax.ShapeDtypeStruct((MLA_HEADS, S, MLA_QK_PAD), BF16),
        compiler_params=_cparams(("parallel", "parallel")),
        name="mla_q",
    )(cq, w_bf, cos, sin)


def _mla_k_kernel(c_ref, wk_ref, kr_ref, k_ref, *, heads):
    kn = _dot(c_ref[...], wk_ref[...])
    for hh in range(heads):
        k_ref[hh, :, :MLA_NOPE] = kn[:, hh * MLA_NOPE:(hh + 1) * MLA_NOPE].astype(BF16)
        k_ref[hh, :, MLA_NOPE:] = kr_ref[...]


def _mla_k(ckv, wk_bf, kr, *, tm=1024, heads=4):
    S, K = ckv.shape
    return pl.pallas_call(
        functools.partial(_mla_k_kernel, heads=heads),
        grid=(S // tm, MLA_HEADS // heads),
        in_specs=[pl.BlockSpec((tm, K), lambda i, j: (i, 0)),
                  pl.BlockSpec((K, heads * MLA_NOPE), lambda i, j: (0, j)),
                  pl.BlockSpec((tm, LANES), lambda i, j: (i, 0))],
        out_specs=pl.BlockSpec((heads, tm, MLA_QK_PAD), lambda i, j: (j, i, 0)),
        out_shape=jax.ShapeDtypeStruct((MLA_HEADS, S, MLA_QK_PAD), BF16),
        compiler_params=_cparams(("parallel", "parallel")),
        name="mla_k",
    )(ckv, wk_bf, kr)


def kernel(x, ab_w_in, ab_q_norm, ab_k_norm, ab_lambda_q1, ab_lambda_k1, ab_lambda_q2, ab_lambda_k2, ab_subln, ab_w_out, mla_w_in, mla_q_norm, mla_kv_norm, mla_w_uq, mla_w_ukv, mla_w_out, ln_mix_g, ln_mix_b, moe_w_router, moe_w_gate, moe_w_up, moe_w_down, ln_ffn_g, ln_ffn_b):
    B, S, D = x.shape
    assert B == 1 and D == D_MODEL and S % 512 == 0
    xf = x.reshape(S, D)
    x_bf = xf.astype(BF16)
    cos_a, sin_a = _rope_tables(S, HEAD_DIM // 4, HEAD_DIM)
    cos_c, sin_c = _rope_tables(S, MLA_ROPE // 4, LANES)
    row2 = lambda v: v.reshape(1, -1)

    lam_init = 0.8 - 0.6 * math.exp(-0.3 * 0)
    w_in = ab_w_in[0].astype(BF16)
    c_av = A_Q + A_KV
    c_bq = c_av + A_KV
    c_bv = c_bq + 2 * B_QK
    w_qk = jnp.concatenate([w_in[:, :c_av], w_in[:, c_bq:c_bv]], axis=1)
    hq = _proj0(x_bf, w_qk, cos_a, sin_a, row2(ab_q_norm[0]), row2(ab_k_norm[0]))
    vt_a = _values_t(w_in[:, c_av:c_bq], x_bf, name="values_t_gqa")
    vt_b = _values_t(w_in[:, c_bv:], x_bf, name="values_t_diff")
    t_ak = A_Q // HEAD_DIM
    t_bq = t_ak + GQA_KV_HEADS
    t_bk = t_bq + 2 * DIFF_HEADS
    a_out = _flash(hq, hq, vt_a, n_kv=GQA_KV_HEADS, rep=GQA_Q_HEADS // GQA_KV_HEADS,
                   q_tile0=0, k_tile0=t_ak, dv=HEAD_DIM, bq=256, kc=2048, name="gqa_attention")
    b_out = _diff_attention(hq, vt_b, row2(ab_lambda_q1[0]), row2(ab_lambda_k1[0]),
                            row2(ab_lambda_q2[0]), row2(ab_lambda_k2[0]), row2(ab_subln[0]),
                            q_tile0=t_bq, k_tile0=t_bk, lam_init=lam_init)
    w_out = ab_w_out[0].astype(BF16)
    xf, x_bf, logits_t = _wout_ln_router(
        [a_out, b_out], [w_out[:A_Q], w_out[A_Q:]], xf, row2(ln_mix_g[0]), row2(ln_mix_b[0]),
        moe_w_router[0].T)
    xf, x_bf = _moe(xf, x_bf, logits_t, moe_w_gate, moe_w_up, moe_w_down, 0,
                    row2(ln_ffn_g[0]), row2(ln_ffn_b[0]))

    w1 = jnp.pad(mla_w_in[0], ((0, 0), (0, LANES - MLA_ROPE))).astype(BF16)
    cq, ckv, kr = _mla_in(x_bf, w1, cos_c, sin_c, row2(mla_q_norm[0]), row2(mla_kv_norm[0]))
    w_uq = mla_w_uq[0].reshape(MLA_Q_RANK, MLA_HEADS, MLA_NOPE + MLA_ROPE)
    w_uq = jnp.pad(w_uq, ((0, 0), (0, 0), (0, MLA_QK_PAD - MLA_NOPE - MLA_ROPE)))
    w_uq = w_uq.reshape(MLA_Q_RANK, MLA_HEADS * MLA_QK_PAD).astype(BF16)
    w_ukv = mla_w_ukv[0].reshape(MLA_KV_RANK, MLA_HEADS, MLA_NOPE + MLA_V)
    w_uk = w_ukv[:, :, :MLA_NOPE].reshape(MLA_KV_RANK, MLA_HEADS * MLA_NOPE).astype(BF16)
    w_uv = w_ukv[:, :, MLA_NOPE:].reshape(MLA_KV_RANK, MLA_HEADS * MLA_V)
    q_pad = _mla_q(cq, w_uq, cos_c, sin_c)
    k_pad = _mla_k(ckv, w_uk, kr)
    vt1 = _values_t(w_uv, ckv, name="values_t_mla")
    c_out = _flash(q_pad, k_pad, vt1, n_kv=MLA_HEADS, rep=1, q_tile0=0, k_tile0=0, dv=MLA_V,
                   bq=512, kc=2048, name="mla_attention")
    xf, x_bf, logits_t = _wout_ln_router(
        [c_out], [mla_w_out[0].astype(BF16)], xf, row2(ln_mix_g[1]), row2(ln_mix_b[1]),
        moe_w_router[1].T)
    xf, x_bf = _moe(xf, x_bf, logits_t, moe_w_gate, moe_w_up, moe_w_down, 1,
                    row2(ln_ffn_g[1]), row2(ln_ffn_b[1]))
    return xf.reshape(B, S, D)
```

```python
import functools
import math

import jax
import jax.numpy as jnp
from jax import lax
from jax.experimental import pallas as pl
from jax.experimental.pallas import tpu as pltpu

F32 = jnp.float32
BF16 = jnp.bfloat16
I32 = jnp.int32

D_MODEL = 2048
DEPTH = 2
GRID_W = 64
ROPE_THETA = 10000.0
NORM_EPS = 1e-6
LN_EPS = 1e-5

HEAD_DIM = 128
GQA_Q_HEADS = 8
GQA_KV_HEADS = 2
DIFF_HEADS = 4
A_Q = GQA_Q_HEADS * HEAD_DIM
A_KV = GQA_KV_HEADS * HEAD_DIM
B_QK = DIFF_HEADS * 2 * HEAD_DIM
B_V = DIFF_HEADS * 2 * HEAD_DIM

MLA_HEADS = 16
MLA_Q_RANK = 512
MLA_KV_RANK = 512
MLA_NOPE = 128
MLA_ROPE = 64
MLA_V = 128
MLA_QK_PAD = 256

N_EXPERTS = 16
EXPERT_FF = 2048
EC_FACTOR = 2

DEEPNORM_ALPHA = (2 * DEPTH) ** 0.25
LOG2E = math.log2(math.e)

LANES = 128
BF16_ROWS = 16
GATHER_WIN = LANES + BF16_ROWS
GATHER_SMALL = 32
VMEM_LIMIT = 56 * 1024 * 1024


def _cparams(sem, vmem=VMEM_LIMIT, flags=None):
    return pltpu.CompilerParams(dimension_semantics=sem, vmem_limit_bytes=vmem, flags=flags)


def _dot(a, b):
    return jnp.dot(a, b, preferred_element_type=F32)


def _dot_nt(a, b):
    return lax.dot_general(a, b, (((1,), (1,)), ((), ())), preferred_element_type=F32)


def _rope(y, cos, sin, half):
    n = y.shape[-1]
    lane = lax.broadcasted_iota(I32, y.shape, 1)
    up = pltpu.roll(y, n - half, 1)
    dn = pltpu.roll(y, half, 1)
    partner = jnp.where((lane % (2 * half)) < half, up, dn)
    return y * cos + partner * sin


def _rms(y, w):
    return y * lax.rsqrt(jnp.mean(y * y, axis=-1, keepdims=True) + NORM_EPS) * w


def _layer_norm(y, g, b):
    mu = jnp.mean(y, axis=-1, keepdims=True)
    d = y - mu
    var = jnp.mean(d * d, axis=-1, keepdims=True)
    return d * lax.rsqrt(var + LN_EPS) * g + b


def _rope_tables(S, half, width):
    n_rows = S // GRID_W
    n = 2 * half
    freqs = ROPE_THETA ** (-jnp.arange(0, n, 2, dtype=F32) / n)[None, :]
    parts_c, parts_s = [], []
    for count, along_rows in ((n_rows, True), (GRID_W, False)):
        ang = jnp.arange(count, dtype=F32)[:, None] * freqs
        c = jnp.concatenate([jnp.cos(ang)] * 2, axis=1)
        s = jnp.concatenate([-jnp.sin(ang), jnp.sin(ang)], axis=1)
        shape = (n_rows, GRID_W, n)
        expand = (lambda a: a[:, None, :]) if along_rows else (lambda a: a[None, :, :])
        parts_c.append(jnp.broadcast_to(expand(c), shape).reshape(S, n))
        parts_s.append(jnp.broadcast_to(expand(s), shape).reshape(S, n))
    pad = width - 4 * half
    if pad:
        parts_c.append(jnp.ones((S, pad), F32))
        parts_s.append(jnp.zeros((S, pad), F32))
    return jnp.concatenate(parts_c, axis=1), jnp.concatenate(parts_s, axis=1)


def _proj0_kernel(x_ref, w_ref, cos_ref, sin_ref, qn_ref, kn_ref, o_ref, *, tn):
    j = pl.program_id(1)
    h = _dot(x_ref[...], w_ref[...])
    nsub = tn // HEAD_DIM
    scale = HEAD_DIM ** -0.5 * LOG2E
    k_tile = A_Q // tn
    bq_lo = (A_Q + A_KV) // tn
    bq_hi = bq_lo + B_QK // tn

    def normed(c, w):
        y = _rms(h[:, c * HEAD_DIM:(c + 1) * HEAD_DIM], w)
        return _rope(y, cos_ref[...], sin_ref[...], HEAD_DIM // 4)

    @pl.when(j < k_tile)
    def _():
        for c in range(nsub):
            o_ref[c] = (normed(c, qn_ref[...]) * scale).astype(o_ref.dtype)

    @pl.when(j == k_tile)
    def _():
        for c in range(nsub):
            o_ref[c] = normed(c, kn_ref[...]).astype(o_ref.dtype)

    @pl.when(j > k_tile)
    def _():
        mul = jnp.where((j >= bq_lo) & (j < bq_hi), scale, 1.0).astype(F32)
        for c in range(nsub):
            o_ref[c] = (h[:, c * HEAD_DIM:(c + 1) * HEAD_DIM] * mul).astype(o_ref.dtype)


def _proj0(x_bf, w_bf, cos, sin, qn, kn, *, tm=1024, tn=256):
    S, K = x_bf.shape
    N = w_bf.shape[1]
    assert A_Q % tn == 0 and tn == A_KV and N % tn == 0
    return pl.pallas_call(
        functools.partial(_proj0_kernel, tn=tn),
        grid=(S // tm, N // tn),
        in_specs=[
            pl.BlockSpec((tm, K), lambda i, j: (i, 0)),
            pl.BlockSpec((K, tn), lambda i, j: (0, j)),
            pl.BlockSpec((tm, HEAD_DIM), lambda i, j: (i, 0)),
            pl.BlockSpec((tm, HEAD_DIM), lambda i, j: (i, 0)),
            pl.BlockSpec((1, HEAD_DIM), lambda i, j: (0, 0)),
            pl.BlockSpec((1, HEAD_DIM), lambda i, j: (0, 0)),
        ],
        out_specs=pl.BlockSpec((tn // HEAD_DIM, tm, HEAD_DIM), lambda i, j: (j, i, 0)),
        out_shape=jax.ShapeDtypeStruct((N // HEAD_DIM, S, HEAD_DIM), BF16),
        compiler_params=_cparams(("parallel", "parallel")),
        name="proj0",
    )(x_bf, w_bf, cos, sin, qn, kn)


def _values_t_kernel(wt_ref, x_ref, o_ref):
    o_ref[...] = _dot_nt(wt_ref[...], x_ref[...]).astype(o_ref.dtype)


def _values_t(w, x_bf, *, tm=1024, tn=256, name="values_t"):
    S, K = x_bf.shape
    N = w.shape[1]
    return pl.pallas_call(
        _values_t_kernel,
        grid=(S // tm, N // tn),
        in_specs=[pl.BlockSpec((tn, K), lambda i, j: (j, 0)),
                  pl.BlockSpec((tm, K), lambda i, j: (i, 0))],
        out_specs=pl.BlockSpec((tn, tm), lambda i, j: (j, i)),
        out_shape=jax.ShapeDtypeStruct((N, S), BF16),
        compiler_params=_cparams(("parallel", "parallel")),
        name=name,
    )(w.T.astype(BF16), x_bf)


def _online_softmax_pv(s, vt, m_ref, l_ref, acc_ref, shift=None):
    m_prev = m_ref[...]
    m_chunk = jnp.max(s, axis=0, keepdims=True)
    if shift is None:
        m_new = jnp.maximum(m_prev, m_chunk)
        p = jnp.exp2(s - m_new)
    else:
        m_new = jnp.maximum(m_prev, m_chunk + shift)
        p = jnp.exp2(s - (m_new - shift))
    alpha = jnp.exp2(m_prev - m_new)
    l_ref[...] = alpha * l_ref[...] + jnp.sum(p, axis=0, keepdims=True)
    acc_ref[...] = alpha * acc_ref[...] + _dot(vt, p.astype(BF16))
    m_ref[...] = m_new


def _flash_kernel(q_ref, qnext_ref, k_ref, vt_ref, o_ref, qt_sc, s_sc, m_sc, l_sc, acc_sc, *, kc):
    R, bq, D = q_ref.shape
    S = k_ref.shape[1]
    Dv = vt_ref.shape[0]
    n = S // kc
    assert n % 2 == 0
    cur = pl.program_id(1) % 2
    nxt = 1 - cur

    def scores(c):
        wrap = c >= n
        off = pl.multiple_of(jnp.where(wrap, 0, c) * kc, kc)
        return _dot(k_ref[0, pl.ds(off, kc), :], qt_sc[jnp.where(wrap, nxt, cur)])

    def values_t(c):
        return vt_ref[:, pl.ds(pl.multiple_of(c * kc, kc), kc)]

    @pl.when(pl.program_id(1) == 0)
    def _():
        qt_sc[0] = q_ref[...].reshape(R * bq, D).T
        s_sc[0] = scores(0)

    qt_sc[nxt] = qnext_ref[...].reshape(R * bq, D).T
    m_sc[...] = jnp.full(m_sc.shape, -jnp.inf, F32)
    l_sc[...] = jnp.zeros(l_sc.shape, F32)
    acc_sc[...] = jnp.zeros(acc_sc.shape, F32)

    def body(i, carry):
        c = 2 * i
        s_sc[1] = scores(c + 1)
        _online_softmax_pv(s_sc[0], values_t(c), m_sc, l_sc, acc_sc)
        s_sc[0] = scores(c + 2)
        _online_softmax_pv(s_sc[1], values_t(c + 1), m_sc, l_sc, acc_sc)
        return carry

    lax.fori_loop(0, n // 2, body, 0)
    out = (acc_sc[...] / l_sc[...]).T
    for r in range(R):
        o_ref[:, r * Dv:(r + 1) * Dv] = out[r * bq:(r + 1) * bq].astype(o_ref.dtype)


def _flash(q_arr, k_arr, vt_arr, *, n_kv, rep, q_tile0, k_tile0, dv, bq, kc, kv_buffers, name):
    _, S, D = q_arr.shape
    kc = min(kc, S // 2)
    bq = min(bq, S)
    assert q_tile0 % rep == 0
    N = rep * bq
    rows = dv
    nq = S // bq
    return pl.pallas_call(
        functools.partial(_flash_kernel, kc=kc),
        grid=(n_kv, nq),
        in_specs=[
            pl.BlockSpec((rep, bq, D), lambda g, i: (q_tile0 // rep + g, i, 0)),
            pl.BlockSpec((rep, bq, D), lambda g, i: (q_tile0 // rep + g, jnp.minimum(i + 1, nq - 1), 0)),
            pl.BlockSpec((1, S, D), lambda g, i: (k_tile0 + g, 0, 0), pipeline_mode=pl.Buffered(kv_buffers)),
            pl.BlockSpec((rows, vt_arr.shape[1]), lambda g, i: (g, 0), pipeline_mode=pl.Buffered(kv_buffers)),
        ],
        out_specs=pl.BlockSpec((bq, rep * dv), lambda g, i: (i, g)),
        out_shape=jax.ShapeDtypeStruct((S, n_kv * rep * dv), BF16),
        scratch_shapes=[pltpu.VMEM((2, D, N), BF16),
                        pltpu.VMEM((2, kc, N), F32),
                        pltpu.VMEM((1, N), F32), pltpu.VMEM((1, N), F32), pltpu.VMEM((rows, N), F32)],
        compiler_params=_cparams(("parallel", "arbitrary")),
        name=name,
    )(q_arr, q_arr, k_arr, vt_arr)


def _diff_kernel(q_ref, qnext_ref, k_ref, vt_ref, lq1_ref, lk1_ref, lq2_ref, lk2_ref, subln_ref, o_ref,
                 qt_sc, s_sc, m_sc, l_sc, acc_sc, *, kc, lam_init):
    _, bq, D = q_ref.shape
    S = k_ref.shape[1]
    h = pl.program_id(0)
    q0 = pl.program_id(1) * bq
    slope = jnp.float32(2.0 ** (-8.0 * DIFF_HEADS / DIFF_HEADS))
    for hh in range(DIFF_HEADS - 1):
        slope = jnp.where(h == hh, jnp.float32(2.0 ** (-8.0 * (hh + 1) / DIFF_HEADS)), slope)
    slope2 = slope * LOG2E
    rel = (lax.broadcasted_iota(I32, (kc, bq), 0) - lax.broadcasted_iota(I32, (kc, bq), 1)).astype(F32)
    rel_s = rel * slope2
    n = S // kc
    cur = pl.program_id(1) % 2
    nxt = 1 - cur

    def scores(j, c):
        wrap = c >= n
        off = pl.multiple_of(jnp.where(wrap, 0, c) * kc, kc)
        return _dot(k_ref[j, pl.ds(off, kc), :], qt_sc[jnp.where(wrap, nxt, cur), j])

    @pl.when(pl.program_id(1) == 0)
    def _():
        for j in range(2):
            qt_sc[0, j] = q_ref[j].T
        s_sc[0] = scores(0, 0)

    for j in range(2):
        qt_sc[nxt, j] = qnext_ref[j].T
    m_sc[...] = jnp.full(m_sc.shape, -jnp.inf, F32)
    l_sc[...] = jnp.zeros(l_sc.shape, F32)
    acc_sc[...] = jnp.zeros(acc_sc.shape, F32)

    def chunk_step(c, penalty):
        off = pl.multiple_of(c * kc, kc)
        delta = (off - q0).astype(F32)
        vt = vt_ref[:, pl.ds(off, kc)]
        s_sc[1] = scores(1, c)
        s0, shift = penalty(s_sc[0], delta)
        _online_softmax_pv(s0, vt, m_sc.at[0], l_sc.at[0], acc_sc.at[0], shift)
        s_sc[0] = scores(0, c + 1)
        s1, shift = penalty(s_sc[1], delta)
        _online_softmax_pv(s1, vt, m_sc.at[1], l_sc.at[1], acc_sc.at[1], shift)

    def keys_before(c, carry):
        chunk_step(c, lambda s, delta: (s + rel_s, slope2 * delta))
        return carry

    def keys_overlap(c, carry):
        chunk_step(c, lambda s, delta: (s - jnp.abs(rel + delta) * slope2, None))
        return carry

    def keys_after(c, carry):
        chunk_step(c, lambda s, delta: (s - rel_s, -(slope2 * delta)))
        return carry

    c_lo = q0 // kc
    c_hi = (q0 + bq + kc - 1) // kc
    lax.fori_loop(0, c_lo, keys_before, 0)
    lax.fori_loop(c_lo, c_hi, keys_overlap, 0)
    lax.fori_loop(c_hi, n, keys_after, 0)
    lam = (jnp.exp(jnp.sum(lq1_ref[...] * lk1_ref[...], axis=1, keepdims=True))
           - jnp.exp(jnp.sum(lq2_ref[...] * lk2_ref[...], axis=1, keepdims=True)) + lam_init)
    out = (acc_sc[0] / l_sc[0] - lam * (acc_sc[1] / l_sc[1])).T
    o_ref[...] = (_rms(out, subln_ref[...]) * (1.0 - lam_init)).astype(o_ref.dtype)


def _diff_attention(hq, vt_arr, lq1, lk1, lq2, lk2, subln, *, q_tile0, k_tile0, lam_init,
                    bq=512, kc=1024):
    _, S, D = hq.shape
    kc = min(kc, S)
    bq = min(bq, S)
    rows = 2 * D
    vec = pl.BlockSpec((1, D), lambda h, i: (0, 0))
    nq = S // bq
    return pl.pallas_call(
        functools.partial(_diff_kernel, kc=kc, lam_init=lam_init),
        grid=(DIFF_HEADS, nq),
        in_specs=[
            pl.BlockSpec((2, bq, D), lambda h, i: (q_tile0 // 2 + h, i, 0)),
            pl.BlockSpec((2, bq, D), lambda h, i: (q_tile0 // 2 + h, jnp.minimum(i + 1, nq - 1), 0)),
            pl.BlockSpec((2, S, D), lambda h, i: (k_tile0 // 2 + h, 0, 0)),
            pl.BlockSpec((rows, vt_arr.shape[1]), lambda h, i: (h, 0)),
            vec, vec, vec, vec,
            pl.BlockSpec((1, 2 * D), lambda h, i: (0, 0)),
        ],
        out_specs=pl.BlockSpec((bq, 2 * D), lambda h, i: (i, h)),
        out_shape=jax.ShapeDtypeStruct((S, DIFF_HEADS * 2 * D), BF16),
        scratch_shapes=[pltpu.VMEM((2, 2, D, bq), BF16), pltpu.VMEM((2, kc, bq), F32),
                        pltpu.VMEM((2, 1, bq), F32), pltpu.VMEM((2, 1, bq), F32),
                        pltpu.VMEM((2, rows, bq), F32)],
        compiler_params=_cparams(("parallel", "arbitrary")),
        name="diff_attention",
    )(hq, hq, hq, vt_arr, lq1, lk1, lq2, lk2, subln)


def _wout_kernel(*refs, n_in):
    a_refs = refs[:n_in]
    w_refs = refs[n_in:2 * n_in]
    x_ref, g_ref, b_ref, wr_ref, xo_ref, xb_ref, lg_ref = refs[2 * n_in:]
    mix = _dot(a_refs[0][...], w_refs[0][...])
    for a, w in zip(a_refs[1:], w_refs[1:]):
        mix = mix + _dot(a[...], w[...])
    y = _layer_norm(DEEPNORM_ALPHA * x_ref[...] + mix, g_ref[...], b_ref[...])
    xo_ref[...] = y
    y_hi = y.astype(BF16)
    xb_ref[...] = y_hi
    y_lo = (y - y_hi.astype(F32)).astype(BF16)
    wr = wr_ref[...]
    w_hi = wr.astype(BF16)
    w_lo = (wr - w_hi.astype(F32)).astype(BF16)
    E = wr.shape[0]
    both = _dot_nt(jnp.concatenate([w_hi, w_lo], axis=0), y_hi)
    lg_ref[...] = both[:E] + both[E:] + _dot_nt(w_hi, y_lo)


def _wout_ln_router(a_list, w_list, x, g, b, wr_t, *, tm=512):
    S, D = x.shape
    n_in = len(a_list)
    E = wr_t.shape[0]
    in_specs = ([pl.BlockSpec((tm, a.shape[1]), lambda i: (i, 0)) for a in a_list]
                + [pl.BlockSpec(w.shape, lambda i: (0, 0)) for w in w_list]
                + [pl.BlockSpec((tm, D), lambda i: (i, 0)),
                   pl.BlockSpec((1, D), lambda i: (0, 0)),
                   pl.BlockSpec((1, D), lambda i: (0, 0)),
                   pl.BlockSpec((E, D), lambda i: (0, 0))])
    return pl.pallas_call(
        functools.partial(_wout_kernel, n_in=n_in),
        grid=(S // tm,),
        in_specs=in_specs,
        out_specs=[pl.BlockSpec((tm, D), lambda i: (i, 0)),
                   pl.BlockSpec((tm, D), lambda i: (i, 0)),
                   pl.BlockSpec((E, tm), lambda i: (0, i))],
        out_shape=[jax.ShapeDtypeStruct((S, D), F32),
                   jax.ShapeDtypeStruct((S, D), BF16),
                   jax.ShapeDtypeStruct((E, S), F32)],
        compiler_params=_cparams(("parallel",)),
        name="wout_ln_router",
    )(*a_list, *w_list, x, g, b, wr_t)


def _select_kernel(lg_ref, pos_ref, gate_ref, off_ref, cnt_ref, *, E, nb, cap):
    lg = lg_ref[...].reshape(E, nb, LANES)
    mx = jnp.max(lg, axis=0, keepdims=True)
    ex = jnp.exp(lg - mx)
    aff = ex / jnp.sum(ex, axis=0, keepdims=True)
    bits = lax.bitcast_convert_type(aff, I32)

    def count(msk):
        c = jnp.sum(msk.astype(F32), axis=1, keepdims=True)
        return jnp.sum(c, axis=2, keepdims=True)

    def search(i, thr):
        cand = thr | lax.shift_left(jnp.int32(1), 30 - i)
        return jnp.where(count(bits >= cand) >= cap, cand, thr)

    thr = lax.fori_loop(0, 31, search, jnp.zeros((E, 1, 1), I32))
    gt = bits > thr
    eq = bits == thr
    need = cap - count(gt)

    col = lax.broadcasted_iota(I32, (LANES, LANES), 1)
    rw = lax.broadcasted_iota(I32, (LANES, LANES), 0)
    upper = (rw < col).astype(BF16)
    ones = jnp.ones((LANES, LANES), BF16)
    lower = (lax.broadcasted_iota(I32, (nb, nb), 1) < lax.broadcasted_iota(I32, (nb, nb), 0)).astype(BF16)

    def prefix(msk):
        m2 = jnp.where(msk, 1.0, 0.0).reshape(E * nb, LANES).astype(BF16)
        within = _dot(m2, upper).reshape(E, nb, LANES)
        tot = _dot(m2, ones).reshape(E, nb, LANES)
        offs = jnp.stack([_dot(lower, tot[e].astype(BF16)) for e in range(E)], axis=0)
        return within + offs, offs, tot

    eq_rank, _, _ = prefix(eq)
    sel = gt | (eq & (eq_rank < need))
    pos, offs, tot = prefix(sel)
    pos_ref[...] = jnp.where(sel, pos, -1.0).astype(I32).reshape(E * nb, LANES)
    gate_ref[...] = jnp.where(sel, aff, 0.0).reshape(E * nb, LANES)
    off_ref[...] = offs.astype(I32).reshape(E * nb, LANES)
    cnt_ref[...] = tot.astype(I32).reshape(E * nb, LANES)


def _select(logits_t, cap):
    E, S = logits_t.shape
    nb = S // LANES
    shp = (E * nb, LANES)
    full = pl.BlockSpec(shp, lambda: (0, 0))
    return pl.pallas_call(
        functools.partial(_select_kernel, E=E, nb=nb, cap=cap),
        in_specs=[full],
        out_specs=[full, full, full, full],
        out_shape=[jax.ShapeDtypeStruct(shp, I32), jax.ShapeDtypeStruct(shp, F32),
                   jax.ShapeDtypeStruct(shp, I32), jax.ShapeDtypeStruct(shp, I32)],
        compiler_params=pltpu.CompilerParams(vmem_limit_bytes=VMEM_LIMIT),
        name="select",
    )(logits_t.reshape(shp))


def _gather_kernel(off_s, cnt_s, x_ref, pos_ref, gate_ref, xe_ref, gs_ref, *, cap, tbs, ep):
    grp = pl.program_id(0)
    t = pl.program_id(1)

    @pl.when(t == 0)
    def _():
        xe_ref[...] = jnp.zeros(xe_ref.shape, xe_ref.dtype)
        gs_ref[...] = jnp.zeros(gs_ref.shape, gs_ref.dtype)

    def place(k, u, e, tb, width):
        base = jnp.minimum((off_s[e, tb] // BF16_ROWS) * BF16_ROWS, cap - width)
        base = pl.multiple_of(base, BF16_ROWS)
        slot = base + lax.broadcasted_iota(I32, (width, LANES), 0)
        hit = pos_ref[k, u] == slot
        rows = _dot(hit.astype(BF16), x_ref[pl.ds(u * LANES, LANES), :])
        win = pl.ds(base, width)
        xe_ref[k, win, :] = xe_ref[k, win, :] + rows.astype(xe_ref.dtype)
        g = jnp.sum(jnp.where(hit, gate_ref[k, u], 0.0), axis=1, keepdims=True)
        gs_ref[k, win, :] = gs_ref[k, win, :] + jnp.broadcast_to(g, (width, LANES))

    for u in range(tbs):
        for k in range(ep):
            e = grp * ep + k
            tb = t * tbs + u
            cnt = cnt_s[e, tb]

            @pl.when((cnt > 0) & (cnt <= GATHER_SMALL))
            def _():
                place(k, u, e, tb, GATHER_SMALL + BF16_ROWS)

            @pl.when(cnt > GATHER_SMALL)
            def _():
                place(k, u, e, tb, GATHER_WIN)


def _gather(off_s, cnt_s, x_bf, pos, gate, cap, *, tbs=8, ep=2):
    S, D = x_bf.shape
    E, nb = off_s.shape
    assert cap >= GATHER_WIN and cap % BF16_ROWS == 0 and nb % tbs == 0 and E % ep == 0
    pos4 = pos.reshape(E, nb, 1, LANES)
    gate4 = gate.reshape(E, nb, 1, LANES)
    return pl.pallas_call(
        functools.partial(_gather_kernel, cap=cap, tbs=tbs, ep=ep),
        grid_spec=pltpu.PrefetchScalarGridSpec(
            num_scalar_prefetch=2,
            grid=(E // ep, nb // tbs),
            in_specs=[
                pl.BlockSpec((tbs * LANES, D), lambda g, t, o, c: (t, 0)),
                pl.BlockSpec((ep, tbs, 1, LANES), lambda g, t, o, c: (g, t, 0, 0)),
                pl.BlockSpec((ep, tbs, 1, LANES), lambda g, t, o, c: (g, t, 0, 0)),
            ],
            out_specs=[pl.BlockSpec((ep, cap, D), lambda g, t, o, c: (g, 0, 0)),
                       pl.BlockSpec((ep, cap, LANES), lambda g, t, o, c: (g, 0, 0))],
        ),
        out_shape=[jax.ShapeDtypeStruct((E, cap, D), BF16),
                   jax.ShapeDtypeStruct((E, cap, LANES), F32)],
        compiler_params=_cparams(("parallel", "arbitrary")),
        name="gather",
    )(off_s, cnt_s, x_bf, pos4, gate4)


def _ffn_kernel(xe_ref, wg_ref, wu_ref, wd_ref, gs_ref, ye_ref, acc_sc, *, sub):
    fc = pl.program_id(2)
    rows = xe_ref.shape[1]
    @pl.when(fc == 0)
    def _():
        acc_sc[...] = jnp.zeros(acc_sc.shape, F32)

    wg = wg_ref[...].astype(BF16)
    wu = wu_ref[...].astype(BF16)
    wd = wd_ref[...].astype(BF16)
    for r in range(rows // sub):
        sl = pl.ds(r * sub, sub)
        xt = xe_ref[0, sl, :]
        hg = _dot(xt, wg)
        hu = _dot(xt, wu)
        hid = (hg * jax.nn.sigmoid(hg) * hu).astype(BF16)
        acc_sc[sl, :] = acc_sc[sl, :] + _dot(hid, wd)

    @pl.when(fc == pl.num_programs(2) - 1)
    def _():
        ye_ref[0] = (acc_sc[...] * gs_ref[0][:, :1]).astype(ye_ref.dtype)


def _ffn(xe, gs, w_gate, w_up, w_down, layer, *, halves=2, fcw=256):
    E, cap, D = xe.shape
    F = w_gate.shape[-1]
    rows = cap // halves
    sub = min(256, rows)
    return pl.pallas_call(
        functools.partial(_ffn_kernel, sub=sub),
        grid=(E, halves, F // fcw),
        in_specs=[
            pl.BlockSpec((1, rows, D), lambda e, t, f: (e, t, 0)),
            pl.BlockSpec((None, None, D, fcw), lambda e, t, f: (layer, e, 0, f)),
            pl.BlockSpec((None, None, D, fcw), lambda e, t, f: (layer, e, 0, f)),
            pl.BlockSpec((None, None, fcw, D), lambda e, t, f: (layer, e, f, 0)),
            pl.BlockSpec((1, rows, LANES), lambda e, t, f: (e, t, 0)),
        ],
        out_specs=pl.BlockSpec((1, rows, D), lambda e, t, f: (e, t, 0)),
        out_shape=jax.ShapeDtypeStruct((E, cap, D), BF16),
        scratch_shapes=[pltpu.VMEM((rows, D), F32)],
        compiler_params=_cparams(("parallel", "parallel", "arbitrary")),
        name="expert_ffn",
    )(xe, w_gate, w_up, w_down, gs)


def _combine_kernel(win_s, *refs, eg):
    y_refs = refs[:eg]
    post_ref, x_ref, g_ref, b_ref, xo_ref, xb_ref, acc_sc = refs[eg:]
    tb = pl.program_id(0)
    grp = pl.program_id(1)

    @pl.when(grp == 0)
    def _():
        acc_sc[...] = jnp.zeros(acc_sc.shape, F32)

    pt = post_ref[...].astype(F32)
    lane = lax.broadcasted_iota(I32, pt.shape, 1)
    col = lax.broadcasted_iota(I32, (LANES, GATHER_WIN), 1)
    total = None
    for k in range(eg):
        e = grp * eg + k
        pcol = jnp.sum(jnp.where(lane == e, pt, 0.0), axis=1, keepdims=True).astype(I32)
        hit = ((pcol - win_s[e, tb]) == col).astype(BF16)
        part = _dot(hit, y_refs[k][0])
        total = part if total is None else total + part
    acc_sc[...] = acc_sc[...] + total

    @pl.when(grp == pl.num_programs(1) - 1)
    def _():
        y = _layer_norm(DEEPNORM_ALPHA * x_ref[...] + acc_sc[...], g_ref[...], b_ref[...])
        xo_ref[...] = y
        xb_ref[...] = y.astype(BF16)


def _combine_ln(win_s, ye, pos_t, x, g, b, *, eg=16):
    S, D = x.shape
    E, cap, _ = ye.shape
    nb = S // LANES
    assert E % eg == 0

    def y_spec(k):
        return pl.BlockSpec((pl.Element(1), pl.Element(GATHER_WIN), pl.Element(D)),
                            lambda t, gr, ws: (gr * eg + k, (ws[gr * eg + k, t] // BF16_ROWS) * BF16_ROWS, 0))

    row = lambda t, gr, ws: (t, 0)
    fixed = lambda t, gr, ws: (0, 0)
    return pl.pallas_call(
        functools.partial(_combine_kernel, eg=eg),
        grid_spec=pltpu.PrefetchScalarGridSpec(
            num_scalar_prefetch=1,
            grid=(nb, E // eg),
            in_specs=([y_spec(k) for k in range(eg)]
                      + [pl.BlockSpec((LANES, E), row), pl.BlockSpec((LANES, D), row),
                         pl.BlockSpec((1, D), fixed), pl.BlockSpec((1, D), fixed)]),
            out_specs=[pl.BlockSpec((LANES, D), row), pl.BlockSpec((LANES, D), row)],
            scratch_shapes=[pltpu.VMEM((LANES, D), F32)],
        ),
        out_shape=[jax.ShapeDtypeStruct((S, D), F32), jax.ShapeDtypeStruct((S, D), BF16)],
        compiler_params=_cparams(("parallel", "arbitrary")),
        name="combine_ln",
    )(win_s, *([ye] * eg), pos_t, x, g, b)


def _moe(x, x_bf, logits_t, w_gate, w_up, w_down, layer, g, b):
    S, D = x.shape
    E = logits_t.shape[0]
    nb = S // LANES
    cap = EC_FACTOR * S // E
    pos, gate, off, cnt = _select(logits_t, cap)
    off_s = off[:, 0].reshape(E, nb)
    cnt_s = cnt[:, 0].reshape(E, nb)
    xe, gs = _gather(off_s, cnt_s, x_bf, pos, gate, cap)
    ye = _ffn(xe, gs, w_gate, w_up, w_down, layer)
    pos_t = pos.reshape(E, S).T
    win_s = jnp.minimum((off_s // BF16_ROWS) * BF16_ROWS, cap - GATHER_WIN)
    return _combine_ln(win_s, ye, pos_t, x, g, b)


def _mla_in_kernel(x_ref, w_ref, cos_ref, sin_ref, qn_ref, kvn_ref, cq_ref, ckv_ref, kr_ref):
    h = _dot(x_ref[...], w_ref[...])
    cq_ref[...] = _rms(h[:, :MLA_Q_RANK], qn_ref[...]).astype(BF16)
    ckv_ref[...] = _rms(h[:, MLA_Q_RANK:MLA_Q_RANK + MLA_KV_RANK], kvn_ref[...]).astype(BF16)
    kr = h[:, MLA_Q_RANK + MLA_KV_RANK:]
    kr_ref[...] = _rope(kr, cos_ref[...], sin_ref[...], MLA_ROPE // 4).astype(BF16)


def _mla_in(x_bf, w_bf, cos, sin, qn, kvn, *, tm=1024):
    S, K = x_bf.shape
    N = w_bf.shape[1]
    row = lambda i: (i, 0)
    fixed = lambda i: (0, 0)
    return pl.pallas_call(
        _mla_in_kernel,
        grid=(S // tm,),
        in_specs=[pl.BlockSpec((tm, K), row), pl.BlockSpec((K, N), fixed),
                  pl.BlockSpec((tm, LANES), row), pl.BlockSpec((tm, LANES), row),
                  pl.BlockSpec((1, MLA_Q_RANK), fixed), pl.BlockSpec((1, MLA_KV_RANK), fixed)],
        out_specs=[pl.BlockSpec((tm, MLA_Q_RANK), row), pl.BlockSpec((tm, MLA_KV_RANK), row),
                   pl.BlockSpec((tm, LANES), row)],
        out_shape=[jax.ShapeDtypeStruct((S, MLA_Q_RANK), BF16),
                   jax.ShapeDtypeStruct((S, MLA_KV_RANK), BF16),
                   jax.ShapeDtypeStruct((S, LANES), BF16)],
        compiler_params=_cparams(("parallel",)),
        name="mla_in",
    )(x_bf, w_bf, cos, sin, qn, kvn)


def _mla_q_kernel(c_ref, w_ref, cos_ref, sin_ref, o_ref, *, heads):
    h = _dot(c_ref[...], w_ref[...])
    scale = (MLA_NOPE + MLA_ROPE) ** -0.5 * LOG2E
    for c in range(heads):
        base = c * MLA_QK_PAD
        o_ref[c, :, :MLA_NOPE] = (h[:, base:base + MLA_NOPE] * scale).astype(BF16)
        rp = _rope(h[:, base + MLA_NOPE:base + MLA_QK_PAD], cos_ref[...], sin_ref[...], MLA_ROPE // 4)
        o_ref[c, :, MLA_NOPE:] = (rp * scale).astype(BF16)


def _mla_q(cq, w_bf, cos, sin, *, tm=1024, heads=4):
    S, K = cq.shape
    tn = heads * MLA_QK_PAD
    return pl.pallas_call(
        functools.partial(_mla_q_kernel, heads=heads),
        grid=(S // tm, MLA_HEADS // heads),
        in_specs=[pl.BlockSpec((tm, K), lambda i, j: (i, 0)),
                  pl.BlockSpec((K, tn), lambda i, j: (0, j)),
                  pl.BlockSpec((tm, LANES), lambda i, j: (i, 0)),
                  pl.BlockSpec((tm, LANES), lambda i, j: (i, 0))],
        out_specs=pl.BlockSpec((heads, tm, MLA_QK_PAD), lambda i, j: (j, i, 0)),
        out_shape=jax.ShapeDtypeStruct((MLA_HEADS, S, MLA_QK_PAD), BF16),
        compiler_params=_cparams(("parallel", "parallel")),
        name="mla_q",
    )(cq, w_bf, cos, sin)


def _mla_k_kernel(c_ref, wk_ref, kr_ref, k_ref, *, heads):
    kn = _dot(c_ref[...], wk_ref[...])
    for hh in range(heads):
        k_ref[hh, :, :MLA_NOPE] = kn[:, hh * MLA_NOPE:(hh + 1) * MLA_NOPE].astype(BF16)
        k_ref[hh, :, MLA_NOPE:] = kr_ref[...]


def _mla_k(ckv, wk_bf, kr, *, tm=1024, heads=4):
    S, K = ckv.shape
    return pl.pallas_call(
        functools.partial(_mla_k_kernel, heads=heads),
        grid=(S // tm, MLA_HEADS // heads),
        in_specs=[pl.BlockSpec((tm, K), lambda i, j: (i, 0)),
                  pl.BlockSpec((K, heads * MLA_NOPE), lambda i, j: (0, j)),
                  pl.BlockSpec((tm, LANES), lambda i, j: (i, 0))],
        out_specs=pl.BlockSpec((heads, tm, MLA_QK_PAD), lambda i, j: (j, i, 0)),
        out_shape=jax.ShapeDtypeStruct((MLA_HEADS, S, MLA_QK_PAD), BF16),
        compiler_params=_cparams(("parallel", "parallel")),
        name="mla_k",
    )(ckv, wk_bf, kr)


def kernel(x, ab_w_in, ab_q_norm, ab_k_norm, ab_lambda_q1, ab_lambda_k1, ab_lambda_q2, ab_lambda_k2, ab_subln, ab_w_out, mla_w_in, mla_q_norm, mla_kv_norm, mla_w_uq, mla_w_ukv, mla_w_out, ln_mix_g, ln_mix_b, moe_w_router, moe_w_gate, moe_w_up, moe_w_down, ln_ffn_g, ln_ffn_b):
    B, S, D = x.shape
    assert B == 1 and D == D_MODEL and S % 512 == 0
    xf = x.reshape(S, D)
    x_bf = xf.astype(BF16)
    cos_a, sin_a = _rope_tables(S, HEAD_DIM // 4, HEAD_DIM)
    cos_c, sin_c = _rope_tables(S, MLA_ROPE // 4, LANES)
    row2 = lambda v: v.reshape(1, -1)

    lam_init = 0.8 - 0.6 * math.exp(-0.3 * 0)
    w_in = ab_w_in[0].astype(BF16)
    c_av = A_Q + A_KV
    c_bq = c_av + A_KV
    c_bv = c_bq + 2 * B_QK
    w_qk = jnp.concatenate([w_in[:, :c_av], w_in[:, c_bq:c_bv]], axis=1)
    hq = _proj0(x_bf, w_qk, cos_a, sin_a, row2(ab_q_norm[0]), row2(ab_k_norm[0]))
    vt_a = _values_t(w_in[:, c_av:c_bq], x_bf, name="values_t_gqa")
    vt_b = _values_t(w_in[:, c_bv:], x_bf, name="values_t_diff")
    t_ak = A_Q // HEAD_DIM
    t_bq = t_ak + GQA_KV_HEADS
    t_bk = t_bq + 2 * DIFF_HEADS
    a_out = _flash(hq, hq, vt_a, n_kv=GQA_KV_HEADS, rep=GQA_Q_HEADS // GQA_KV_HEADS,
                   q_tile0=0, k_tile0=t_ak, dv=HEAD_DIM, bq=256, kc=4096, kv_buffers=1, name="gqa_attention")
    b_out = _diff_attention(hq, vt_b, row2(ab_lambda_q1[0]), row2(ab_lambda_k1[0]),
                            row2(ab_lambda_q2[0]), row2(ab_lambda_k2[0]), row2(ab_subln[0]),
                            q_tile0=t_bq, k_tile0=t_bk, lam_init=lam_init)
    w_out = ab_w_out[0].astype(BF16)
    xf, x_bf, logits_t = _wout_ln_router(
        [a_out, b_out], [w_out[:A_Q], w_out[A_Q:]], xf, row2(ln_mix_g[0]), row2(ln_mix_b[0]),
        moe_w_router[0].T)
    xf, x_bf = _moe(xf, x_bf, logits_t, moe_w_gate, moe_w_up, moe_w_down, 0,
                    row2(ln_ffn_g[0]), row2(ln_ffn_b[0]))

    w1 = jnp.pad(mla_w_in[0], ((0, 0), (0, LANES - MLA_ROPE))).astype(BF16)
    cq, ckv, kr = _mla_in(x_bf, w1, cos_c, sin_c, row2(mla_q_norm[0]), row2(mla_kv_norm[0]))
    w_uq = mla_w_uq[0].reshape(MLA_Q_RANK, MLA_HEADS, MLA_NOPE + MLA_ROPE)
    w_uq = jnp.pad(w_uq, ((0, 0), (0, 0), (0, MLA_QK_PAD - MLA_NOPE - MLA_ROPE)))
    w_uq = w_uq.reshape(MLA_Q_RANK, MLA_HEADS * MLA_QK_PAD).astype(BF16)
    w_ukv = mla_w_ukv[0].reshape(MLA_KV_RANK, MLA_HEADS, MLA_NOPE + MLA_V)
    w_uk = w_ukv[:, :, :MLA_NOPE].reshape(MLA_KV_RANK, MLA_HEADS * MLA_NOPE).astype(BF16)
    w_uv = w_ukv[:, :, MLA_NOPE:].reshape(MLA_KV_RANK, MLA_HEADS * MLA_V)
    q_pad = _mla_q(cq, w_uq, cos_c, sin_c)
    k_pad = _mla_k(ckv, w_uk, kr)
    vt1 = _values_t(w_uv, ckv, name="values_t_mla")
    c_out = _flash(q_pad, k_pad, vt1, n_kv=MLA_HEADS, rep=1, q_tile0=0, k_tile0=0, dv=MLA_V,
                   bq=512, kc=4096, kv_buffers=2, name="mla_attention")
    xf, x_bf, logits_t = _wout_ln_router(
        [c_out], [mla_w_out[0].astype(BF16)], xf, row2(ln_mix_g[1]), row2(ln_mix_b[1]),
        moe_w_router[1].T)
    xf, x_bf = _moe(xf, x_bf, logits_t, moe_w_gate, moe_w_up, moe_w_down, 1,
                    row2(ln_ffn_g[1]), row2(ln_ffn_b[1]))
    return xf.reshape(B, S, D)
```

```python
import functools
import math

import jax
import jax.numpy as jnp
from jax import lax
from jax.experimental import pallas as pl
from jax.experimental.pallas import tpu as pltpu

F32 = jnp.float32
BF16 = jnp.bfloat16
I32 = jnp.int32

D_MODEL = 2048
DEPTH = 2
GRID_W = 64
ROPE_THETA = 10000.0
NORM_EPS = 1e-6
LN_EPS = 1e-5

HEAD_DIM = 128
GQA_Q_HEADS = 8
GQA_KV_HEADS = 2
DIFF_HEADS = 4
A_Q = GQA_Q_HEADS * HEAD_DIM
A_KV = GQA_KV_HEADS * HEAD_DIM
B_QK = DIFF_HEADS * 2 * HEAD_DIM
B_V = DIFF_HEADS * 2 * HEAD_DIM

MLA_HEADS = 16
MLA_Q_RANK = 512
MLA_KV_RANK = 512
MLA_NOPE = 128
MLA_ROPE = 64
MLA_V = 128
MLA_QK_PAD = 256

N_EXPERTS = 16
EXPERT_FF = 2048
EC_FACTOR = 2

DEEPNORM_ALPHA = (2 * DEPTH) ** 0.25
LOG2E = math.log2(math.e)

LANES = 128
BF16_ROWS = 16
GATHER_WIN = LANES + BF16_ROWS
GATHER_SMALL = 32
VMEM_LIMIT = 56 * 1024 * 1024


def _cparams(sem, vmem=VMEM_LIMIT, flags=None):
    return pltpu.CompilerParams(dimension_semantics=sem, vmem_limit_bytes=vmem, flags=flags)


def _dot(a, b):
    return jnp.dot(a, b, preferred_element_type=F32)


def _dot_nt(a, b):
    return lax.dot_general(a, b, (((1,), (1,)), ((), ())), preferred_element_type=F32)


def _rope(y, cos, sin, half):
    n = y.shape[-1]
    lane = lax.broadcasted_iota(I32, y.shape, 1)
    up = pltpu.roll(y, n - half, 1)
    dn = pltpu.roll(y, half, 1)
    partner = jnp.where((lane % (2 * half)) < half, up, dn)
    return y * cos + partner * sin


def _rms(y, w):
    return y * lax.rsqrt(jnp.mean(y * y, axis=-1, keepdims=True) + NORM_EPS) * w


def _layer_norm(y, g, b):
    mu = jnp.mean(y, axis=-1, keepdims=True)
    d = y - mu
    var = jnp.mean(d * d, axis=-1, keepdims=True)
    return d * lax.rsqrt(var + LN_EPS) * g + b


def _rope_tables(S, half, width):
    n_rows = S // GRID_W
    n = 2 * half
    freqs = ROPE_THETA ** (-jnp.arange(0, n, 2, dtype=F32) / n)[None, :]
    parts_c, parts_s = [], []
    for count, along_rows in ((n_rows, True), (GRID_W, False)):
        ang = jnp.arange(count, dtype=F32)[:, None] * freqs
        c = jnp.concatenate([jnp.cos(ang)] * 2, axis=1)
        s = jnp.concatenate([-jnp.sin(ang), jnp.sin(ang)], axis=1)
        shape = (n_rows, GRID_W, n)
        expand = (lambda a: a[:, None, :]) if along_rows else (lambda a: a[None, :, :])
        parts_c.append(jnp.broadcast_to(expand(c), shape).reshape(S, n))
        parts_s.append(jnp.broadcast_to(expand(s), shape).reshape(S, n))
    pad = width - 4 * half
    if pad:
        parts_c.append(jnp.ones((S, pad), F32))
        parts_s.append(jnp.zeros((S, pad), F32))
    return jnp.concatenate(parts_c, axis=1), jnp.concatenate(parts_s, axis=1)


def _proj0_kernel(x_ref, w_ref, cos_ref, sin_ref, qn_ref, kn_ref, o_ref, *, tn):
    j = pl.program_id(1)
    h = _dot(x_ref[...], w_ref[...])
    nsub = tn // HEAD_DIM
    scale = HEAD_DIM ** -0.5 * LOG2E
    k_tile = A_Q // tn
    bq_lo = (A_Q + A_KV) // tn
    bq_hi = bq_lo + B_QK // tn

    def normed(c, w):
        y = _rms(h[:, c * HEAD_DIM:(c + 1) * HEAD_DIM], w)
        return _rope(y, cos_ref[...], sin_ref[...], HEAD_DIM // 4)

    @pl.when(j < k_tile)
    def _():
        for c in range(nsub):
            o_ref[c] = (normed(c, qn_ref[...]) * scale).astype(o_ref.dtype)

    @pl.when(j == k_tile)
    def _():
        for c in range(nsub):
            o_ref[c] = normed(c, kn_ref[...]).astype(o_ref.dtype)

    @pl.when(j > k_tile)
    def _():
        mul = jnp.where((j >= bq_lo) & (j < bq_hi), scale, 1.0).astype(F32)
        for c in range(nsub):
            o_ref[c] = (h[:, c * HEAD_DIM:(c + 1) * HEAD_DIM] * mul).astype(o_ref.dtype)


def _proj0(x_bf, w_bf, cos, sin, qn, kn, *, tm=1024, tn=256):
    S, K = x_bf.shape
    N = w_bf.shape[1]
    assert A_Q % tn == 0 and tn == A_KV and N % tn == 0
    return pl.pallas_call(
        functools.partial(_proj0_kernel, tn=tn),
        grid=(S // tm, N // tn),
        in_specs=[
            pl.BlockSpec((tm, K), lambda i, j: (i, 0)),
            pl.BlockSpec((K, tn), lambda i, j: (0, j)),
            pl.BlockSpec((tm, HEAD_DIM), lambda i, j: (i, 0)),
            pl.BlockSpec((tm, HEAD_DIM), lambda i, j: (i, 0)),
            pl.BlockSpec((1, HEAD_DIM), lambda i, j: (0, 0)),
            pl.BlockSpec((1, HEAD_DIM), lambda i, j: (0, 0)),
        ],
        out_specs=pl.BlockSpec((tn // HEAD_DIM, tm, HEAD_DIM), lambda i, j: (j, i, 0)),
        out_shape=jax.ShapeDtypeStruct((N // HEAD_DIM, S, HEAD_DIM), BF16),
        compiler_params=_cparams(("parallel", "parallel")),
        name="proj0",
    )(x_bf, w_bf, cos, sin, qn, kn)


def _values_t_kernel(wt_ref, x_ref, o_ref):
    o_ref[...] = _dot_nt(wt_ref[...], x_ref[...]).astype(o_ref.dtype)


def _values_t(w, x_bf, *, tm=1024, tn=256, name="values_t"):
    S, K = x_bf.shape
    N = w.shape[1]
    return pl.pallas_call(
        _values_t_kernel,
        grid=(S // tm, N // tn),
        in_specs=[pl.BlockSpec((tn, K), lambda i, j: (j, 0)),
                  pl.BlockSpec((tm, K), lambda i, j: (i, 0))],
        out_specs=pl.BlockSpec((tn, tm), lambda i, j: (j, i)),
        out_shape=jax.ShapeDtypeStruct((N, S), BF16),
        compiler_params=_cparams(("parallel", "parallel")),
        name=name,
    )(w.T.astype(BF16), x_bf)


def _online_softmax_pv(s, vt, m_ref, l_ref, acc_ref, shift=None):
    m_prev = m_ref[...]
    m_chunk = jnp.max(s, axis=0, keepdims=True)
    if shift is None:
        m_new = jnp.maximum(m_prev, m_chunk)
        p = jnp.exp2(s - m_new)
    else:
        m_new = jnp.maximum(m_prev, m_chunk + shift)
        p = jnp.exp2(s - (m_new - shift))
    alpha = jnp.exp2(m_prev - m_new)
    l_ref[...] = alpha * l_ref[...] + jnp.sum(p, axis=0, keepdims=True)
    acc_ref[...] = alpha * acc_ref[...] + _dot(vt, p.astype(BF16))
    m_ref[...] = m_new


def _flash_kernel(q_ref, qnext_ref, k_ref, vt_ref, o_ref, qt_sc, s_sc, m_sc, l_sc, acc_sc, *, kc):
    R, bq, D = q_ref.shape
    S = k_ref.shape[1]
    Dv = vt_ref.shape[0]
    n = S // kc
    assert n % 2 == 0
    cur = pl.program_id(1) % 2
    nxt = 1 - cur

    def scores(c):
        wrap = c >= n
        off = pl.multiple_of(jnp.where(wrap, 0, c) * kc, kc)
        return _dot(k_ref[0, pl.ds(off, kc), :], qt_sc[jnp.where(wrap, nxt, cur)])

    def values_t(c):
        return vt_ref[:, pl.ds(pl.multiple_of(c * kc, kc), kc)]

    @pl.when(pl.program_id(1) == 0)
    def _():
        qt_sc[0] = q_ref[...].reshape(R * bq, D).T
        s_sc[0] = scores(0)

    qt_sc[nxt] = qnext_ref[...].reshape(R * bq, D).T
    m_sc[...] = jnp.full(m_sc.shape, -jnp.inf, F32)
    l_sc[...] = jnp.zeros(l_sc.shape, F32)
    acc_sc[...] = jnp.zeros(acc_sc.shape, F32)

    def body(i, carry):
        c = 2 * i
        s_sc[1] = scores(c + 1)
        _online_softmax_pv(s_sc[0], values_t(c), m_sc, l_sc, acc_sc)
        s_sc[0] = scores(c + 2)
        _online_softmax_pv(s_sc[1], values_t(c + 1), m_sc, l_sc, acc_sc)
        return carry

    lax.fori_loop(0, n // 2, body, 0)
    out = (acc_sc[...] / l_sc[...]).T
    for r in range(R):
        o_ref[:, r * Dv:(r + 1) * Dv] = out[r * bq:(r + 1) * bq].astype(o_ref.dtype)


def _flash(q_arr, k_arr, vt_arr, *, n_kv, rep, q_tile0, k_tile0, dv, bq, kc, kv_buffers, name):
    _, S, D = q_arr.shape
    kc = min(kc, S // 2)
    bq = min(bq, S)
    assert q_tile0 % rep == 0
    N = rep * bq
    rows = dv
    nq = S // bq
    return pl.pallas_call(
        functools.partial(_flash_kernel, kc=kc),
        grid=(n_kv, nq),
        in_specs=[
            pl.BlockSpec((rep, bq, D), lambda g, i: (q_tile0 // rep + g, i, 0)),
            pl.BlockSpec((rep, bq, D), lambda g, i: (q_tile0 // rep + g, jnp.minimum(i + 1, nq - 1), 0)),
            pl.BlockSpec((1, S, D), lambda g, i: (k_tile0 + g, 0, 0), pipeline_mode=pl.Buffered(kv_buffers)),
            pl.BlockSpec((rows, vt_arr.shape[1]), lambda g, i: (g, 0), pipeline_mode=pl.Buffered(kv_buffers)),
        ],
        out_specs=pl.BlockSpec((bq, rep * dv), lambda g, i: (i, g)),
        out_shape=jax.ShapeDtypeStruct((S, n_kv * rep * dv), BF16),
        scratch_shapes=[pltpu.VMEM((2, D, N), BF16),
                        pltpu.VMEM((2, kc, N), F32),
                        pltpu.VMEM((1, N), F32), pltpu.VMEM((1, N), F32), pltpu.VMEM((rows, N), F32)],
        compiler_params=_cparams(("parallel", "arbitrary")),
        name=name,
    )(q_arr, q_arr, k_arr, vt_arr)


def _diff_kernel(q_ref, qnext_ref, k_ref, vt_ref, lq1_ref, lk1_ref, lq2_ref, lk2_ref, subln_ref, o_ref,
                 qt_sc, s_sc, m_sc, l_sc, acc_sc, *, kc, lam_init):
    _, bq, D = q_ref.shape
    S = k_ref.shape[1]
    h = pl.program_id(0)
    q0 = pl.program_id(1) * bq
    slope = jnp.float32(2.0 ** (-8.0 * DIFF_HEADS / DIFF_HEADS))
    for hh in range(DIFF_HEADS - 1):
        slope = jnp.where(h == hh, jnp.float32(2.0 ** (-8.0 * (hh + 1) / DIFF_HEADS)), slope)
    slope2 = slope * LOG2E
    rel = (lax.broadcasted_iota(I32, (kc, bq), 0) - lax.broadcasted_iota(I32, (kc, bq), 1)).astype(F32)
    rel_s = rel * slope2
    n = S // kc
    cur = pl.program_id(1) % 2
    nxt = 1 - cur

    def scores(j, c):
        wrap = c >= n
        off = pl.multiple_of(jnp.where(wrap, 0, c) * kc, kc)
        return _dot(k_ref[j, pl.ds(off, kc), :], qt_sc[jnp.where(wrap, nxt, cur), j])

    @pl.when(pl.program_id(1) == 0)
    def _():
        for j in range(2):
            qt_sc[0, j] = q_ref[j].T
        s_sc[0] = scores(0, 0)

    for j in range(2):
        qt_sc[nxt, j] = qnext_ref[j].T
    m_sc[...] = jnp.full(m_sc.shape, -jnp.inf, F32)
    l_sc[...] = jnp.zeros(l_sc.shape, F32)
    acc_sc[...] = jnp.zeros(acc_sc.shape, F32)

    def chunk_step(c, penalty):
        off = pl.multiple_of(c * kc, kc)
        delta = (off - q0).astype(F32)
        vt = vt_ref[:, pl.ds(off, kc)]
        s_sc[1] = scores(1, c)
        s0, shift = penalty(s_sc[0], delta)
        _online_softmax_pv(s0, vt, m_sc.at[0], l_sc.at[0], acc_sc.at[0], shift)
        s_sc[0] = scores(0, c + 1)
        s1, shift = penalty(s_sc[1], delta)
        _online_softmax_pv(s1, vt, m_sc.at[1], l_sc.at[1], acc_sc.at[1], shift)

    def keys_before(c, carry):
        chunk_step(c, lambda s, delta: (s + rel_s, slope2 * delta))
        return carry

    def keys_overlap(c, carry):
        chunk_step(c, lambda s, delta: (s - jnp.abs(rel + delta) * slope2, None))
        return carry

    def keys_after(c, carry):
        chunk_step(c, lambda s, delta: (s - rel_s, -(slope2 * delta)))
        return carry

    c_lo = q0 // kc
    c_hi = (q0 + bq + kc - 1) // kc
    lax.fori_loop(0, c_lo, keys_before, 0)
    lax.fori_loop(c_lo, c_hi, keys_overlap, 0)
    lax.fori_loop(c_hi, n, keys_after, 0)
    lam = (jnp.exp(jnp.sum(lq1_ref[...] * lk1_ref[...], axis=1, keepdims=True))
           - jnp.exp(jnp.sum(lq2_ref[...] * lk2_ref[...], axis=1, keepdims=True)) + lam_init)
    out = (acc_sc[0] / l_sc[0] - lam * (acc_sc[1] / l_sc[1])).T
    o_ref[...] = (_rms(out, subln_ref[...]) * (1.0 - lam_init)).astype(o_ref.dtype)


def _diff_attention(hq, vt_arr, lq1, lk1, lq2, lk2, subln, *, q_tile0, k_tile0, lam_init,
                    bq=512, kc=2048):
    _, S, D = hq.shape
    kc = min(kc, S)
    bq = min(bq, S)
    rows = 2 * D
    vec = pl.BlockSpec((1, D), lambda h, i: (0, 0))
    nq = S // bq
    return pl.pallas_call(
        functools.partial(_diff_kernel, kc=kc, lam_init=lam_init),
        grid=(DIFF_HEADS, nq),
        in_specs=[
            pl.BlockSpec((2, bq, D), lambda h, i: (q_tile0 // 2 + h, i, 0)),
            pl.BlockSpec((2, bq, D), lambda h, i: (q_tile0 // 2 + h, jnp.minimum(i + 1, nq - 1), 0)),
            pl.BlockSpec((2, S, D), lambda h, i: (k_tile0 // 2 + h, 0, 0), pipeline_mode=pl.Buffered(1)),
            pl.BlockSpec((rows, vt_arr.shape[1]), lambda h, i: (h, 0), pipeline_mode=pl.Buffered(1)),
            vec, vec, vec, vec,
            pl.BlockSpec((1, 2 * D), lambda h, i: (0, 0)),
        ],
        out_specs=pl.BlockSpec((bq, 2 * D), lambda h, i: (i, h)),
        out_shape=jax.ShapeDtypeStruct((S, DIFF_HEADS * 2 * D), BF16),
        scratch_shapes=[pltpu.VMEM((2, 2, D, bq), BF16), pltpu.VMEM((2, kc, bq), F32),
                        pltpu.VMEM((2, 1, bq), F32), pltpu.VMEM((2, 1, bq), F32),
                        pltpu.VMEM((2, rows, bq), F32)],
        compiler_params=_cparams(("parallel", "arbitrary")),
        name="diff_attention",
    )(hq, hq, hq, vt_arr, lq1, lk1, lq2, lk2, subln)


def _wout_kernel(*refs, n_in):
    a_refs = refs[:n_in]
    w_refs = refs[n_in:2 * n_in]
    x_ref, g_ref, b_ref, wr_ref, xo_ref, xb_ref, lg_ref = refs[2 * n_in:]
    mix = _dot(a_refs[0][...], w_refs[0][...])
    for a, w in zip(a_refs[1:], w_refs[1:]):
        mix = mix + _dot(a[...], w[...])
    y = _layer_norm(DEEPNORM_ALPHA * x_ref[...] + mix, g_ref[...], b_ref[...])
    xo_ref[...] = y
    y_hi = y.astype(BF16)
    xb_ref[...] = y_hi
    y_lo = (y - y_hi.astype(F32)).astype(BF16)
    wr = wr_ref[...]
    w_hi = wr.astype(BF16)
    w_lo = (wr - w_hi.astype(F32)).astype(BF16)
    E = wr.shape[0]
    both = _dot_nt(jnp.concatenate([w_hi, w_lo], axis=0), y_hi)
    lg_ref[...] = both[:E] + both[E:] + _dot_nt(w_hi, y_lo)


def _wout_ln_router(a_list, w_list, x, g, b, wr_t, *, tm=512):
    S, D = x.shape
    n_in = len(a_list)
    E = wr_t.shape[0]
    in_specs = ([pl.BlockSpec((tm, a.shape[1]), lambda i: (i, 0)) for a in a_list]
                + [pl.BlockSpec(w.shape, lambda i: (0, 0)) for w in w_list]
                + [pl.BlockSpec((tm, D), lambda i: (i, 0)),
                   pl.BlockSpec((1, D), lambda i: (0, 0)),
                   pl.BlockSpec((1, D), lambda i: (0, 0)),
                   pl.BlockSpec((E, D), lambda i: (0, 0))])
    return pl.pallas_call(
        functools.partial(_wout_kernel, n_in=n_in),
        grid=(S // tm,),
        in_specs=in_specs,
        out_specs=[pl.BlockSpec((tm, D), lambda i: (i, 0)),
                   pl.BlockSpec((tm, D), lambda i: (i, 0)),
                   pl.BlockSpec((E, tm), lambda i: (0, i))],
        out_shape=[jax.ShapeDtypeStruct((S, D), F32),
                   jax.ShapeDtypeStruct((S, D), BF16),
                   jax.ShapeDtypeStruct((E, S), F32)],
        compiler_params=_cparams(("parallel",)),
        name="wout_ln_router",
    )(*a_list, *w_list, x, g, b, wr_t)


def _select_kernel(lg_ref, pos_ref, gate_ref, off_ref, cnt_ref, *, E, nb, cap):
    lg = lg_ref[...].reshape(E, nb, LANES)
    mx = jnp.max(lg, axis=0, keepdims=True)
    ex = jnp.exp(lg - mx)
    aff = ex / jnp.sum(ex, axis=0, keepdims=True)
    bits = lax.bitcast_convert_type(aff, I32)

    def count(msk):
        c = jnp.sum(msk.astype(F32), axis=1, keepdims=True)
        return jnp.sum(c, axis=2, keepdims=True)

    def search(i, thr):
        cand = thr | lax.shift_left(jnp.int32(1), 30 - i)
        return jnp.where(count(bits >= cand) >= cap, cand, thr)

    thr = lax.fori_loop(0, 31, search, jnp.zeros((E, 1, 1), I32))
    gt = bits > thr
    eq = bits == thr
    need = cap - count(gt)

    col = lax.broadcasted_iota(I32, (LANES, LANES), 1)
    rw = lax.broadcasted_iota(I32, (LANES, LANES), 0)
    upper = (rw < col).astype(BF16)
    ones = jnp.ones((LANES, LANES), BF16)
    lower = (lax.broadcasted_iota(I32, (nb, nb), 1) < lax.broadcasted_iota(I32, (nb, nb), 0)).astype(BF16)

    def prefix(msk):
        m2 = jnp.where(msk, 1.0, 0.0).reshape(E * nb, LANES).astype(BF16)
        within = _dot(m2, upper).reshape(E, nb, LANES)
        tot = _dot(m2, ones).reshape(E, nb, LANES)
        offs = jnp.stack([_dot(lower, tot[e].astype(BF16)) for e in range(E)], axis=0)
        return within + offs, offs, tot

    eq_rank, _, _ = prefix(eq)
    sel = gt | (eq & (eq_rank < need))
    pos, offs, tot = prefix(sel)
    pos_ref[...] = jnp.where(sel, pos, -1.0).astype(I32).reshape(E * nb, LANES)
    gate_ref[...] = jnp.where(sel, aff, 0.0).reshape(E * nb, LANES)
    off_ref[...] = offs.astype(I32).reshape(E * nb, LANES)
    cnt_ref[...] = tot.astype(I32).reshape(E * nb, LANES)


def _select(logits_t, cap):
    E, S = logits_t.shape
    nb = S // LANES
    shp = (E * nb, LANES)
    full = pl.BlockSpec(shp, lambda: (0, 0))
    return pl.pallas_call(
        functools.partial(_select_kernel, E=E, nb=nb, cap=cap),
        in_specs=[full],
        out_specs=[full, full, full, full],
        out_shape=[jax.ShapeDtypeStruct(shp, I32), jax.ShapeDtypeStruct(shp, F32),
                   jax.ShapeDtypeStruct(shp, I32), jax.ShapeDtypeStruct(shp, I32)],
        compiler_params=pltpu.CompilerParams(vmem_limit_bytes=VMEM_LIMIT),
        name="select",
    )(logits_t.reshape(shp))


def _gather_kernel(off_s, cnt_s, x_ref, pos_ref, gate_ref, xe_ref, gs_ref, *, cap, tbs, ep):
    grp = pl.program_id(0)
    t = pl.program_id(1)

    @pl.when(t == 0)
    def _():
        xe_ref[...] = jnp.zeros(xe_ref.shape, xe_ref.dtype)
        gs_ref[...] = jnp.zeros(gs_ref.shape, gs_ref.dtype)

    def place(k, u, e, tb, width):
        base = jnp.minimum((off_s[e, tb] // BF16_ROWS) * BF16_ROWS, cap - width)
        base = pl.multiple_of(base, BF16_ROWS)
        slot = base + lax.broadcasted_iota(I32, (width, LANES), 0)
        hit = pos_ref[k, u] == slot
        rows = _dot(hit.astype(BF16), x_ref[pl.ds(u * LANES, LANES), :])
        win = pl.ds(base, width)
        xe_ref[k, win, :] = xe_ref[k, win, :] + rows.astype(xe_ref.dtype)
        g = jnp.sum(jnp.where(hit, gate_ref[k, u], 0.0), axis=1, keepdims=True)
        gs_ref[k, win, :] = gs_ref[k, win, :] + jnp.broadcast_to(g, (width, LANES))

    for u in range(tbs):
        for k in range(ep):
            e = grp * ep + k
            tb = t * tbs + u
            cnt = cnt_s[e, tb]

            @pl.when((cnt > 0) & (cnt <= GATHER_SMALL))
            def _():
                place(k, u, e, tb, GATHER_SMALL + BF16_ROWS)

            @pl.when(cnt > GATHER_SMALL)
            def _():
                place(k, u, e, tb, GATHER_WIN)


def _gather(off_s, cnt_s, x_bf, pos, gate, cap, *, tbs=8, ep=2):
    S, D = x_bf.shape
    E, nb = off_s.shape
    assert cap >= GATHER_WIN and cap % BF16_ROWS == 0 and nb % tbs == 0 and E % ep == 0
    pos4 = pos.reshape(E, nb, 1, LANES)
    gate4 = gate.reshape(E, nb, 1, LANES)
    return pl.pallas_call(
        functools.partial(_gather_kernel, cap=cap, tbs=tbs, ep=ep),
        grid_spec=pltpu.PrefetchScalarGridSpec(
            num_scalar_prefetch=2,
            grid=(E // ep, nb // tbs),
            in_specs=[
                pl.BlockSpec((tbs * LANES, D), lambda g, t, o, c: (t, 0)),
                pl.BlockSpec((ep, tbs, 1, LANES), lambda g, t, o, c: (g, t, 0, 0)),
                pl.BlockSpec((ep, tbs, 1, LANES), lambda g, t, o, c: (g, t, 0, 0)),
            ],
            out_specs=[pl.BlockSpec((ep, cap, D), lambda g, t, o, c: (g, 0, 0)),
                       pl.BlockSpec((ep, cap, LANES), lambda g, t, o, c: (g, 0, 0))],
        ),
        out_shape=[jax.ShapeDtypeStruct((E, cap, D), BF16),
                   jax.ShapeDtypeStruct((E, cap, LANES), F32)],
        compiler_params=_cparams(("parallel", "arbitrary")),
        name="gather",
    )(off_s, cnt_s, x_bf, pos4, gate4)


def _ffn_kernel(xe_ref, wg_ref, wu_ref, wd_ref, gs_ref, ye_ref, acc_sc, *, sub):
    fc = pl.program_id(2)
    rows = xe_ref.shape[1]
    @pl.when(fc == 0)
    def _():
        acc_sc[...] = jnp.zeros(acc_sc.shape, F32)

    wg = wg_ref[...].astype(BF16)
    wu = wu_ref[...].astype(BF16)
    wd = wd_ref[...].astype(BF16)
    for r in range(rows // sub):
        sl = pl.ds(r * sub, sub)
        xt = xe_ref[0, sl, :]
        hg = _dot(xt, wg)
        hu = _dot(xt, wu)
        hid = (hg * jax.nn.sigmoid(hg) * hu).astype(BF16)
        acc_sc[sl, :] = acc_sc[sl, :] + _dot(hid, wd)

    @pl.when(fc == pl.num_programs(2) - 1)
    def _():
        ye_ref[0] = (acc_sc[...] * gs_ref[0][:, :1]).astype(ye_ref.dtype)


def _ffn(xe, gs, w_gate, w_up, w_down, layer, *, halves=2, fcw=256):
    E, cap, D = xe.shape
    F = w_gate.shape[-1]
    rows = cap // halves
    sub = min(256, rows)
    return pl.pallas_call(
        functools.partial(_ffn_kernel, sub=sub),
        grid=(E, halves, F // fcw),
        in_specs=[
            pl.BlockSpec((1, rows, D), lambda e, t, f: (e, t, 0)),
            pl.BlockSpec((None, None, D, fcw), lambda e, t, f: (layer, e, 0, f)),
            pl.BlockSpec((None, None, D, fcw), lambda e, t, f: (layer, e, 0, f)),
            pl.BlockSpec((None, None, fcw, D), lambda e, t, f: (layer, e, f, 0)),
            pl.BlockSpec((1, rows, LANES), lambda e, t, f: (e, t, 0)),
        ],
        out_specs=pl.BlockSpec((1, rows, D), lambda e, t, f: (e, t, 0)),
        out_shape=jax.ShapeDtypeStruct((E, cap, D), BF16),
        scratch_shapes=[pltpu.VMEM((rows, D), F32)],
        compiler_params=_cparams(("parallel", "parallel", "arbitrary")),
        name="expert_ffn",
    )(xe, w_gate, w_up, w_down, gs)


def _combine_kernel(win_s, *refs, eg):
    y_refs = refs[:eg]
    post_ref, x_ref, g_ref, b_ref, xo_ref, xb_ref, acc_sc = refs[eg:]
    tb = pl.program_id(0)
    grp = pl.program_id(1)

    @pl.when(grp == 0)
    def _():
        acc_sc[...] = jnp.zeros(acc_sc.shape, F32)

    pt = post_ref[...].astype(F32)
    lane = lax.broadcasted_iota(I32, pt.shape, 1)
    col = lax.broadcasted_iota(I32, (LANES, GATHER_WIN), 1)
    total = None
    for k in range(eg):
        e = grp * eg + k
        pcol = jnp.sum(jnp.where(lane == e, pt, 0.0), axis=1, keepdims=True).astype(I32)
        hit = ((pcol - win_s[e, tb]) == col).astype(BF16)
        part = _dot(hit, y_refs[k][0])
        total = part if total is None else total + part
    acc_sc[...] = acc_sc[...] + total

    @pl.when(grp == pl.num_programs(1) - 1)
    def _():
        y = _layer_norm(DEEPNORM_ALPHA * x_ref[...] + acc_sc[...], g_ref[...], b_ref[...])
        xo_ref[...] = y
        xb_ref[...] = y.astype(BF16)


def _combine_ln(win_s, ye, pos_t, x, g, b, *, eg=16):
    S, D = x.shape
    E, cap, _ = ye.shape
    nb = S // LANES
    assert E % eg == 0

    def y_spec(k):
        return pl.BlockSpec((pl.Element(1), pl.Element(GATHER_WIN), pl.Element(D)),
                            lambda t, gr, ws: (gr * eg + k, (ws[gr * eg + k, t] // BF16_ROWS) * BF16_ROWS, 0))

    row = lambda t, gr, ws: (t, 0)
    fixed = lambda t, gr, ws: (0, 0)
    return pl.pallas_call(
        functools.partial(_combine_kernel, eg=eg),
        grid_spec=pltpu.PrefetchScalarGridSpec(
            num_scalar_prefetch=1,
            grid=(nb, E // eg),
            in_specs=([y_spec(k) for k in range(eg)]
                      + [pl.BlockSpec((LANES, E), row), pl.BlockSpec((LANES, D), row),
                         pl.BlockSpec((1, D), fixed), pl.BlockSpec((1, D), fixed)]),
            out_specs=[pl.BlockSpec((LANES, D), row), pl.BlockSpec((LANES, D), row)],
            scratch_shapes=[pltpu.VMEM((LANES, D), F32)],
        ),
        out_shape=[jax.ShapeDtypeStruct((S, D), F32), jax.ShapeDtypeStruct((S, D), BF16)],
        compiler_params=_cparams(("parallel", "arbitrary")),
        name="combine_ln",
    )(win_s, *([ye] * eg), pos_t, x, g, b)


def _moe(x, x_bf, logits_t, w_gate, w_up, w_down, layer, g, b):
    S, D = x.shape
    E = logits_t.shape[0]
    nb = S // LANES
    cap = EC_FACTOR * S // E
    pos, gate, off, cnt = _select(logits_t, cap)
    off_s = off[:, 0].reshape(E, nb)
    cnt_s = cnt[:, 0].reshape(E, nb)
    xe, gs = _gather(off_s, cnt_s, x_bf, pos, gate, cap)
    ye = _ffn(xe, gs, w_gate, w_up, w_down, layer)
    pos_t = pos.reshape(E, S).T
    win_s = jnp.minimum((off_s // BF16_ROWS) * BF16_ROWS, cap - GATHER_WIN)
    return _combine_ln(win_s, ye, pos_t, x, g, b)


def _mla_in_kernel(x_ref, w_ref, cos_ref, sin_ref, qn_ref, kvn_ref, cq_ref, ckv_ref, kr_ref):
    h = _dot(x_ref[...], w_ref[...])
    cq_ref[...] = _rms(h[:, :MLA_Q_RANK], qn_ref[...]).astype(BF16)
    ckv_ref[...] = _rms(h[:, MLA_Q_RANK:MLA_Q_RANK + MLA_KV_RANK], kvn_ref[...]).astype(BF16)
    kr = h[:, MLA_Q_RANK + MLA_KV_RANK:]
    kr_ref[...] = _rope(kr, cos_ref[...], sin_ref[...], MLA_ROPE // 4).astype(BF16)


def _mla_in(x_bf, w_bf, cos, sin, qn, kvn, *, tm=1024):
    S, K = x_bf.shape
    N = w_bf.shape[1]
    row = lambda i: (i, 0)
    fixed = lambda i: (0, 0)
    return pl.pallas_call(
        _mla_in_kernel,
        grid=(S // tm,),
        in_specs=[pl.BlockSpec((tm, K), row), pl.BlockSpec((K, N), fixed),
                  pl.BlockSpec((tm, LANES), row), pl.BlockSpec((tm, LANES), row),
                  pl.BlockSpec((1, MLA_Q_RANK), fixed), pl.BlockSpec((1, MLA_KV_RANK), fixed)],
        out_specs=[pl.BlockSpec((tm, MLA_Q_RANK), row), pl.BlockSpec((tm, MLA_KV_RANK), row),
                   pl.BlockSpec((tm, LANES), row)],
        out_shape=[jax.ShapeDtypeStruct((S, MLA_Q_RANK), BF16),
                   jax.ShapeDtypeStruct((S, MLA_KV_RANK), BF16),
                   jax.ShapeDtypeStruct((S, LANES), BF16)],
        compiler_params=_cparams(("parallel",)),
        name="mla_in",
    )(x_bf, w_bf, cos, sin, qn, kvn)


def _mla_q_kernel(c_ref, w_ref, cos_ref, sin_ref, o_ref, *, heads):
    h = _dot(c_ref[...], w_ref[...])
    scale = (MLA_NOPE + MLA_ROPE) ** -0.5 * LOG2E
    for c in range(heads):
        base = c * MLA_QK_PAD
        o_ref[c, :, :MLA_NOPE] = (h[:, base:base + MLA_NOPE] * scale).astype(BF16)
        rp = _rope(h[:, base + MLA_NOPE:base + MLA_QK_PAD], cos_ref[...], sin_ref[...], MLA_ROPE // 4)
        o_ref[c, :, MLA_NOPE:] = (rp * scale).astype(BF16)


def _mla_q(cq, w_bf, cos, sin, *, tm=1024, heads=4):
    S, K = cq.shape
    tn = heads * MLA_QK_PAD
    return pl.pallas_call(
        functools.partial(_mla_q_kernel, heads=heads),
        grid=(S // tm, MLA_HEADS // heads),
        in_specs=[pl.BlockSpec((tm, K), lambda i, j: (i, 0)),
                  pl.BlockSpec((K, tn), lambda i, j: (0, j)),
                  pl.BlockSpec((tm, LANES), lambda i, j: (i, 0)),
                  pl.BlockSpec((tm, LANES), lambda i, j: (i, 0))],
        out_specs=pl.BlockSpec((heads, tm, MLA_QK_PAD), lambda i, j: (j, i, 0)),
        out_shape=jax.ShapeDtypeStruct((MLA_HEADS, S, MLA_QK_PAD), BF16),
        compiler_params=_cparams(("parallel", "parallel")),
        name="mla_q",
    )(cq, w_bf, cos, sin)


def _mla_k_kernel(c_ref, wk_ref, kr_ref, k_ref, *, heads):
    kn = _dot(c_ref[...], wk_ref[...])
    for hh in range(heads):
        k_ref[hh, :, :MLA_NOPE] = kn[:, hh * MLA_NOPE:(hh + 1) * MLA_NOPE].astype(BF16)
        k_ref[hh, :, MLA_NOPE:] = kr_ref[...]


def _mla_k(ckv, wk_bf, kr, *, tm=1024, heads=4):
    S, K = ckv.shape
    return pl.pallas_call(
        functools.partial(_mla_k_kernel, heads=heads),
        grid=(S // tm, MLA_HEADS // heads),
        in_specs=[pl.BlockSpec((tm, K), lambda i, j: (i, 0)),
                  pl.BlockSpec((K, heads * MLA_NOPE), lambda i, j: (0, j)),
                  pl.BlockSpec((tm, LANES), lambda i, j: (i, 0))],
        out_specs=pl.BlockSpec((heads, tm, MLA_QK_PAD), lambda i, j: (j, i, 0)),
        out_shape=jax.ShapeDtypeStruct((MLA_HEADS, S, MLA_QK_PAD), BF16),
        compiler_params=_cparams(("parallel", "parallel")),
        name="mla_k",
    )(ckv, wk_bf, kr)


def kernel(x, ab_w_in, ab_q_norm, ab_k_norm, ab_lambda_q1, ab_lambda_k1, ab_lambda_q2, ab_lambda_k2, ab_subln, ab_w_out, mla_w_in, mla_q_norm, mla_kv_norm, mla_w_uq, mla_w_ukv, mla_w_out, ln_mix_g, ln_mix_b, moe_w_router, moe_w_gate, moe_w_up, moe_w_down, ln_ffn_g, ln_ffn_b):
    B, S, D = x.shape
    assert B == 1 and D == D_MODEL and S % 512 == 0
    xf = x.reshape(S, D)
    x_bf = xf.astype(BF16)
    cos_a, sin_a = _rope_tables(S, HEAD_DIM // 4, HEAD_DIM)
    cos_c, sin_c = _rope_tables(S, MLA_ROPE // 4, LANES)
    row2 = lambda v: v.reshape(1, -1)

    lam_init = 0.8 - 0.6 * math.exp(-0.3 * 0)
    w_in = ab_w_in[0].astype(BF16)
    c_av = A_Q + A_KV
    c_bq = c_av + A_KV
    c_bv = c_bq + 2 * B_QK
    w_qk = jnp.concatenate([w_in[:, :c_av], w_in[:, c_bq:c_bv]], axis=1)
    hq = _proj0(x_bf, w_qk, cos_a, sin_a, row2(ab_q_norm[0]), row2(ab_k_norm[0]))
    vt_a = _values_t(w_in[:, c_av:c_bq], x_bf, name="values_t_gqa")
    vt_b = _values_t(w_in[:, c_bv:], x_bf, name="values_t_diff")
    t_ak = A_Q // HEAD_DIM
    t_bq = t_ak + GQA_KV_HEADS
    t_bk = t_bq + 2 * DIFF_HEADS
    a_out = _flash(hq, hq, vt_a, n_kv=GQA_KV_HEADS, rep=GQA_Q_HEADS // GQA_KV_HEADS,
                   q_tile0=0, k_tile0=t_ak, dv=HEAD_DIM, bq=128, kc=8192, kv_buffers=1, name="gqa_attention")
    b_out = _diff_attention(hq, vt_b, row2(ab_lambda_q1[0]), row2(ab_lambda_k1[0]),
                            row2(ab_lambda_q2[0]), row2(ab_lambda_k2[0]), row2(ab_subln[0]),
                            q_tile0=t_bq, k_tile0=t_bk, lam_init=lam_init)
    w_out = ab_w_out[0].astype(BF16)
    xf, x_bf, logits_t = _wout_ln_router(
        [a_out, b_out], [w_out[:A_Q], w_out[A_Q:]], xf, row2(ln_mix_g[0]), row2(ln_mix_b[0]),
        moe_w_router[0].T)
    xf, x_bf = _moe(xf, x_bf, logits_t, moe_w_gate, moe_w_up, moe_w_down, 0,
                    row2(ln_ffn_g[0]), row2(ln_ffn_b[0]))

    w1 = jnp.pad(mla_w_in[0], ((0, 0), (0, LANES - MLA_ROPE))).astype(BF16)
    cq, ckv, kr = _mla_in(x_bf, w1, cos_c, sin_c, row2(mla_q_norm[0]), row2(mla_kv_norm[0]))
    w_uq = mla_w_uq[0].reshape(MLA_Q_RANK, MLA_HEADS, MLA_NOPE + MLA_ROPE)
    w_uq = jnp.pad(w_uq, ((0, 0), (0, 0), (0, MLA_QK_PAD - MLA_NOPE - MLA_ROPE)))
    w_uq = w_uq.reshape(MLA_Q_RANK, MLA_HEADS * MLA_QK_PAD).astype(BF16)
    w_ukv = mla_w_ukv[0].reshape(MLA_KV_RANK, MLA_HEADS, MLA_NOPE + MLA_V)
    w_uk = w_ukv[:, :, :MLA_NOPE].reshape(MLA_KV_RANK, MLA_HEADS * MLA_NOPE).astype(BF16)
    w_uv = w_ukv[:, :, MLA_NOPE:].reshape(MLA_KV_RANK, MLA_HEADS * MLA_V)
    q_pad = _mla_q(cq, w_uq, cos_c, sin_c)
    k_pad = _mla_k(ckv, w_uk, kr)
    vt1 = _values_t(w_uv, ckv, name="values_t_mla")
    c_out = _flash(q_pad, k_pad, vt1, n_kv=MLA_HEADS, rep=1, q_tile0=0, k_tile0=0, dv=MLA_V,
                   bq=512, kc=8192, kv_buffers=1, name="mla_attention")
    xf, x_bf, logits_t = _wout_ln_router(
        [c_out], [mla_w_out[0].astype(BF16)], xf, row2(ln_mix_g[1]), row2(ln_mix_b[1]),
        moe_w_router[1].T)
    xf, x_bf = _moe(xf, x_bf, logits_t, moe_w_gate, moe_w_up, moe_w_down, 1,
                    row2(ln_ffn_g[1]), row2(ln_ffn_b[1]))
    return xf.reshape(B, S, D)
```

```python
import functools
import math

import jax
import jax.numpy as jnp
from jax import lax
from jax.experimental import pallas as pl
from jax.experimental.pallas import tpu as pltpu

F32 = jnp.float32
BF16 = jnp.bfloat16
I32 = jnp.int32

D_MODEL = 2048
DEPTH = 2
GRID_W = 64
ROPE_THETA = 10000.0
NORM_EPS = 1e-6
LN_EPS = 1e-5

HEAD_DIM = 128
GQA_Q_HEADS = 8
GQA_KV_HEADS = 2
DIFF_HEADS = 4
A_Q = GQA_Q_HEADS * HEAD_DIM
A_KV = GQA_KV_HEADS * HEAD_DIM
B_QK = DIFF_HEADS * 2 * HEAD_DIM
B_V = DIFF_HEADS * 2 * HEAD_DIM

MLA_HEADS = 16
MLA_Q_RANK = 512
MLA_KV_RANK = 512
MLA_NOPE = 128
MLA_ROPE = 64
MLA_V = 128
MLA_QK_PAD = 256

N_EXPERTS = 16
EXPERT_FF = 2048
EC_FACTOR = 2

DEEPNORM_ALPHA = (2 * DEPTH) ** 0.25
LOG2E = math.log2(math.e)

LANES = 128
BF16_ROWS = 16
GATHER_WIN = LANES + BF16_ROWS
GATHER_SMALL = 32
VMEM_LIMIT = 56 * 1024 * 1024


def _cparams(sem, vmem=VMEM_LIMIT, flags=None):
    return pltpu.CompilerParams(dimension_semantics=sem, vmem_limit_bytes=vmem, flags=flags)


def _dot(a, b):
    return jnp.dot(a, b, preferred_element_type=F32)


def _dot_nt(a, b):
    return lax.dot_general(a, b, (((1,), (1,)), ((), ())), preferred_element_type=F32)


def _rope(y, cos, sin, half):
    n = y.shape[-1]
    lane = lax.broadcasted_iota(I32, y.shape, 1)
    up = pltpu.roll(y, n - half, 1)
    dn = pltpu.roll(y, half, 1)
    partner = jnp.where((lane % (2 * half)) < half, up, dn)
    return y * cos + partner * sin


def _rms(y, w):
    return y * lax.rsqrt(jnp.mean(y * y, axis=-1, keepdims=True) + NORM_EPS) * w


def _layer_norm(y, g, b):
    mu = jnp.mean(y, axis=-1, keepdims=True)
    d = y - mu
    var = jnp.mean(d * d, axis=-1, keepdims=True)
    return d * lax.rsqrt(var + LN_EPS) * g + b


def _rope_tables(S, half, width):
    n_rows = S // GRID_W
    n = 2 * half
    freqs = ROPE_THETA ** (-jnp.arange(0, n, 2, dtype=F32) / n)[None, :]
    parts_c, parts_s = [], []
    for count, along_rows in ((n_rows, True), (GRID_W, False)):
        ang = jnp.arange(count, dtype=F32)[:, None] * freqs
        c = jnp.concatenate([jnp.cos(ang)] * 2, axis=1)
        s = jnp.concatenate([-jnp.sin(ang), jnp.sin(ang)], axis=1)
        shape = (n_rows, GRID_W, n)
        expand = (lambda a: a[:, None, :]) if along_rows else (lambda a: a[None, :, :])
        parts_c.append(jnp.broadcast_to(expand(c), shape).reshape(S, n))
        parts_s.append(jnp.broadcast_to(expand(s), shape).reshape(S, n))
    pad = width - 4 * half
    if pad:
        parts_c.append(jnp.ones((S, pad), F32))
        parts_s.append(jnp.zeros((S, pad), F32))
    return jnp.concatenate(parts_c, axis=1), jnp.concatenate(parts_s, axis=1)


def _proj0_kernel(x_ref, w_ref, cos_ref, sin_ref, qn_ref, kn_ref, o_ref, *, tn):
    j = pl.program_id(1)
    h = _dot(x_ref[...], w_ref[...])
    nsub = tn // HEAD_DIM
    scale = HEAD_DIM ** -0.5 * LOG2E
    k_tile = A_Q // tn
    bq_lo = (A_Q + A_KV) // tn
    bq_hi = bq_lo + B_QK // tn

    def normed(c, w):
        y = _rms(h[:, c * HEAD_DIM:(c + 1) * HEAD_DIM], w)
        return _rope(y, cos_ref[...], sin_ref[...], HEAD_DIM // 4)

    @pl.when(j < k_tile)
    def _():
        for c in range(nsub):
            o_ref[c] = (normed(c, qn_ref[...]) * scale).astype(o_ref.dtype)

    @pl.when(j == k_tile)
    def _():
        for c in range(nsub):
            o_ref[c] = normed(c, kn_ref[...]).astype(o_ref.dtype)

    @pl.when(j > k_tile)
    def _():
        mul = jnp.where((j >= bq_lo) & (j < bq_hi), scale, 1.0).astype(F32)
        for c in range(nsub):
            o_ref[c] = (h[:, c * HEAD_DIM:(c + 1) * HEAD_DIM] * mul).astype(o_ref.dtype)


def _proj0(x_bf, w_bf, cos, sin, qn, kn, *, tm=1024, tn=256):
    S, K = x_bf.shape
    N = w_bf.shape[1]
    assert A_Q % tn == 0 and tn == A_KV and N % tn == 0
    return pl.pallas_call(
        functools.partial(_proj0_kernel, tn=tn),
        grid=(S // tm, N // tn),
        in_specs=[
            pl.BlockSpec((tm, K), lambda i, j: (i, 0)),
            pl.BlockSpec((K, tn), lambda i, j: (0, j)),
            pl.BlockSpec((tm, HEAD_DIM), lambda i, j: (i, 0)),
            pl.BlockSpec((tm, HEAD_DIM), lambda i, j: (i, 0)),
            pl.BlockSpec((1, HEAD_DIM), lambda i, j: (0, 0)),
            pl.BlockSpec((1, HEAD_DIM), lambda i, j: (0, 0)),
        ],
        out_specs=pl.BlockSpec((tn // HEAD_DIM, tm, HEAD_DIM), lambda i, j: (j, i, 0)),
        out_shape=jax.ShapeDtypeStruct((N // HEAD_DIM, S, HEAD_DIM), BF16),
        compiler_params=_cparams(("parallel", "parallel")),
        name="proj0",
    )(x_bf, w_bf, cos, sin, qn, kn)


def _values_t_kernel(wt_ref, x_ref, o_ref):
    o_ref[...] = _dot_nt(wt_ref[...], x_ref[...]).astype(o_ref.dtype)


def _values_t(w, x_bf, *, tm=1024, tn=256, name="values_t"):
    S, K = x_bf.shape
    N = w.shape[1]
    return pl.pallas_call(
        _values_t_kernel,
        grid=(S // tm, N // tn),
        in_specs=[pl.BlockSpec((tn, K), lambda i, j: (j, 0)),
                  pl.BlockSpec((tm, K), lambda i, j: (i, 0))],
        out_specs=pl.BlockSpec((tn, tm), lambda i, j: (j, i)),
        out_shape=jax.ShapeDtypeStruct((N, S), BF16),
        compiler_params=_cparams(("parallel", "parallel")),
        name=name,
    )(w.T.astype(BF16), x_bf)


def _online_softmax_pv(s, vt, m_ref, l_ref, acc_ref, shift=None):
    m_prev = m_ref[...]
    m_chunk = jnp.max(s, axis=0, keepdims=True)
    if shift is None:
        m_new = jnp.maximum(m_prev, m_chunk)
        p = jnp.exp2(s - m_new)
    else:
        m_new = jnp.maximum(m_prev, m_chunk + shift)
        p = jnp.exp2(s - (m_new - shift))
    alpha = jnp.exp2(m_prev - m_new)
    l_ref[...] = alpha * l_ref[...] + jnp.sum(p, axis=0, keepdims=True)
    acc_ref[...] = alpha * acc_ref[...] + _dot(vt, p.astype(BF16))
    m_ref[...] = m_new


def _flash_kernel(q_ref, qnext_ref, k_ref, vt_ref, o_ref, qt_sc, s_sc, m_sc, l_sc, acc_sc, *, kc):
    R, bq, D = q_ref.shape
    S = k_ref.shape[1]
    Dv = vt_ref.shape[0]
    n = S // kc
    assert n % 2 == 0
    cur = pl.program_id(1) % 2
    nxt = 1 - cur

    def scores(c):
        wrap = c >= n
        off = pl.multiple_of(jnp.where(wrap, 0, c) * kc, kc)
        return _dot(k_ref[0, pl.ds(off, kc), :], qt_sc[jnp.where(wrap, nxt, cur)])

    def values_t(c):
        return vt_ref[:, pl.ds(pl.multiple_of(c * kc, kc), kc)]

    @pl.when(pl.program_id(1) == 0)
    def _():
        qt_sc[0] = q_ref[...].reshape(R * bq, D).T
        s_sc[0] = scores(0)

    qt_sc[nxt] = qnext_ref[...].reshape(R * bq, D).T
    m_sc[...] = jnp.full(m_sc.shape, -jnp.inf, F32)
    l_sc[...] = jnp.zeros(l_sc.shape, F32)
    acc_sc[...] = jnp.zeros(acc_sc.shape, F32)

    def body(i, carry):
        c = 2 * i
        s_sc[1] = scores(c + 1)
        _online_softmax_pv(s_sc[0], values_t(c), m_sc, l_sc, acc_sc)
        s_sc[0] = scores(c + 2)
        _online_softmax_pv(s_sc[1], values_t(c + 1), m_sc, l_sc, acc_sc)
        return carry

    lax.fori_loop(0, n // 2, body, 0)
    out = (acc_sc[...] / l_sc[...]).T
    for r in range(R):
        o_ref[:, r * Dv:(r + 1) * Dv] = out[r * bq:(r + 1) * bq].astype(o_ref.dtype)


def _flash(q_arr, k_arr, vt_arr, *, n_kv, rep, q_tile0, k_tile0, dv, bq, kc, kv_buffers, name):
    _, S, D = q_arr.shape
    kc = min(kc, S // 2)
    bq = min(bq, S)
    assert q_tile0 % rep == 0
    N = rep * bq
    rows = dv
    nq = S // bq
    return pl.pallas_call(
        functools.partial(_flash_kernel, kc=kc),
        grid=(n_kv, nq),
        in_specs=[
            pl.BlockSpec((rep, bq, D), lambda g, i: (q_tile0 // rep + g, i, 0)),
            pl.BlockSpec((rep, bq, D), lambda g, i: (q_tile0 // rep + g, jnp.minimum(i + 1, nq - 1), 0)),
            pl.BlockSpec((1, S, D), lambda g, i: (k_tile0 + g, 0, 0), pipeline_mode=pl.Buffered(kv_buffers)),
            pl.BlockSpec((rows, vt_arr.shape[1]), lambda g, i: (g, 0), pipeline_mode=pl.Buffered(kv_buffers)),
        ],
        out_specs=pl.BlockSpec((bq, rep * dv), lambda g, i: (i, g)),
        out_shape=jax.ShapeDtypeStruct((S, n_kv * rep * dv), BF16),
        scratch_shapes=[pltpu.VMEM((2, D, N), BF16),
                        pltpu.VMEM((2, kc, N), F32),
                        pltpu.VMEM((1, N), F32), pltpu.VMEM((1, N), F32), pltpu.VMEM((rows, N), F32)],
        compiler_params=_cparams(("parallel", "arbitrary")),
        name=name,
    )(q_arr, q_arr, k_arr, vt_arr)


def _diff_kernel(q_ref, qnext_ref, k_ref, vt_ref, lq1_ref, lk1_ref, lq2_ref, lk2_ref, subln_ref, o_ref,
                 qt_sc, s_sc, m_sc, l_sc, acc_sc, *, kc, lam_init):
    _, bq, D = q_ref.shape
    S = k_ref.shape[1]
    h = pl.program_id(0)
    q0 = pl.program_id(1) * bq
    slope = jnp.float32(2.0 ** (-8.0 * DIFF_HEADS / DIFF_HEADS))
    for hh in range(DIFF_HEADS - 1):
        slope = jnp.where(h == hh, jnp.float32(2.0 ** (-8.0 * (hh + 1) / DIFF_HEADS)), slope)
    slope2 = slope * LOG2E
    k_loc = lax.broadcasted_iota(I32, (kc, LANES), 0).astype(F32)
    k_s = k_loc * slope2
    q_loc = lax.broadcasted_iota(I32, (1, bq), 1).astype(F32)
    n = S // kc

    def per_lane_group(s, fn):
        return jnp.concatenate([fn(s[:, g * LANES:(g + 1) * LANES], g) for g in range(bq // LANES)], axis=1)
    cur = pl.program_id(1) % 2
    nxt = 1 - cur

    def scores(j, c):
        wrap = c >= n
        off = pl.multiple_of(jnp.where(wrap, 0, c) * kc, kc)
        return _dot(k_ref[j, pl.ds(off, kc), :], qt_sc[jnp.where(wrap, nxt, cur), j])

    @pl.when(pl.program_id(1) == 0)
    def _():
        for j in range(2):
            qt_sc[0, j] = q_ref[j].T
        s_sc[0] = scores(0, 0)

    for j in range(2):
        qt_sc[nxt, j] = qnext_ref[j].T
    m_sc[...] = jnp.full(m_sc.shape, -jnp.inf, F32)
    l_sc[...] = jnp.zeros(l_sc.shape, F32)
    acc_sc[...] = jnp.zeros(acc_sc.shape, F32)

    def chunk_step(c, penalty):
        off = pl.multiple_of(c * kc, kc)
        delta = (off - q0).astype(F32) - q_loc
        vt = vt_ref[:, pl.ds(off, kc)]
        s_sc[1] = scores(1, c)
        s0, shift = penalty(s_sc[0], delta)
        _online_softmax_pv(s0, vt, m_sc.at[0], l_sc.at[0], acc_sc.at[0], shift)
        s_sc[0] = scores(0, c + 1)
        s1, shift = penalty(s_sc[1], delta)
        _online_softmax_pv(s1, vt, m_sc.at[1], l_sc.at[1], acc_sc.at[1], shift)

    def keys_before(c, carry):
        chunk_step(c, lambda s, delta: (per_lane_group(s, lambda t, g: t + k_s), slope2 * delta))
        return carry

    def keys_overlap(c, carry):
        def general(t, g, delta):
            dist = jnp.abs(k_loc + delta[:, g * LANES:(g + 1) * LANES])
            return t - dist * slope2
        chunk_step(c, lambda s, delta: (per_lane_group(s, lambda t, g: general(t, g, delta)), None))
        return carry

    def keys_after(c, carry):
        chunk_step(c, lambda s, delta: (per_lane_group(s, lambda t, g: t - k_s), -(slope2 * delta)))
        return carry

    c_lo = q0 // kc
    c_hi = (q0 + bq + kc - 1) // kc
    lax.fori_loop(0, c_lo, keys_before, 0)
    lax.fori_loop(c_lo, c_hi, keys_overlap, 0)
    lax.fori_loop(c_hi, n, keys_after, 0)
    lam = (jnp.exp(jnp.sum(lq1_ref[...] * lk1_ref[...], axis=1, keepdims=True))
           - jnp.exp(jnp.sum(lq2_ref[...] * lk2_ref[...], axis=1, keepdims=True)) + lam_init)
    out = (acc_sc[0] / l_sc[0] - lam * (acc_sc[1] / l_sc[1])).T
    o_ref[...] = (_rms(out, subln_ref[...]) * (1.0 - lam_init)).astype(o_ref.dtype)


def _diff_attention(hq, vt_arr, lq1, lk1, lq2, lk2, subln, *, q_tile0, k_tile0, lam_init,
                    bq=512, kc=4096):
    _, S, D = hq.shape
    kc = min(kc, S)
    bq = min(bq, S)
    rows = 2 * D
    vec = pl.BlockSpec((1, D), lambda h, i: (0, 0))
    nq = S // bq
    return pl.pallas_call(
        functools.partial(_diff_kernel, kc=kc, lam_init=lam_init),
        grid=(DIFF_HEADS, nq),
        in_specs=[
            pl.BlockSpec((2, bq, D), lambda h, i: (q_tile0 // 2 + h, i, 0)),
            pl.BlockSpec((2, bq, D), lambda h, i: (q_tile0 // 2 + h, jnp.minimum(i + 1, nq - 1), 0)),
            pl.BlockSpec((2, S, D), lambda h, i: (k_tile0 // 2 + h, 0, 0), pipeline_mode=pl.Buffered(1)),
            pl.BlockSpec((rows, vt_arr.shape[1]), lambda h, i: (h, 0), pipeline_mode=pl.Buffered(1)),
            vec, vec, vec, vec,
            pl.BlockSpec((1, 2 * D), lambda h, i: (0, 0)),
        ],
        out_specs=pl.BlockSpec((bq, 2 * D), lambda h, i: (i, h)),
        out_shape=jax.ShapeDtypeStruct((S, DIFF_HEADS * 2 * D), BF16),
        scratch_shapes=[pltpu.VMEM((2, 2, D, bq), BF16), pltpu.VMEM((2, kc, bq), F32),
                        pltpu.VMEM((2, 1, bq), F32), pltpu.VMEM((2, 1, bq), F32),
                        pltpu.VMEM((2, rows, bq), F32)],
        compiler_params=_cparams(("parallel", "arbitrary")),
        name="diff_attention",
    )(hq, hq, hq, vt_arr, lq1, lk1, lq2, lk2, subln)


def _wout_kernel(*refs, n_in):
    a_refs = refs[:n_in]
    w_refs = refs[n_in:2 * n_in]
    x_ref, g_ref, b_ref, wr_ref, xo_ref, xb_ref, lg_ref = refs[2 * n_in:]
    mix = _dot(a_refs[0][...], w_refs[0][...])
    for a, w in zip(a_refs[1:], w_refs[1:]):
        mix = mix + _dot(a[...], w[...])
    y = _layer_norm(DEEPNORM_ALPHA * x_ref[...] + mix, g_ref[...], b_ref[...])
    xo_ref[...] = y
    y_hi = y.astype(BF16)
    xb_ref[...] = y_hi
    y_lo = (y - y_hi.astype(F32)).astype(BF16)
    wr = wr_ref[...]
    w_hi = wr.astype(BF16)
    w_lo = (wr - w_hi.astype(F32)).astype(BF16)
    E = wr.shape[0]
    both = _dot_nt(jnp.concatenate([w_hi, w_lo], axis=0), y_hi)
    lg_ref[...] = both[:E] + both[E:] + _dot_nt(w_hi, y_lo)


def _wout_ln_router(a_list, w_list, x, g, b, wr_t, *, tm=512):
    S, D = x.shape
    n_in = len(a_list)
    E = wr_t.shape[0]
    in_specs = ([pl.BlockSpec((tm, a.shape[1]), lambda i: (i, 0)) for a in a_list]
                + [pl.BlockSpec(w.shape, lambda i: (0, 0)) for w in w_list]
                + [pl.BlockSpec((tm, D), lambda i: (i, 0)),
                   pl.BlockSpec((1, D), lambda i: (0, 0)),
                   pl.BlockSpec((1, D), lambda i: (0, 0)),
                   pl.BlockSpec((E, D), lambda i: (0, 0))])
    return pl.pallas_call(
        functools.partial(_wout_kernel, n_in=n_in),
        grid=(S // tm,),
        in_specs=in_specs,
        out_specs=[pl.BlockSpec((tm, D), lambda i: (i, 0)),
                   pl.BlockSpec((tm, D), lambda i: (i, 0)),
                   pl.BlockSpec((E, tm), lambda i: (0, i))],
        out_shape=[jax.ShapeDtypeStruct((S, D), F32),
                   jax.ShapeDtypeStruct((S, D), BF16),
                   jax.ShapeDtypeStruct((E, S), F32)],
        compiler_params=_cparams(("parallel",)),
        name="wout_ln_router",
    )(*a_list, *w_list, x, g, b, wr_t)


def _select_kernel(lg_ref, pos_ref, gate_ref, off_ref, cnt_ref, *, E, nb, cap):
    lg = lg_ref[...].reshape(E, nb, LANES)
    mx = jnp.max(lg, axis=0, keepdims=True)
    ex = jnp.exp(lg - mx)
    aff = ex / jnp.sum(ex, axis=0, keepdims=True)
    bits = lax.bitcast_convert_type(aff, I32)

    def count(msk):
        c = jnp.sum(msk.astype(F32), axis=1, keepdims=True)
        return jnp.sum(c, axis=2, keepdims=True)

    def search(i, thr):
        cand = thr | lax.shift_left(jnp.int32(1), 30 - i)
        return jnp.where(count(bits >= cand) >= cap, cand, thr)

    thr = lax.fori_loop(0, 31, search, jnp.zeros((E, 1, 1), I32))
    gt = bits > thr
    eq = bits == thr
    need = cap - count(gt)

    col = lax.broadcasted_iota(I32, (LANES, LANES), 1)
    rw = lax.broadcasted_iota(I32, (LANES, LANES), 0)
    upper = (rw < col).astype(BF16)
    ones = jnp.ones((LANES, LANES), BF16)
    lower = (lax.broadcasted_iota(I32, (nb, nb), 1) < lax.broadcasted_iota(I32, (nb, nb), 0)).astype(BF16)

    def prefix(msk):
        m2 = jnp.where(msk, 1.0, 0.0).reshape(E * nb, LANES).astype(BF16)
        within = _dot(m2, upper).reshape(E, nb, LANES)
        tot = _dot(m2, ones).reshape(E, nb, LANES)
        offs = jnp.stack([_dot(lower, tot[e].astype(BF16)) for e in range(E)], axis=0)
        return within + offs, offs, tot

    eq_rank, _, _ = prefix(eq)
    sel = gt | (eq & (eq_rank < need))
    pos, offs, tot = prefix(sel)
    pos_ref[...] = jnp.where(sel, pos, -1.0).astype(I32).reshape(E * nb, LANES)
    gate_ref[...] = jnp.where(sel, aff, 0.0).reshape(E * nb, LANES)
    off_ref[...] = offs.astype(I32).reshape(E * nb, LANES)
    cnt_ref[...] = tot.astype(I32).reshape(E * nb, LANES)


def _select(logits_t, cap):
    E, S = logits_t.shape
    nb = S // LANES
    shp = (E * nb, LANES)
    full = pl.BlockSpec(shp, lambda: (0, 0))
    return pl.pallas_call(
        functools.partial(_select_kernel, E=E, nb=nb, cap=cap),
        in_specs=[full],
        out_specs=[full, full, full, full],
        out_shape=[jax.ShapeDtypeStruct(shp, I32), jax.ShapeDtypeStruct(shp, F32),
                   jax.ShapeDtypeStruct(shp, I32), jax.ShapeDtypeStruct(shp, I32)],
        compiler_params=pltpu.CompilerParams(vmem_limit_bytes=VMEM_LIMIT),
        name="select",
    )(logits_t.reshape(shp))


def _gather_kernel(off_s, cnt_s, x_ref, pos_ref, gate_ref, xe_ref, gs_ref, *, cap, tbs, ep):
    grp = pl.program_id(0)
    t = pl.program_id(1)

    @pl.when(t == 0)
    def _():
        xe_ref[...] = jnp.zeros(xe_ref.shape, xe_ref.dtype)
        gs_ref[...] = jnp.zeros(gs_ref.shape, gs_ref.dtype)

    def place(k, u, e, tb, width):
        base = jnp.minimum((off_s[e, tb] // BF16_ROWS) * BF16_ROWS, cap - width)
        base = pl.multiple_of(base, BF16_ROWS)
        slot = base + lax.broadcasted_iota(I32, (width, LANES), 0)
        hit = pos_ref[k, u] == slot
        rows = _dot(hit.astype(BF16), x_ref[pl.ds(u * LANES, LANES), :])
        win = pl.ds(base, width)
        xe_ref[k, win, :] = xe_ref[k, win, :] + rows.astype(xe_ref.dtype)
        g = jnp.sum(jnp.where(hit, gate_ref[k, u], 0.0), axis=1, keepdims=True)
        gs_ref[k, win, :] = gs_ref[k, win, :] + jnp.broadcast_to(g, (width, LANES))

    for u in range(tbs):
        for k in range(ep):
            e = grp * ep + k
            tb = t * tbs + u
            cnt = cnt_s[e, tb]

            @pl.when((cnt > 0) & (cnt <= GATHER_SMALL))
            def _():
                place(k, u, e, tb, GATHER_SMALL + BF16_ROWS)

            @pl.when(cnt > GATHER_SMALL)
            def _():
                place(k, u, e, tb, GATHER_WIN)


def _gather(off_s, cnt_s, x_bf, pos, gate, cap, *, tbs=8, ep=2):
    S, D = x_bf.shape
    E, nb = off_s.shape
    assert cap >= GATHER_WIN and cap % BF16_ROWS == 0 and nb % tbs == 0 and E % ep == 0
    pos4 = pos.reshape(E, nb, 1, LANES)
    gate4 = gate.reshape(E, nb, 1, LANES)
    return pl.pallas_call(
        functools.partial(_gather_kernel, cap=cap, tbs=tbs, ep=ep),
        grid_spec=pltpu.PrefetchScalarGridSpec(
            num_scalar_prefetch=2,
            grid=(E // ep, nb // tbs),
            in_specs=[
                pl.BlockSpec((tbs * LANES, D), lambda g, t, o, c: (t, 0)),
                pl.BlockSpec((ep, tbs, 1, LANES), lambda g, t, o, c: (g, t, 0, 0)),
                pl.BlockSpec((ep, tbs, 1, LANES), lambda g, t, o, c: (g, t, 0, 0)),
            ],
            out_specs=[pl.BlockSpec((ep, cap, D), lambda g, t, o, c: (g, 0, 0)),
                       pl.BlockSpec((ep, cap, LANES), lambda g, t, o, c: (g, 0, 0))],
        ),
        out_shape=[jax.ShapeDtypeStruct((E, cap, D), BF16),
                   jax.ShapeDtypeStruct((E, cap, LANES), F32)],
        compiler_params=_cparams(("parallel", "arbitrary")),
        name="gather",
    )(off_s, cnt_s, x_bf, pos4, gate4)


def _ffn_kernel(xe_ref, wg_ref, wu_ref, wd_ref, gs_ref, ye_ref, acc_sc, *, sub):
    fc = pl.program_id(2)
    rows = xe_ref.shape[1]
    @pl.when(fc == 0)
    def _():
        acc_sc[...] = jnp.zeros(acc_sc.shape, F32)

    wg = wg_ref[...].astype(BF16)
    wu = wu_ref[...].astype(BF16)
    wd = wd_ref[...].astype(BF16)
    for r in range(rows // sub):
        sl = pl.ds(r * sub, sub)
        xt = xe_ref[0, sl, :]
        hg = _dot(xt, wg)
        hu = _dot(xt, wu)
        hid = (hg * jax.nn.sigmoid(hg) * hu).astype(BF16)
        acc_sc[sl, :] = acc_sc[sl, :] + _dot(hid, wd)

    @pl.when(fc == pl.num_programs(2) - 1)
    def _():
        ye_ref[0] = (acc_sc[...] * gs_ref[0][:, :1]).astype(ye_ref.dtype)


def _ffn(xe, gs, w_gate, w_up, w_down, layer, *, halves=2, fcw=256):
    E, cap, D = xe.shape
    F = w_gate.shape[-1]
    rows = cap // halves
    sub = min(256, rows)
    return pl.pallas_call(
        functools.partial(_ffn_kernel, sub=sub),
        grid=(E, halves, F // fcw),
        in_specs=[
            pl.BlockSpec((1, rows, D), lambda e, t, f: (e, t, 0)),
            pl.BlockSpec((None, None, D, fcw), lambda e, t, f: (layer, e, 0, f)),
            pl.BlockSpec((None, None, D, fcw), lambda e, t, f: (layer, e, 0, f)),
            pl.BlockSpec((None, None, fcw, D), lambda e, t, f: (layer, e, f, 0)),
            pl.BlockSpec((1, rows, LANES), lambda e, t, f: (e, t, 0)),
        ],
        out_specs=pl.BlockSpec((1, rows, D), lambda e, t, f: (e, t, 0)),
        out_shape=jax.ShapeDtypeStruct((E, cap, D), BF16),
        scratch_shapes=[pltpu.VMEM((rows, D), F32)],
        compiler_params=_cparams(("parallel", "parallel", "arbitrary")),
        name="expert_ffn",
    )(xe, w_gate, w_up, w_down, gs)


def _combine_kernel(win_s, *refs, eg):
    y_refs = refs[:eg]
    post_ref, x_ref, g_ref, b_ref, xo_ref, xb_ref, acc_sc = refs[eg:]
    tb = pl.program_id(0)
    grp = pl.program_id(1)

    @pl.when(grp == 0)
    def _():
        acc_sc[...] = jnp.zeros(acc_sc.shape, F32)

    pt = post_ref[...].astype(F32)
    lane = lax.broadcasted_iota(I32, pt.shape, 1)
    col = lax.broadcasted_iota(I32, (LANES, GATHER_WIN), 1)
    total = None
    for k in range(eg):
        e = grp * eg + k
        pcol = jnp.sum(jnp.where(lane == e, pt, 0.0), axis=1, keepdims=True).astype(I32)
        hit = ((pcol - win_s[e, tb]) == col).astype(BF16)
        part = _dot(hit, y_refs[k][0])
        total = part if total is None else total + part
    acc_sc[...] = acc_sc[...] + total

    @pl.when(grp == pl.num_programs(1) - 1)
    def _():
        y = _layer_norm(DEEPNORM_ALPHA * x_ref[...] + acc_sc[...], g_ref[...], b_ref[...])
        xo_ref[...] = y
        xb_ref[...] = y.astype(BF16)


def _combine_ln(win_s, ye, pos_t, x, g, b, *, eg=16):
    S, D = x.shape
    E, cap, _ = ye.shape
    nb = S // LANES
    assert E % eg == 0

    def y_spec(k):
        return pl.BlockSpec((pl.Element(1), pl.Element(GATHER_WIN), pl.Element(D)),
                            lambda t, gr, ws: (gr * eg + k, (ws[gr * eg + k, t] // BF16_ROWS) * BF16_ROWS, 0))

    row = lambda t, gr, ws: (t, 0)
    fixed = lambda t, gr, ws: (0, 0)
    return pl.pallas_call(
        functools.partial(_combine_kernel, eg=eg),
        grid_spec=pltpu.PrefetchScalarGridSpec(
            num_scalar_prefetch=1,
            grid=(nb, E // eg),
            in_specs=([y_spec(k) for k in range(eg)]
                      + [pl.BlockSpec((LANES, E), row), pl.BlockSpec((LANES, D), row),
                         pl.BlockSpec((1, D), fixed), pl.BlockSpec((1, D), fixed)]),
            out_specs=[pl.BlockSpec((LANES, D), row), pl.BlockSpec((LANES, D), row)],
            scratch_shapes=[pltpu.VMEM((LANES, D), F32)],
        ),
        out_shape=[jax.ShapeDtypeStruct((S, D), F32), jax.ShapeDtypeStruct((S, D), BF16)],
        compiler_params=_cparams(("parallel", "arbitrary")),
        name="combine_ln",
    )(win_s, *([ye] * eg), pos_t, x, g, b)


def _moe(x, x_bf, logits_t, w_gate, w_up, w_down, layer, g, b):
    S, D = x.shape
    E = logits_t.shape[0]
    nb = S // LANES
    cap = EC_FACTOR * S // E
    pos, gate, off, cnt = _select(logits_t, cap)
    off_s = off[:, 0].reshape(E, nb)
    cnt_s = cnt[:, 0].reshape(E, nb)
    xe, gs = _gather(off_s, cnt_s, x_bf, pos, gate, cap)
    ye = _ffn(xe, gs, w_gate, w_up, w_down, layer)
    pos_t = pos.reshape(E, S).T
    win_s = jnp.minimum((off_s // BF16_ROWS) * BF16_ROWS, cap - GATHER_WIN)
    return _combine_ln(win_s, ye, pos_t, x, g, b)


def _mla_in_kernel(x_ref, w_ref, cos_ref, sin_ref, qn_ref, kvn_ref, cq_ref, ckv_ref, kr_ref):
    h = _dot(x_ref[...], w_ref[...])
    cq_ref[...] = _rms(h[:, :MLA_Q_RANK], qn_ref[...]).astype(BF16)
    ckv_ref[...] = _rms(h[:, MLA_Q_RANK:MLA_Q_RANK + MLA_KV_RANK], kvn_ref[...]).astype(BF16)
    kr = h[:, MLA_Q_RANK + MLA_KV_RANK:]
    kr_ref[...] = _rope(kr, cos_ref[...], sin_ref[...], MLA_ROPE // 4).astype(BF16)


def _mla_in(x_bf, w_bf, cos, sin, qn, kvn, *, tm=1024):
    S, K = x_bf.shape
    N = w_bf.shape[1]
    row = lambda i: (i, 0)
    fixed = lambda i: (0, 0)
    return pl.pallas_call(
        _mla_in_kernel,
        grid=(S // tm,),
        in_specs=[pl.BlockSpec((tm, K), row), pl.BlockSpec((K, N), fixed),
                  pl.BlockSpec((tm, LANES), row), pl.BlockSpec((tm, LANES), row),
                  pl.BlockSpec((1, MLA_Q_RANK), fixed), pl.BlockSpec((1, MLA_KV_RANK), fixed)],
        out_specs=[pl.BlockSpec((tm, MLA_Q_RANK), row), pl.BlockSpec((tm, MLA_KV_RANK), row),
                   pl.BlockSpec((tm, LANES), row)],
        out_shape=[jax.ShapeDtypeStruct((S, MLA_Q_RANK), BF16),
                   jax.ShapeDtypeStruct((S, MLA_KV_RANK), BF16),
                   jax.ShapeDtypeStruct((S, LANES), BF16)],
        compiler_params=_cparams(("parallel",)),
        name="mla_in",
    )(x_bf, w_bf, cos, sin, qn, kvn)


def _mla_q_kernel(c_ref, w_ref, cos_ref, sin_ref, o_ref, *, heads):
    h = _dot(c_ref[...], w_ref[...])
    scale = (MLA_NOPE + MLA_ROPE) ** -0.5 * LOG2E
    for c in range(heads):
        base = c * MLA_QK_PAD
        o_ref[c, :, :MLA_NOPE] = (h[:, base:base + MLA_NOPE] * scale).astype(BF16)
        rp = _rope(h[:, base + MLA_NOPE:base + MLA_QK_PAD], cos_ref[...], sin_ref[...], MLA_ROPE // 4)
        o_ref[c, :, MLA_NOPE:] = (rp * scale).astype(BF16)


def _mla_q(cq, w_bf, cos, sin, *, tm=1024, heads=4):
    S, K = cq.shape
    tn = heads * MLA_QK_PAD
    return pl.pallas_call(
        functools.partial(_mla_q_kernel, heads=heads),
        grid=(S // tm, MLA_HEADS // heads),
        in_specs=[pl.BlockSpec((tm, K), lambda i, j: (i, 0)),
                  pl.BlockSpec((K, tn), lambda i, j: (0, j)),
                  pl.BlockSpec((tm, LANES), lambda i, j: (i, 0)),
                  pl.BlockSpec((tm, LANES), lambda i, j: (i, 0))],
        out_specs=pl.BlockSpec((heads, tm, MLA_QK_PAD), lambda i, j: (j, i, 0)),
        out_shape=jax.ShapeDtypeStruct((MLA_HEADS, S, MLA_QK_PAD), BF16),
        compiler_params=_cparams(("parallel", "parallel")),
        name="mla_q",
    )(cq, w_bf, cos, sin)


def _mla_k_kernel(c_ref, wk_ref, kr_ref, k_ref, *, heads):
    kn = _dot(c_ref[...], wk_ref[...])
    for hh in range(heads):
        k_ref[hh, :, :MLA_NOPE] = kn[:, hh * MLA_NOPE:(hh + 1) * MLA_NOPE].astype(BF16)
        k_ref[hh, :, MLA_NOPE:] = kr_ref[...]


def _mla_k(ckv, wk_bf, kr, *, tm=1024, heads=4):
    S, K = ckv.shape
    return pl.pallas_call(
        functools.partial(_mla_k_kernel, heads=heads),
        grid=(S // tm, MLA_HEADS // heads),
        in_specs=[pl.BlockSpec((tm, K), lambda i, j: (i, 0)),
                  pl.BlockSpec((K, heads * MLA_NOPE), lambda i, j: (0, j)),
                  pl.BlockSpec((tm, LANES), lambda i, j: (i, 0))],
        out_specs=pl.BlockSpec((heads, tm, MLA_QK_PAD), lambda i, j: (j, i, 0)),
        out_shape=jax.ShapeDtypeStruct((MLA_HEADS, S, MLA_QK_PAD), BF16),
        compiler_params=_cparams(("parallel", "parallel")),
        name="mla_k",
    )(ckv, wk_bf, kr)


def kernel(x, ab_w_in, ab_q_norm, ab_k_norm, ab_lambda_q1, ab_lambda_k1, ab_lambda_q2, ab_lambda_k2, ab_subln, ab_w_out, mla_w_in, mla_q_norm, mla_kv_norm, mla_w_uq, mla_w_ukv, mla_w_out, ln_mix_g, ln_mix_b, moe_w_router, moe_w_gate, moe_w_up, moe_w_down, ln_ffn_g, ln_ffn_b):
    B, S, D = x.shape
    assert B == 1 and D == D_MODEL and S % 512 == 0
    xf = x.reshape(S, D)
    x_bf = xf.astype(BF16)
    cos_a, sin_a = _rope_tables(S, HEAD_DIM // 4, HEAD_DIM)
    cos_c, sin_c = _rope_tables(S, MLA_ROPE // 4, LANES)
    row2 = lambda v: v.reshape(1, -1)

    lam_init = 0.8 - 0.6 * math.exp(-0.3 * 0)
    w_in = ab_w_in[0].astype(BF16)
    c_av = A_Q + A_KV
    c_bq = c_av + A_KV
    c_bv = c_bq + 2 * B_QK
    w_qk = jnp.concatenate([w_in[:, :c_av], w_in[:, c_bq:c_bv]], axis=1)
    hq = _proj0(x_bf, w_qk, cos_a, sin_a, row2(ab_q_norm[0]), row2(ab_k_norm[0]))
    vt_a = _values_t(w_in[:, c_av:c_bq], x_bf, name="values_t_gqa")
    vt_b = _values_t(w_in[:, c_bv:], x_bf, name="values_t_diff")
    t_ak = A_Q // HEAD_DIM
    t_bq = t_ak + GQA_KV_HEADS
    t_bk = t_bq + 2 * DIFF_HEADS
    a_out = _flash(hq, hq, vt_a, n_kv=GQA_KV_HEADS, rep=GQA_Q_HEADS // GQA_KV_HEADS,
                   q_tile0=0, k_tile0=t_ak, dv=HEAD_DIM, bq=128, kc=8192, kv_buffers=1, name="gqa_attention")
    b_out = _diff_attention(hq, vt_b, row2(ab_lambda_q1[0]), row2(ab_lambda_k1[0]),
                            row2(ab_lambda_q2[0]), row2(ab_lambda_k2[0]), row2(ab_subln[0]),
                            q_tile0=t_bq, k_tile0=t_bk, lam_init=lam_init)
    w_out = ab_w_out[0].astype(BF16)
    xf, x_bf, logits_t = _wout_ln_router(
        [a_out, b_out], [w_out[:A_Q], w_out[A_Q:]], xf, row2(ln_mix_g[0]), row2(ln_mix_b[0]),
        moe_w_router[0].T)
    xf, x_bf = _moe(xf, x_bf, logits_t, moe_w_gate, moe_w_up, moe_w_down, 0,
                    row2(ln_ffn_g[0]), row2(ln_ffn_b[0]))

    w1 = jnp.pad(mla_w_in[0], ((0, 0), (0, LANES - MLA_ROPE))).astype(BF16)
    cq, ckv, kr = _mla_in(x_bf, w1, cos_c, sin_c, row2(mla_q_norm[0]), row2(mla_kv_norm[0]))
    w_uq = mla_w_uq[0].reshape(MLA_Q_RANK, MLA_HEADS, MLA_NOPE + MLA_ROPE)
    w_uq = jnp.pad(w_uq, ((0, 0), (0, 0), (0, MLA_QK_PAD - MLA_NOPE - MLA_ROPE)))
    w_uq = w_uq.reshape(MLA_Q_RANK, MLA_HEADS * MLA_QK_PAD).astype(BF16)
    w_ukv = mla_w_ukv[0].reshape(MLA_KV_RANK, MLA_HEADS, MLA_NOPE + MLA_V)
    w_uk = w_ukv[:, :, :MLA_NOPE].reshape(MLA_KV_RANK, MLA_HEADS * MLA_NOPE).astype(BF16)
    w_uv = w_ukv[:, :, MLA_NOPE:].reshape(MLA_KV_RANK, MLA_HEADS * MLA_V)
    q_pad = _mla_q(cq, w_uq, cos_c, sin_c)
    k_pad = _mla_k(ckv, w_uk, kr)
    vt1 = _values_t(w_uv, ckv, name="values_t_mla")
    c_out = _flash(q_pad, k_pad, vt1, n_kv=MLA_HEADS, rep=1, q_tile0=0, k_tile0=0, dv=MLA_V,
                   bq=512, kc=8192, kv_buffers=1, name="mla_attention")
    xf, x_bf, logits_t = _wout_ln_router(
        [c_out], [mla_w_out[0].astype(BF16)], xf, row2(ln_mix_g[1]), row2(ln_mix_b[1]),
        moe_w_router[1].T)
    xf, x_bf = _moe(xf, x_bf, logits_t, moe_w_gate, moe_w_up, moe_w_down, 1,
                    row2(ln_ffn_g[1]), row2(ln_ffn_b[1]))
    return xf.reshape(B, S, D)
```

```python
import functools
import math

import jax
import jax.numpy as jnp
from jax import lax
from jax.experimental import pallas as pl
from jax.experimental.pallas import tpu as pltpu

F32 = jnp.float32
BF16 = jnp.bfloat16
I32 = jnp.int32

D_MODEL = 2048
DEPTH = 2
GRID_W = 64
ROPE_THETA = 10000.0
NORM_EPS = 1e-6
LN_EPS = 1e-5

HEAD_DIM = 128
GQA_Q_HEADS = 8
GQA_KV_HEADS = 2
DIFF_HEADS = 4
A_Q = GQA_Q_HEADS * HEAD_DIM
A_KV = GQA_KV_HEADS * HEAD_DIM
B_QK = DIFF_HEADS * 2 * HEAD_DIM
B_V = DIFF_HEADS * 2 * HEAD_DIM

MLA_HEADS = 16
MLA_Q_RANK = 512
MLA_KV_RANK = 512
MLA_NOPE = 128
MLA_ROPE = 64
MLA_V = 128
MLA_QK_PAD = 256

N_EXPERTS = 16
EXPERT_FF = 2048
EC_FACTOR = 2

DEEPNORM_ALPHA = (2 * DEPTH) ** 0.25
LOG2E = math.log2(math.e)

LANES = 128
BF16_ROWS = 16
GATHER_WIN = LANES + BF16_ROWS
GATHER_SMALL = 32
VMEM_LIMIT = 56 * 1024 * 1024


def _cparams(sem, vmem=VMEM_LIMIT, flags=None):
    return pltpu.CompilerParams(dimension_semantics=sem, vmem_limit_bytes=vmem, flags=flags)


def _dot(a, b):
    return jnp.dot(a, b, preferred_element_type=F32)


def _dot_nt(a, b):
    return lax.dot_general(a, b, (((1,), (1,)), ((), ())), preferred_element_type=F32)


def _rope(y, cos, sin, half):
    n = y.shape[-1]
    lane = lax.broadcasted_iota(I32, y.shape, 1)
    up = pltpu.roll(y, n - half, 1)
    dn = pltpu.roll(y, half, 1)
    partner = jnp.where((lane % (2 * half)) < half, up, dn)
    return y * cos + partner * sin


def _rms(y, w):
    return y * lax.rsqrt(jnp.mean(y * y, axis=-1, keepdims=True) + NORM_EPS) * w


def _layer_norm(y, g, b):
    mu = jnp.mean(y, axis=-1, keepdims=True)
    d = y - mu
    var = jnp.mean(d * d, axis=-1, keepdims=True)
    return d * lax.rsqrt(var + LN_EPS) * g + b


def _rope_tables(S, half, width):
    n_rows = S // GRID_W
    n = 2 * half
    freqs = ROPE_THETA ** (-jnp.arange(0, n, 2, dtype=F32) / n)[None, :]
    parts_c, parts_s = [], []
    for count, along_rows in ((n_rows, True), (GRID_W, False)):
        ang = jnp.arange(count, dtype=F32)[:, None] * freqs
        c = jnp.concatenate([jnp.cos(ang)] * 2, axis=1)
        s = jnp.concatenate([-jnp.sin(ang), jnp.sin(ang)], axis=1)
        shape = (n_rows, GRID_W, n)
        expand = (lambda a: a[:, None, :]) if along_rows else (lambda a: a[None, :, :])
        parts_c.append(jnp.broadcast_to(expand(c), shape).reshape(S, n))
        parts_s.append(jnp.broadcast_to(expand(s), shape).reshape(S, n))
    pad = width - 4 * half
    if pad:
        parts_c.append(jnp.ones((S, pad), F32))
        parts_s.append(jnp.zeros((S, pad), F32))
    return jnp.concatenate(parts_c, axis=1), jnp.concatenate(parts_s, axis=1)


def _proj0_kernel(x_ref, w_ref, cos_ref, sin_ref, qn_ref, kn_ref, o_ref, *, tn):
    j = pl.program_id(1)
    h = _dot(x_ref[...], w_ref[...])
    nsub = tn // HEAD_DIM
    scale = HEAD_DIM ** -0.5 * LOG2E
    k_tile = A_Q // tn
    bq_lo = (A_Q + A_KV) // tn
    bq_hi = bq_lo + B_QK // tn

    def normed(c, w):
        y = _rms(h[:, c * HEAD_DIM:(c + 1) * HEAD_DIM], w)
        return _rope(y, cos_ref[...], sin_ref[...], HEAD_DIM // 4)

    @pl.when(j < k_tile)
    def _():
        for c in range(nsub):
            o_ref[c] = (normed(c, qn_ref[...]) * scale).astype(o_ref.dtype)

    @pl.when(j == k_tile)
    def _():
        for c in range(nsub):
            o_ref[c] = normed(c, kn_ref[...]).astype(o_ref.dtype)

    @pl.when(j > k_tile)
    def _():
        mul = jnp.where((j >= bq_lo) & (j < bq_hi), scale, 1.0).astype(F32)
        for c in range(nsub):
            o_ref[c] = (h[:, c * HEAD_DIM:(c + 1) * HEAD_DIM] * mul).astype(o_ref.dtype)


def _proj0(x_bf, w_bf, cos, sin, qn, kn, *, tm=1024, tn=256):
    S, K = x_bf.shape
    N = w_bf.shape[1]
    assert A_Q % tn == 0 and tn == A_KV and N % tn == 0
    return pl.pallas_call(
        functools.partial(_proj0_kernel, tn=tn),
        grid=(S // tm, N // tn),
        in_specs=[
            pl.BlockSpec((tm, K), lambda i, j: (i, 0)),
            pl.BlockSpec((K, tn), lambda i, j: (0, j)),
            pl.BlockSpec((tm, HEAD_DIM), lambda i, j: (i, 0)),
            pl.BlockSpec((tm, HEAD_DIM), lambda i, j: (i, 0)),
            pl.BlockSpec((1, HEAD_DIM), lambda i, j: (0, 0)),
            pl.BlockSpec((1, HEAD_DIM), lambda i, j: (0, 0)),
        ],
        out_specs=pl.BlockSpec((tn // HEAD_DIM, tm, HEAD_DIM), lambda i, j: (j, i, 0)),
        out_shape=jax.ShapeDtypeStruct((N // HEAD_DIM, S, HEAD_DIM), BF16),
        compiler_params=_cparams(("parallel", "parallel")),
        name="proj0",
    )(x_bf, w_bf, cos, sin, qn, kn)


def _values_t_kernel(wt_ref, x_ref, o_ref):
    o_ref[...] = _dot_nt(wt_ref[...], x_ref[...]).astype(o_ref.dtype)


def _values_t(w, x_bf, *, tm=1024, tn=256, name="values_t"):
    S, K = x_bf.shape
    N = w.shape[1]
    return pl.pallas_call(
        _values_t_kernel,
        grid=(S // tm, N // tn),
        in_specs=[pl.BlockSpec((tn, K), lambda i, j: (j, 0)),
                  pl.BlockSpec((tm, K), lambda i, j: (i, 0))],
        out_specs=pl.BlockSpec((tn, tm), lambda i, j: (j, i)),
        out_shape=jax.ShapeDtypeStruct((N, S), BF16),
        compiler_params=_cparams(("parallel", "parallel")),
        name=name,
    )(w.T.astype(BF16), x_bf)


def _online_softmax_pv(s, vt, m_ref, l_ref, acc_ref, shift=None):
    m_prev = m_ref[...]
    m_chunk = jnp.max(s, axis=0, keepdims=True)
    if shift is None:
        m_new = jnp.maximum(m_prev, m_chunk)
        p = jnp.exp2(s - m_new)
    else:
        m_new = jnp.maximum(m_prev, m_chunk + shift)
        p = jnp.exp2(s - (m_new - shift))
    alpha = jnp.exp2(m_prev - m_new)
    l_ref[...] = alpha * l_ref[...] + jnp.sum(p, axis=0, keepdims=True)
    acc_ref[...] = alpha * acc_ref[...] + _dot(vt, p.astype(BF16))
    m_ref[...] = m_new


def _flash_kernel(q_ref, qnext_ref, k_ref, vt_ref, o_ref, qt_sc, s_sc, m_sc, l_sc, acc_sc, *, kc):
    R, bq, D = q_ref.shape
    S = k_ref.shape[1]
    Dv = vt_ref.shape[0]
    n = S // kc
    assert n % 2 == 0
    cur = pl.program_id(1) % 2
    nxt = 1 - cur

    def scores(c):
        wrap = c >= n
        off = pl.multiple_of(jnp.where(wrap, 0, c) * kc, kc)
        return _dot(k_ref[0, pl.ds(off, kc), :], qt_sc[jnp.where(wrap, nxt, cur)])

    def values_t(c):
        return vt_ref[:, pl.ds(pl.multiple_of(c * kc, kc), kc)]

    @pl.when(pl.program_id(1) == 0)
    def _():
        qt_sc[0] = q_ref[...].reshape(R * bq, D).T
        s_sc[0] = scores(0)

    qt_sc[nxt] = qnext_ref[...].reshape(R * bq, D).T
    m_sc[...] = jnp.full(m_sc.shape, -jnp.inf, F32)
    l_sc[...] = jnp.zeros(l_sc.shape, F32)
    acc_sc[...] = jnp.zeros(acc_sc.shape, F32)

    def body(i, carry):
        c = 2 * i
        s_sc[1] = scores(c + 1)
        _online_softmax_pv(s_sc[0], values_t(c), m_sc, l_sc, acc_sc)
        s_sc[0] = scores(c + 2)
        _online_softmax_pv(s_sc[1], values_t(c + 1), m_sc, l_sc, acc_sc)
        return carry

    lax.fori_loop(0, n // 2, body, 0)
    out = (acc_sc[...] / l_sc[...]).T
    for r in range(R):
        o_ref[:, r * Dv:(r + 1) * Dv] = out[r * bq:(r + 1) * bq].astype(o_ref.dtype)


def _flash(q_arr, k_arr, vt_arr, *, n_kv, rep, q_tile0, k_tile0, dv, bq, kc, kv_buffers, name):
    _, S, D = q_arr.shape
    kc = min(kc, S // 2)
    bq = min(bq, S)
    assert q_tile0 % rep == 0
    N = rep * bq
    rows = dv
    nq = S // bq
    return pl.pallas_call(
        functools.partial(_flash_kernel, kc=kc),
        grid=(n_kv, nq),
        in_specs=[
            pl.BlockSpec((rep, bq, D), lambda g, i: (q_tile0 // rep + g, i, 0)),
            pl.BlockSpec((rep, bq, D), lambda g, i: (q_tile0 // rep + g, jnp.minimum(i + 1, nq - 1), 0)),
            pl.BlockSpec((1, S, D), lambda g, i: (k_tile0 + g, 0, 0), pipeline_mode=pl.Buffered(kv_buffers)),
            pl.BlockSpec((rows, vt_arr.shape[1]), lambda g, i: (g, 0), pipeline_mode=pl.Buffered(kv_buffers)),
        ],
        out_specs=pl.BlockSpec((bq, rep * dv), lambda g, i: (i, g)),
        out_shape=jax.ShapeDtypeStruct((S, n_kv * rep * dv), BF16),
        scratch_shapes=[pltpu.VMEM((2, D, N), BF16),
                        pltpu.VMEM((2, kc, N), F32),
                        pltpu.VMEM((1, N), F32), pltpu.VMEM((1, N), F32), pltpu.VMEM((rows, N), F32)],
        compiler_params=_cparams(("parallel", "arbitrary")),
        name=name,
    )(q_arr, q_arr, k_arr, vt_arr)


def _diff_kernel(q_ref, qnext_ref, k_ref, vt_ref, lq1_ref, lk1_ref, lq2_ref, lk2_ref, subln_ref, o_ref,
                 qt_sc, s_sc, m_sc, l_sc, acc_sc, *, kc, lam_init):
    _, bq, D = q_ref.shape
    S = k_ref.shape[1]
    h = pl.program_id(0)
    q0 = pl.program_id(1) * bq
    slope = jnp.float32(2.0 ** (-8.0 * DIFF_HEADS / DIFF_HEADS))
    for hh in range(DIFF_HEADS - 1):
        slope = jnp.where(h == hh, jnp.float32(2.0 ** (-8.0 * (hh + 1) / DIFF_HEADS)), slope)
    slope2 = slope * LOG2E
    k_loc = lax.broadcasted_iota(I32, (kc, LANES), 0).astype(F32)
    k_s = k_loc * slope2
    q_loc = lax.broadcasted_iota(I32, (1, bq), 1).astype(F32)
    n = S // kc

    def per_lane_group(s, fn):
        return jnp.concatenate([fn(s[:, g * LANES:(g + 1) * LANES], g) for g in range(bq // LANES)], axis=1)
    cur = pl.program_id(1) % 2
    nxt = 1 - cur

    def scores(j, c):
        wrap = c >= n
        off = pl.multiple_of(jnp.where(wrap, 0, c) * kc, kc)
        return _dot(k_ref[j, pl.ds(off, kc), :], qt_sc[jnp.where(wrap, nxt, cur), j])

    @pl.when(pl.program_id(1) == 0)
    def _():
        for j in range(2):
            qt_sc[0, j] = q_ref[j].T
        s_sc[0] = scores(0, 0)

    for j in range(2):
        qt_sc[nxt, j] = qnext_ref[j].T
    m_sc[...] = jnp.full(m_sc.shape, -jnp.inf, F32)
    l_sc[...] = jnp.zeros(l_sc.shape, F32)
    acc_sc[...] = jnp.zeros(acc_sc.shape, F32)

    def chunk_step(c, penalty):
        off = pl.multiple_of(c * kc, kc)
        delta = (off - q0).astype(F32) - q_loc
        vt = vt_ref[:, pl.ds(off, kc)]
        s_sc[1] = scores(1, c)
        s0, shift = penalty(s_sc[0], delta)
        _online_softmax_pv(s0, vt, m_sc.at[0], l_sc.at[0], acc_sc.at[0], shift)
        s_sc[0] = scores(0, c + 1)
        s1, shift = penalty(s_sc[1], delta)
        _online_softmax_pv(s1, vt, m_sc.at[1], l_sc.at[1], acc_sc.at[1], shift)

    def keys_before(c, carry):
        chunk_step(c, lambda s, delta: (per_lane_group(s, lambda t, g: t + k_s), slope2 * delta))
        return carry

    def keys_overlap(c, carry):
        def general(t, g, delta):
            return t - jnp.abs(k_s + slope2 * delta[:, g * LANES:(g + 1) * LANES])
        chunk_step(c, lambda s, delta: (per_lane_group(s, lambda t, g: general(t, g, delta)), None))
        return carry

    def keys_after(c, carry):
        chunk_step(c, lambda s, delta: (per_lane_group(s, lambda t, g: t - k_s), -(slope2 * delta)))
        return carry

    c_lo = q0 // kc
    c_hi = (q0 + bq + kc - 1) // kc
    lax.fori_loop(0, c_lo, keys_before, 0)
    lax.fori_loop(c_lo, c_hi, keys_overlap, 0)
    lax.fori_loop(c_hi, n, keys_after, 0)
    lam = (jnp.exp(jnp.sum(lq1_ref[...] * lk1_ref[...], axis=1, keepdims=True))
           - jnp.exp(jnp.sum(lq2_ref[...] * lk2_ref[...], axis=1, keepdims=True)) + lam_init)
    out = (acc_sc[0] / l_sc[0] - lam * (acc_sc[1] / l_sc[1])).T
    o_ref[...] = (_rms(out, subln_ref[...]) * (1.0 - lam_init)).astype(o_ref.dtype)


def _diff_attention(hq, vt_arr, lq1, lk1, lq2, lk2, subln, *, q_tile0, k_tile0, lam_init,
                    bq=512, kc=4096):
    _, S, D = hq.shape
    kc = min(kc, S)
    bq = min(bq, S)
    rows = 2 * D
    vec = pl.BlockSpec((1, D), lambda h, i: (0, 0))
    nq = S // bq
    return pl.pallas_call(
        functools.partial(_diff_kernel, kc=kc, lam_init=lam_init),
        grid=(DIFF_HEADS, nq),
        in_specs=[
            pl.BlockSpec((2, bq, D), lambda h, i: (q_tile0 // 2 + h, i, 0)),
            pl.BlockSpec((2, bq, D), lambda h, i: (q_tile0 // 2 + h, jnp.minimum(i + 1, nq - 1), 0)),
            pl.BlockSpec((2, S, D), lambda h, i: (k_tile0 // 2 + h, 0, 0), pipeline_mode=pl.Buffered(1)),
            pl.BlockSpec((rows, vt_arr.shape[1]), lambda h, i: (h, 0), pipeline_mode=pl.Buffered(1)),
            vec, vec, vec, vec,
            pl.BlockSpec((1, 2 * D), lambda h, i: (0, 0)),
        ],
        out_specs=pl.BlockSpec((bq, 2 * D), lambda h, i: (i, h)),
        out_shape=jax.ShapeDtypeStruct((S, DIFF_HEADS * 2 * D), BF16),
        scratch_shapes=[pltpu.VMEM((2, 2, D, bq), BF16), pltpu.VMEM((2, kc, bq), F32),
                        pltpu.VMEM((2, 1, bq), F32), pltpu.VMEM((2, 1, bq), F32),
                        pltpu.VMEM((2, rows, bq), F32)],
        compiler_params=_cparams(("parallel", "arbitrary")),
        name="diff_attention",
    )(hq, hq, hq, vt_arr, lq1, lk1, lq2, lk2, subln)


def _wout_kernel(*refs, n_in):
    a_refs = refs[:n_in]
    w_refs = refs[n_in:2 * n_in]
    x_ref, g_ref, b_ref, wr_ref, xo_ref, xb_ref, lg_ref = refs[2 * n_in:]
    mix = _dot(a_refs[0][...], w_refs[0][...])
    for a, w in zip(a_refs[1:], w_refs[1:]):
        mix = mix + _dot(a[...], w[...])
    y = _layer_norm(DEEPNORM_ALPHA * x_ref[...] + mix, g_ref[...], b_ref[...])
    xo_ref[...] = y
    y_hi = y.astype(BF16)
    xb_ref[...] = y_hi
    y_lo = (y - y_hi.astype(F32)).astype(BF16)
    wr = wr_ref[...]
    w_hi = wr.astype(BF16)
    w_lo = (wr - w_hi.astype(F32)).astype(BF16)
    E = wr.shape[0]
    both = _dot_nt(jnp.concatenate([w_hi, w_lo], axis=0), y_hi)
    lg_ref[...] = both[:E] + both[E:] + _dot_nt(w_hi, y_lo)


def _wout_ln_router(a_list, w_list, x, g, b, wr_t, *, tm=512):
    S, D = x.shape
    n_in = len(a_list)
    E = wr_t.shape[0]
    in_specs = ([pl.BlockSpec((tm, a.shape[1]), lambda i: (i, 0)) for a in a_list]
                + [pl.BlockSpec(w.shape, lambda i: (0, 0)) for w in w_list]
                + [pl.BlockSpec((tm, D), lambda i: (i, 0)),
                   pl.BlockSpec((1, D), lambda i: (0, 0)),
                   pl.BlockSpec((1, D), lambda i: (0, 0)),
                   pl.BlockSpec((E, D), lambda i: (0, 0))])
    return pl.pallas_call(
        functools.partial(_wout_kernel, n_in=n_in),
        grid=(S // tm,),
        in_specs=in_specs,
        out_specs=[pl.BlockSpec((tm, D), lambda i: (i, 0)),
                   pl.BlockSpec((tm, D), lambda i: (i, 0)),
                   pl.BlockSpec((E, tm), lambda i: (0, i))],
        out_shape=[jax.ShapeDtypeStruct((S, D), F32),
                   jax.ShapeDtypeStruct((S, D), BF16),
                   jax.ShapeDtypeStruct((E, S), F32)],
        compiler_params=_cparams(("parallel",)),
        name="wout_ln_router",
    )(*a_list, *w_list, x, g, b, wr_t)


def _select_kernel(lg_ref, pos_ref, gate_ref, off_ref, cnt_ref, *, E, nb, cap):
    lg = lg_ref[...].reshape(E, nb, LANES)
    mx = jnp.max(lg, axis=0, keepdims=True)
    ex = jnp.exp(lg - mx)
    aff = ex / jnp.sum(ex, axis=0, keepdims=True)
    bits = lax.bitcast_convert_type(aff, I32)

    def count(msk):
        c = jnp.sum(msk.astype(F32), axis=1, keepdims=True)
        return jnp.sum(c, axis=2, keepdims=True)

    def search(i, thr):
        cand = thr | lax.shift_left(jnp.int32(1), 30 - i)
        return jnp.where(count(bits >= cand) >= cap, cand, thr)

    thr = lax.fori_loop(0, 31, search, jnp.zeros((E, 1, 1), I32))
    gt = bits > thr
    eq = bits == thr
    need = cap - count(gt)

    col = lax.broadcasted_iota(I32, (LANES, LANES), 1)
    rw = lax.broadcasted_iota(I32, (LANES, LANES), 0)
    upper = (rw < col).astype(BF16)
    ones = jnp.ones((LANES, LANES), BF16)
    lower = (lax.broadcasted_iota(I32, (nb, nb), 1) < lax.broadcasted_iota(I32, (nb, nb), 0)).astype(BF16)

    def prefix(msk):
        m2 = jnp.where(msk, 1.0, 0.0).reshape(E * nb, LANES).astype(BF16)
        within = _dot(m2, upper).reshape(E, nb, LANES)
        tot = _dot(m2, ones).reshape(E, nb, LANES)
        offs = jnp.stack([_dot(lower, tot[e].astype(BF16)) for e in range(E)], axis=0)
        return within + offs, offs, tot

    eq_rank, _, _ = prefix(eq)
    sel = gt | (eq & (eq_rank < need))
    pos, offs, tot = prefix(sel)
    pos_ref[...] = jnp.where(sel, pos, -1.0).astype(I32).reshape(E * nb, LANES)
    gate_ref[...] = jnp.where(sel, aff, 0.0).reshape(E * nb, LANES)
    off_ref[...] = offs.astype(I32).reshape(E * nb, LANES)
    cnt_ref[...] = tot.astype(I32).reshape(E * nb, LANES)


def _select(logits_t, cap):
    E, S = logits_t.shape
    nb = S // LANES
    shp = (E * nb, LANES)
    full = pl.BlockSpec(shp, lambda: (0, 0))
    return pl.pallas_call(
        functools.partial(_select_kernel, E=E, nb=nb, cap=cap),
        in_specs=[full],
        out_specs=[full, full, full, full],
        out_shape=[jax.ShapeDtypeStruct(shp, I32), jax.ShapeDtypeStruct(shp, F32),
                   jax.ShapeDtypeStruct(shp, I32), jax.ShapeDtypeStruct(shp, I32)],
        compiler_params=pltpu.CompilerParams(vmem_limit_bytes=VMEM_LIMIT),
        name="select",
    )(logits_t.reshape(shp))


def _gather_kernel(off_s, cnt_s, x_ref, pos_ref, gate_ref, xe_ref, gs_ref, *, cap, tbs, ep):
    grp = pl.program_id(0)
    t = pl.program_id(1)

    @pl.when(t == 0)
    def _():
        xe_ref[...] = jnp.zeros(xe_ref.shape, xe_ref.dtype)
        gs_ref[...] = jnp.zeros(gs_ref.shape, gs_ref.dtype)

    def place(k, u, e, tb, width):
        base = jnp.minimum((off_s[e, tb] // BF16_ROWS) * BF16_ROWS, cap - width)
        base = pl.multiple_of(base, BF16_ROWS)
        slot = base + lax.broadcasted_iota(I32, (width, LANES), 0)
        hit = pos_ref[k, u] == slot
        rows = _dot(hit.astype(BF16), x_ref[pl.ds(u * LANES, LANES), :])
        win = pl.ds(base, width)
        xe_ref[k, win, :] = xe_ref[k, win, :] + rows.astype(xe_ref.dtype)
        g = jnp.sum(jnp.where(hit, gate_ref[k, u], 0.0), axis=1, keepdims=True)
        gs_ref[k, win, :] = gs_ref[k, win, :] + jnp.broadcast_to(g, (width, LANES))

    for u in range(tbs):
        for k in range(ep):
            e = grp * ep + k
            tb = t * tbs + u
            cnt = cnt_s[e, tb]

            @pl.when((cnt > 0) & (cnt <= GATHER_SMALL))
            def _():
                place(k, u, e, tb, GATHER_SMALL + BF16_ROWS)

            @pl.when(cnt > GATHER_SMALL)
            def _():
                place(k, u, e, tb, GATHER_WIN)


def _gather(off_s, cnt_s, x_bf, pos, gate, cap, *, tbs=16, ep=2):
    S, D = x_bf.shape
    E, nb = off_s.shape
    assert cap >= GATHER_WIN and cap % BF16_ROWS == 0 and nb % tbs == 0 and E % ep == 0
    pos4 = pos.reshape(E, nb, 1, LANES)
    gate4 = gate.reshape(E, nb, 1, LANES)
    return pl.pallas_call(
        functools.partial(_gather_kernel, cap=cap, tbs=tbs, ep=ep),
        grid_spec=pltpu.PrefetchScalarGridSpec(
            num_scalar_prefetch=2,
            grid=(E // ep, nb // tbs),
            in_specs=[
                pl.BlockSpec((tbs * LANES, D), lambda g, t, o, c: (t, 0)),
                pl.BlockSpec((ep, tbs, 1, LANES), lambda g, t, o, c: (g, t, 0, 0)),
                pl.BlockSpec((ep, tbs, 1, LANES), lambda g, t, o, c: (g, t, 0, 0)),
            ],
            out_specs=[pl.BlockSpec((ep, cap, D), lambda g, t, o, c: (g, 0, 0)),
                       pl.BlockSpec((ep, cap, LANES), lambda g, t, o, c: (g, 0, 0))],
        ),
        out_shape=[jax.ShapeDtypeStruct((E, cap, D), BF16),
                   jax.ShapeDtypeStruct((E, cap, LANES), F32)],
        compiler_params=_cparams(("parallel", "arbitrary")),
        name="gather",
    )(off_s, cnt_s, x_bf, pos4, gate4)


def _ffn_kernel(xe_ref, wg_ref, wu_ref, wd_ref, gs_ref, ye_ref, acc_sc, *, sub):
    fc = pl.program_id(2)
    rows = xe_ref.shape[1]
    @pl.when(fc == 0)
    def _():
        acc_sc[...] = jnp.zeros(acc_sc.shape, F32)

    wg = wg_ref[...].astype(BF16)
    wu = wu_ref[...].astype(BF16)
    wd = wd_ref[...].astype(BF16)
    for r in range(rows // sub):
        sl = pl.ds(r * sub, sub)
        xt = xe_ref[0, sl, :]
        hg = _dot(xt, wg)
        hu = _dot(xt, wu)
        hid = (hg * jax.nn.sigmoid(hg) * hu).astype(BF16)
        acc_sc[sl, :] = acc_sc[sl, :] + _dot(hid, wd)

    @pl.when(fc == pl.num_programs(2) - 1)
    def _():
        ye_ref[0] = (acc_sc[...] * gs_ref[0][:, :1]).astype(ye_ref.dtype)


def _ffn(xe, gs, w_gate, w_up, w_down, layer, *, halves=2, fcw=256):
    E, cap, D = xe.shape
    F = w_gate.shape[-1]
    rows = cap // halves
    sub = min(256, rows)
    return pl.pallas_call(
        functools.partial(_ffn_kernel, sub=sub),
        grid=(E, halves, F // fcw),
        in_specs=[
            pl.BlockSpec((1, rows, D), lambda e, t, f: (e, t, 0)),
            pl.BlockSpec((None, None, D, fcw), lambda e, t, f: (layer, e, 0, f)),
            pl.BlockSpec((None, None, D, fcw), lambda e, t, f: (layer, e, 0, f)),
            pl.BlockSpec((None, None, fcw, D), lambda e, t, f: (layer, e, f, 0)),
            pl.BlockSpec((1, rows, LANES), lambda e, t, f: (e, t, 0)),
        ],
        out_specs=pl.BlockSpec((1, rows, D), lambda e, t, f: (e, t, 0)),
        out_shape=jax.ShapeDtypeStruct((E, cap, D), BF16),
        scratch_shapes=[pltpu.VMEM((rows, D), F32)],
        compiler_params=_cparams(("parallel", "parallel", "arbitrary")),
        name="expert_ffn",
    )(xe, w_gate, w_up, w_down, gs)


def _combine_kernel(win_s, *refs, eg):
    y_refs = refs[:eg]
    post_ref, x_ref, g_ref, b_ref, xo_ref, xb_ref, acc_sc = refs[eg:]
    tb = pl.program_id(0)
    grp = pl.program_id(1)

    @pl.when(grp == 0)
    def _():
        acc_sc[...] = jnp.zeros(acc_sc.shape, F32)

    pt = post_ref[...].astype(F32)
    lane = lax.broadcasted_iota(I32, pt.shape, 1)
    col = lax.broadcasted_iota(I32, (LANES, GATHER_WIN), 1)
    total = None
    for k in range(eg):
        e = grp * eg + k
        pcol = jnp.sum(jnp.where(lane == e, pt, 0.0), axis=1, keepdims=True).astype(I32)
        hit = ((pcol - win_s[e, tb]) == col).astype(BF16)
        part = _dot(hit, y_refs[k][0])
        total = part if total is None else total + part
    acc_sc[...] = acc_sc[...] + total

    @pl.when(grp == pl.num_programs(1) - 1)
    def _():
        y = _layer_norm(DEEPNORM_ALPHA * x_ref[...] + acc_sc[...], g_ref[...], b_ref[...])
        xo_ref[...] = y
        xb_ref[...] = y.astype(BF16)


def _combine_ln(win_s, ye, pos_t, x, g, b, *, eg=16):
    S, D = x.shape
    E, cap, _ = ye.shape
    nb = S // LANES
    assert E % eg == 0

    def y_spec(k):
        return pl.BlockSpec((pl.Element(1), pl.Element(GATHER_WIN), pl.Element(D)),
                            lambda t, gr, ws: (gr * eg + k, (ws[gr * eg + k, t] // BF16_ROWS) * BF16_ROWS, 0))

    row = lambda t, gr, ws: (t, 0)
    fixed = lambda t, gr, ws: (0, 0)
    return pl.pallas_call(
        functools.partial(_combine_kernel, eg=eg),
        grid_spec=pltpu.PrefetchScalarGridSpec(
            num_scalar_prefetch=1,
            grid=(nb, E // eg),
            in_specs=([y_spec(k) for k in range(eg)]
                      + [pl.BlockSpec((LANES, E), row), pl.BlockSpec((LANES, D), row),
                         pl.BlockSpec((1, D), fixed), pl.BlockSpec((1, D), fixed)]),
            out_specs=[pl.BlockSpec((LANES, D), row), pl.BlockSpec((LANES, D), row)],
            scratch_shapes=[pltpu.VMEM((LANES, D), F32)],
        ),
        out_shape=[jax.ShapeDtypeStruct((S, D), F32), jax.ShapeDtypeStruct((S, D), BF16)],
        compiler_params=_cparams(("parallel", "arbitrary")),
        name="combine_ln",
    )(win_s, *([ye] * eg), pos_t, x, g, b)


def _moe(x, x_bf, logits_t, w_gate, w_up, w_down, layer, g, b):
    S, D = x.shape
    E = logits_t.shape[0]
    nb = S // LANES
    cap = EC_FACTOR * S // E
    pos, gate, off, cnt = _select(logits_t, cap)
    off_s = off[:, 0].reshape(E, nb)
    cnt_s = cnt[:, 0].reshape(E, nb)
    xe, gs = _gather(off_s, cnt_s, x_bf, pos, gate, cap)
    ye = _ffn(xe, gs, w_gate, w_up, w_down, layer)
    pos_t = pos.reshape(E, S).T
    win_s = jnp.minimum((off_s // BF16_ROWS) * BF16_ROWS, cap - GATHER_WIN)
    return _combine_ln(win_s, ye, pos_t, x, g, b)


def _mla_in_kernel(x_ref, w_ref, cos_ref, sin_ref, qn_ref, kvn_ref, cq_ref, ckv_ref, kr_ref):
    h = _dot(x_ref[...], w_ref[...])
    cq_ref[...] = _rms(h[:, :MLA_Q_RANK], qn_ref[...]).astype(BF16)
    ckv_ref[...] = _rms(h[:, MLA_Q_RANK:MLA_Q_RANK + MLA_KV_RANK], kvn_ref[...]).astype(BF16)
    kr = h[:, MLA_Q_RANK + MLA_KV_RANK:]
    kr_ref[...] = _rope(kr, cos_ref[...], sin_ref[...], MLA_ROPE // 4).astype(BF16)


def _mla_in(x_bf, w_bf, cos, sin, qn, kvn, *, tm=1024):
    S, K = x_bf.shape
    N = w_bf.shape[1]
    row = lambda i: (i, 0)
    fixed = lambda i: (0, 0)
    return pl.pallas_call(
        _mla_in_kernel,
        grid=(S // tm,),
        in_specs=[pl.BlockSpec((tm, K), row), pl.BlockSpec((K, N), fixed),
                  pl.BlockSpec((tm, LANES), row), pl.BlockSpec((tm, LANES), row),
                  pl.BlockSpec((1, MLA_Q_RANK), fixed), pl.BlockSpec((1, MLA_KV_RANK), fixed)],
        out_specs=[pl.BlockSpec((tm, MLA_Q_RANK), row), pl.BlockSpec((tm, MLA_KV_RANK), row),
                   pl.BlockSpec((tm, LANES), row)],
        out_shape=[jax.ShapeDtypeStruct((S, MLA_Q_RANK), BF16),
                   jax.ShapeDtypeStruct((S, MLA_KV_RANK), BF16),
                   jax.ShapeDtypeStruct((S, LANES), BF16)],
        compiler_params=_cparams(("parallel",)),
        name="mla_in",
    )(x_bf, w_bf, cos, sin, qn, kvn)


def _mla_q_kernel(c_ref, w_ref, cos_ref, sin_ref, o_ref, *, heads):
    h = _dot(c_ref[...], w_ref[...])
    scale = (MLA_NOPE + MLA_ROPE) ** -0.5 * LOG2E
    for c in range(heads):
        base = c * MLA_QK_PAD
        o_ref[c, :, :MLA_NOPE] = (h[:, base:base + MLA_NOPE] * scale).astype(BF16)
        rp = _rope(h[:, base + MLA_NOPE:base + MLA_QK_PAD], cos_ref[...], sin_ref[...], MLA_ROPE // 4)
        o_ref[c, :, MLA_NOPE:] = (rp * scale).astype(BF16)


def _mla_q(cq, w_bf, cos, sin, *, tm=1024, heads=4):
    S, K = cq.shape
    tn = heads * MLA_QK_PAD
    return pl.pallas_call(
        functools.partial(_mla_q_kernel, heads=heads),
        grid=(S // tm, MLA_HEADS // heads),
        in_specs=[pl.BlockSpec((tm, K), lambda i, j: (i, 0)),
                  pl.BlockSpec((K, tn), lambda i, j: (0, j)),
                  pl.BlockSpec((tm, LANES), lambda i, j: (i, 0)),
                  pl.BlockSpec((tm, LANES), lambda i, j: (i, 0))],
        out_specs=pl.BlockSpec((heads, tm, MLA_QK_PAD), lambda i, j: (j, i, 0)),
        out_shape=jax.ShapeDtypeStruct((MLA_HEADS, S, MLA_QK_PAD), BF16),
        compiler_params=_cparams(("parallel", "parallel")),
        name="mla_q",
    )(cq, w_bf, cos, sin)


def _mla_k_kernel(c_ref, wk_ref, kr_ref, k_ref, *, heads):
    kn = _dot(c_ref[...], wk_ref[...])
    for hh in range(heads):
        k_ref[hh, :, :MLA_NOPE] = kn[:, hh * MLA_NOPE:(hh + 1) * MLA_NOPE].astype(BF16)
        k_ref[hh, :, MLA_NOPE:] = kr_ref[...]


def _mla_k(ckv, wk_bf, kr, *, tm=1024, heads=4):
    S, K = ckv.shape
    return pl.pallas_call(
        functools.partial(_mla_k_kernel, heads=heads),
        grid=(S // tm, MLA_HEADS // heads),
        in_specs=[pl.BlockSpec((tm, K), lambda i, j: (i, 0)),
                  pl.BlockSpec((K, heads * MLA_NOPE), lambda i, j: (0, j)),
                  pl.BlockSpec((tm, LANES), lambda i, j: (i, 0))],
        out_specs=pl.BlockSpec((heads, tm, MLA_QK_PAD), lambda i, j: (j, i, 0)),
        out_shape=jax.ShapeDtypeStruct((MLA_HEADS, S, MLA_QK_PAD), BF16),
        compiler_params=_cparams(("parallel", "parallel")),
        name="mla_k",
    )(ckv, wk_bf, kr)


def kernel(x, ab_w_in, ab_q_norm, ab_k_norm, ab_lambda_q1, ab_lambda_k1, ab_lambda_q2, ab_lambda_k2, ab_subln, ab_w_out, mla_w_in, mla_q_norm, mla_kv_norm, mla_w_uq, mla_w_ukv, mla_w_out, ln_mix_g, ln_mix_b, moe_w_router, moe_w_gate, moe_w_up, moe_w_down, ln_ffn_g, ln_ffn_b):
    B, S, D = x.shape
    assert B == 1 and D == D_MODEL and S % 512 == 0
    xf = x.reshape(S, D)
    x_bf = xf.astype(BF16)
    cos_a, sin_a = _rope_tables(S, HEAD_DIM // 4, HEAD_DIM)
    cos_c, sin_c = _rope_tables(S, MLA_ROPE // 4, LANES)
    row2 = lambda v: v.reshape(1, -1)

    lam_init = 0.8 - 0.6 * math.exp(-0.3 * 0)
    w_in = ab_w_in[0].astype(BF16)
    c_av = A_Q + A_KV
    c_bq = c_av + A_KV
    c_bv = c_bq + 2 * B_QK
    w_qk = jnp.concatenate([w_in[:, :c_av], w_in[:, c_bq:c_bv]], axis=1)
    hq = _proj0(x_bf, w_qk, cos_a, sin_a, row2(ab_q_norm[0]), row2(ab_k_norm[0]))
    vt_a = _values_t(w_in[:, c_av:c_bq], x_bf, name="values_t_gqa")
    vt_b = _values_t(w_in[:, c_bv:], x_bf, name="values_t_diff")
    t_ak = A_Q // HEAD_DIM
    t_bq = t_ak + GQA_KV_HEADS
    t_bk = t_bq + 2 * DIFF_HEADS
    a_out = _flash(hq, hq, vt_a, n_kv=GQA_KV_HEADS, rep=GQA_Q_HEADS // GQA_KV_HEADS,
                   q_tile0=0, k_tile0=t_ak, dv=HEAD_DIM, bq=128, kc=8192, kv_buffers=1, name="gqa_attention")
    b_out = _diff_attention(hq, vt_b, row2(ab_lambda_q1[0]), row2(ab_lambda_k1[0]),
                            row2(ab_lambda_q2[0]), row2(ab_lambda_k2[0]), row2(ab_subln[0]),
                            q_tile0=t_bq, k_tile0=t_bk, lam_init=lam_init)
    w_out = ab_w_out[0].astype(BF16)
    xf, x_bf, logits_t = _wout_ln_router(
        [a_out, b_out], [w_out[:A_Q], w_out[A_Q:]], xf, row2(ln_mix_g[0]), row2(ln_mix_b[0]),
        moe_w_router[0].T)
    xf, x_bf = _moe(xf, x_bf, logits_t, moe_w_gate, moe_w_up, moe_w_down, 0,
                    row2(ln_ffn_g[0]), row2(ln_ffn_b[0]))

    w1 = jnp.pad(mla_w_in[0], ((0, 0), (0, LANES - MLA_ROPE))).astype(BF16)
    cq, ckv, kr = _mla_in(x_bf, w1, cos_c, sin_c, row2(mla_q_norm[0]), row2(mla_kv_norm[0]))
    w_uq = mla_w_uq[0].reshape(MLA_Q_RANK, MLA_HEADS, MLA_NOPE + MLA_ROPE)
    w_uq = jnp.pad(w_uq, ((0, 0), (0, 0), (0, MLA_QK_PAD - MLA_NOPE - MLA_ROPE)))
    w_uq = w_uq.reshape(MLA_Q_RANK, MLA_HEADS * MLA_QK_PAD).astype(BF16)
    w_ukv = mla_w_ukv[0].reshape(MLA_KV_RANK, MLA_HEADS, MLA_NOPE + MLA_V)
    w_uk = w_ukv[:, :, :MLA_NOPE].reshape(MLA_KV_RANK, MLA_HEADS * MLA_NOPE).astype(BF16)
    w_uv = w_ukv[:, :, MLA_NOPE:].reshape(MLA_KV_RANK, MLA_HEADS * MLA_V)
    q_pad = _mla_q(cq, w_uq, cos_c, sin_c)
    k_pad = _mla_k(ckv, w_uk, kr)
    vt1 = _values_t(w_uv, ckv, name="values_t_mla")
    c_out = _flash(q_pad, k_pad, vt1, n_kv=MLA_HEADS, rep=1, q_tile0=0, k_tile0=0, dv=MLA_V,
                   bq=512, kc=8192, kv_buffers=1, name="mla_attention")
    xf, x_bf, logits_t = _wout_ln_router(
        [c_out], [mla_w_out[0].astype(BF16)], xf, row2(ln_mix_g[1]), row2(ln_mix_b[1]),
        moe_w_router[1].T)
    xf, x_bf = _moe(xf, x_bf, logits_t, moe_w_gate, moe_w_up, moe_w_down, 1,
                    row2(ln_ffn_g[1]), row2(ln_ffn_b[1]))
    return xf.reshape(B, S, D)
```

```python
import functools
import math

import jax
import jax.numpy as jnp
from jax import lax
from jax.experimental import pallas as pl
from jax.experimental.pallas import tpu as pltpu

F32 = jnp.float32
BF16 = jnp.bfloat16
I32 = jnp.int32

D_MODEL = 2048
DEPTH = 2
GRID_W = 64
ROPE_THETA = 10000.0
NORM_EPS = 1e-6
LN_EPS = 1e-5

HEAD_DIM = 128
GQA_Q_HEADS = 8
GQA_KV_HEADS = 2
DIFF_HEADS = 4
A_Q = GQA_Q_HEADS * HEAD_DIM
A_KV = GQA_KV_HEADS * HEAD_DIM
B_QK = DIFF_HEADS * 2 * HEAD_DIM
B_V = DIFF_HEADS * 2 * HEAD_DIM

MLA_HEADS = 16
MLA_Q_RANK = 512
MLA_KV_RANK = 512
MLA_NOPE = 128
MLA_ROPE = 64
MLA_V = 128
MLA_QK_PAD = 256

N_EXPERTS = 16
EXPERT_FF = 2048
EC_FACTOR = 2

DEEPNORM_ALPHA = (2 * DEPTH) ** 0.25
LOG2E = math.log2(math.e)

LANES = 128
BF16_ROWS = 16
GATHER_WIN = LANES + BF16_ROWS
GATHER_SMALL = 32
VMEM_LIMIT = 56 * 1024 * 1024


def _cparams(sem, vmem=VMEM_LIMIT):
    return pltpu.CompilerParams(dimension_semantics=sem, vmem_limit_bytes=vmem)


def _dot(a, b):
    return jnp.dot(a, b, preferred_element_type=F32)


def _dot_nt(a, b):
    return lax.dot_general(a, b, (((1,), (1,)), ((), ())), preferred_element_type=F32)


def _rope(y, cos, sin, half):
    n = y.shape[-1]
    lane = lax.broadcasted_iota(I32, y.shape, 1)
    up = pltpu.roll(y, n - half, 1)
    dn = pltpu.roll(y, half, 1)
    partner = jnp.where((lane % (2 * half)) < half, up, dn)
    return y * cos + partner * sin


def _rms(y, w):
    return y * lax.rsqrt(jnp.mean(y * y, axis=-1, keepdims=True) + NORM_EPS) * w


def _layer_norm(y, g, b):
    mu = jnp.mean(y, axis=-1, keepdims=True)
    d = y - mu
    var = jnp.mean(d * d, axis=-1, keepdims=True)
    return d * lax.rsqrt(var + LN_EPS) * g + b


def _rope_tables(S, half, width):
    n_rows = S // GRID_W
    n = 2 * half
    freqs = ROPE_THETA ** (-jnp.arange(0, n, 2, dtype=F32) / n)[None, :]
    parts_c, parts_s = [], []
    for count, along_rows in ((n_rows, True), (GRID_W, False)):
        ang = jnp.arange(count, dtype=F32)[:, None] * freqs
        c = jnp.concatenate([jnp.cos(ang)] * 2, axis=1)
        s = jnp.concatenate([-jnp.sin(ang), jnp.sin(ang)], axis=1)
        shape = (n_rows, GRID_W, n)
        expand = (lambda a: a[:, None, :]) if along_rows else (lambda a: a[None, :, :])
        parts_c.append(jnp.broadcast_to(expand(c), shape).reshape(S, n))
        parts_s.append(jnp.broadcast_to(expand(s), shape).reshape(S, n))
    pad = width - 4 * half
    if pad:
        parts_c.append(jnp.ones((S, pad), F32))
        parts_s.append(jnp.zeros((S, pad), F32))
    return jnp.concatenate(parts_c, axis=1), jnp.concatenate(parts_s, axis=1)


def _proj0_kernel(x_ref, w_ref, cos_ref, sin_ref, qn_ref, kn_ref, o_ref, *, tn):
    j = pl.program_id(1)
    h = _dot(x_ref[...], w_ref[...])
    nsub = tn // HEAD_DIM
    scale = HEAD_DIM ** -0.5 * LOG2E
    k_tile = A_Q // tn
    bq_lo = (A_Q + A_KV) // tn
    bq_hi = bq_lo + B_QK // tn

    def normed(c, w):
        y = _rms(h[:, c * HEAD_DIM:(c + 1) * HEAD_DIM], w)
        return _rope(y, cos_ref[...], sin_ref[...], HEAD_DIM // 4)

    @pl.when(j < k_tile)
    def _():
        for c in range(nsub):
            o_ref[c] = (normed(c, qn_ref[...]) * scale).astype(o_ref.dtype)

    @pl.when(j == k_tile)
    def _():
        for c in range(nsub):
            o_ref[c] = normed(c, kn_ref[...]).astype(o_ref.dtype)

    @pl.when(j > k_tile)
    def _():
        mul = jnp.where((j >= bq_lo) & (j < bq_hi), scale, 1.0).astype(F32)
        for c in range(nsub):
            o_ref[c] = (h[:, c * HEAD_DIM:(c + 1) * HEAD_DIM] * mul).astype(o_ref.dtype)


def _proj0(x_bf, w_bf, cos, sin, qn, kn, *, tm=2048, tn=256):
    S, K = x_bf.shape
    N = w_bf.shape[1]
    assert A_Q % tn == 0 and tn == A_KV and N % tn == 0
    return pl.pallas_call(
        functools.partial(_proj0_kernel, tn=tn),
        grid=(S // tm, N // tn),
        in_specs=[
            pl.BlockSpec((tm, K), lambda i, j: (i, 0)),
            pl.BlockSpec((K, tn), lambda i, j: (0, j)),
            pl.BlockSpec((tm, HEAD_DIM), lambda i, j: (i, 0)),
            pl.BlockSpec((tm, HEAD_DIM), lambda i, j: (i, 0)),
            pl.BlockSpec((1, HEAD_DIM), lambda i, j: (0, 0)),
            pl.BlockSpec((1, HEAD_DIM), lambda i, j: (0, 0)),
        ],
        out_specs=pl.BlockSpec((tn // HEAD_DIM, tm, HEAD_DIM), lambda i, j: (j, i, 0)),
        out_shape=jax.ShapeDtypeStruct((N // HEAD_DIM, S, HEAD_DIM), BF16),
        compiler_params=_cparams(("parallel", "parallel")),
        name="proj0",
    )(x_bf, w_bf, cos, sin, qn, kn)


def _values_t_kernel(wt_ref, x_ref, o_ref):
    o_ref[...] = _dot_nt(wt_ref[...], x_ref[...]).astype(o_ref.dtype)


def _values_t(w, x_bf, *, tm=2048, tn=256, name="values_t"):
    S, K = x_bf.shape
    N = w.shape[1]
    return pl.pallas_call(
        _values_t_kernel,
        grid=(S // tm, N // tn),
        in_specs=[pl.BlockSpec((tn, K), lambda i, j: (j, 0)),
                  pl.BlockSpec((tm, K), lambda i, j: (i, 0))],
        out_specs=pl.BlockSpec((tn, tm), lambda i, j: (j, i)),
        out_shape=jax.ShapeDtypeStruct((N, S), BF16),
        compiler_params=_cparams(("parallel", "parallel")),
        name=name,
    )(w.T.astype(BF16), x_bf)


def _online_softmax_pv(s, vt, m_ref, l_ref, acc_ref, shift=None):
    m_prev = m_ref[...]
    m_chunk = jnp.max(s, axis=0, keepdims=True)
    if shift is None:
        m_new = jnp.maximum(m_prev, m_chunk)
        p = jnp.exp2(s - m_new)
    else:
        m_new = jnp.maximum(m_prev, m_chunk + shift)
        p = jnp.exp2(s - (m_new - shift))
    alpha = jnp.exp2(m_prev - m_new)
    l_ref[...] = alpha * l_ref[...] + jnp.sum(p, axis=0, keepdims=True)
    acc_ref[...] = alpha * acc_ref[...] + _dot(vt, p.astype(BF16))
    m_ref[...] = m_new


def _flash_kernel(q_ref, qnext_ref, k_ref, vt_ref, o_ref, qt_sc, s_sc, m_sc, l_sc, acc_sc, *, kc):
    R, bq, D = q_ref.shape
    S = k_ref.shape[1]
    Dv = vt_ref.shape[0]
    n = S // kc
    assert n % 2 == 0
    cur = pl.program_id(1) % 2
    nxt = 1 - cur

    def scores(c):
        wrap = c >= n
        off = pl.multiple_of(jnp.where(wrap, 0, c) * kc, kc)
        return _dot(k_ref[0, pl.ds(off, kc), :], qt_sc[jnp.where(wrap, nxt, cur)])

    def values_t(c):
        return vt_ref[:, pl.ds(pl.multiple_of(c * kc, kc), kc)]

    @pl.when(pl.program_id(1) == 0)
    def _():
        qt_sc[0] = q_ref[...].reshape(R * bq, D).T
        s_sc[0] = scores(0)

    qt_sc[nxt] = qnext_ref[...].reshape(R * bq, D).T
    m_sc[...] = jnp.full(m_sc.shape, -jnp.inf, F32)
    l_sc[...] = jnp.zeros(l_sc.shape, F32)
    acc_sc[...] = jnp.zeros(acc_sc.shape, F32)

    def body(i, carry):
        c = 2 * i
        s_sc[1] = scores(c + 1)
        _online_softmax_pv(s_sc[0], values_t(c), m_sc, l_sc, acc_sc)
        s_sc[0] = scores(c + 2)
        _online_softmax_pv(s_sc[1], values_t(c + 1), m_sc, l_sc, acc_sc)
        return carry

    lax.fori_loop(0, n // 2, body, 0)
    out = (acc_sc[...] / l_sc[...]).T
    for r in range(R):
        o_ref[:, r * Dv:(r + 1) * Dv] = out[r * bq:(r + 1) * bq].astype(o_ref.dtype)


def _flash(q_arr, k_arr, vt_arr, *, n_kv, rep, q_tile0, k_tile0, dv, bq, kc, kv_buffers, name):
    _, S, D = q_arr.shape
    kc = min(kc, S // 2)
    bq = min(bq, S)
    assert q_tile0 % rep == 0
    N = rep * bq
    rows = dv
    nq = S // bq
    return pl.pallas_call(
        functools.partial(_flash_kernel, kc=kc),
        grid=(n_kv, nq),
        in_specs=[
            pl.BlockSpec((rep, bq, D), lambda g, i: (q_tile0 // rep + g, i, 0)),
            pl.BlockSpec((rep, bq, D), lambda g, i: (q_tile0 // rep + g, jnp.minimum(i + 1, nq - 1), 0)),
            pl.BlockSpec((1, S, D), lambda g, i: (k_tile0 + g, 0, 0), pipeline_mode=pl.Buffered(kv_buffers)),
            pl.BlockSpec((rows, vt_arr.shape[1]), lambda g, i: (g, 0), pipeline_mode=pl.Buffered(kv_buffers)),
        ],
        out_specs=pl.BlockSpec((bq, rep * dv), lambda g, i: (i, g)),
        out_shape=jax.ShapeDtypeStruct((S, n_kv * rep * dv), BF16),
        scratch_shapes=[pltpu.VMEM((2, D, N), BF16),
                        pltpu.VMEM((2, kc, N), F32),
                        pltpu.VMEM((1, N), F32), pltpu.VMEM((1, N), F32), pltpu.VMEM((rows, N), F32)],
        compiler_params=_cparams(("parallel", "arbitrary")),
        name=name,
    )(q_arr, q_arr, k_arr, vt_arr)


def _diff_kernel(q_ref, qnext_ref, k_ref, vt_ref, lq1_ref, lk1_ref, lq2_ref, lk2_ref, subln_ref, o_ref,
                 qt_sc, s_sc, m_sc, l_sc, acc_sc, *, kc, lam_init):
    _, bq, D = q_ref.shape
    S = k_ref.shape[1]
    h = pl.program_id(0)
    q0 = pl.program_id(1) * bq
    slope = jnp.float32(2.0 ** (-8.0 * DIFF_HEADS / DIFF_HEADS))
    for hh in range(DIFF_HEADS - 1):
        slope = jnp.where(h == hh, jnp.float32(2.0 ** (-8.0 * (hh + 1) / DIFF_HEADS)), slope)
    slope2 = slope * LOG2E
    k_loc = lax.broadcasted_iota(I32, (kc, LANES), 0).astype(F32)
    k_s = k_loc * slope2
    q_loc = lax.broadcasted_iota(I32, (1, bq), 1).astype(F32)
    n = S // kc

    def per_lane_group(s, fn):
        return jnp.concatenate([fn(s[:, g * LANES:(g + 1) * LANES], g) for g in range(bq // LANES)], axis=1)
    cur = pl.program_id(1) % 2
    nxt = 1 - cur

    def scores(j, c):
        wrap = c >= n
        off = pl.multiple_of(jnp.where(wrap, 0, c) * kc, kc)
        return _dot(k_ref[j, pl.ds(off, kc), :], qt_sc[jnp.where(wrap, nxt, cur), j])

    @pl.when(pl.program_id(1) == 0)
    def _():
        for j in range(2):
            qt_sc[0, j] = q_ref[j].T
        s_sc[0] = scores(0, 0)

    for j in range(2):
        qt_sc[nxt, j] = qnext_ref[j].T
    m_sc[...] = jnp.full(m_sc.shape, -jnp.inf, F32)
    l_sc[...] = jnp.zeros(l_sc.shape, F32)
    acc_sc[...] = jnp.zeros(acc_sc.shape, F32)

    def chunk_step(c, penalty):
        off = pl.multiple_of(c * kc, kc)
        delta = (off - q0).astype(F32) - q_loc
        vt = vt_ref[:, pl.ds(off, kc)]
        s_sc[1] = scores(1, c)
        s0, shift = penalty(s_sc[0], delta)
        _online_softmax_pv(s0, vt, m_sc.at[0], l_sc.at[0], acc_sc.at[0], shift)
        s_sc[0] = scores(0, c + 1)
        s1, shift = penalty(s_sc[1], delta)
        _online_softmax_pv(s1, vt, m_sc.at[1], l_sc.at[1], acc_sc.at[1], shift)

    def keys_before(c, carry):
        chunk_step(c, lambda s, delta: (per_lane_group(s, lambda t, g: t + k_s), slope2 * delta))
        return carry

    def keys_overlap(c, carry):
        def general(t, g, delta):
            return t - jnp.abs(k_s + slope2 * delta[:, g * LANES:(g + 1) * LANES])
        chunk_step(c, lambda s, delta: (per_lane_group(s, lambda t, g: general(t, g, delta)), None))
        return carry

    def keys_after(c, carry):
        chunk_step(c, lambda s, delta: (per_lane_group(s, lambda t, g: t - k_s), -(slope2 * delta)))
        return carry

    c_lo = q0 // kc
    c_hi = (q0 + bq + kc - 1) // kc
    lax.fori_loop(0, c_lo, keys_before, 0)
    lax.fori_loop(c_lo, c_hi, keys_overlap, 0)
    lax.fori_loop(c_hi, n, keys_after, 0)
    lam = (jnp.exp(jnp.sum(lq1_ref[...] * lk1_ref[...], axis=1, keepdims=True))
           - jnp.exp(jnp.sum(lq2_ref[...] * lk2_ref[...], axis=1, keepdims=True)) + lam_init)
    out = (acc_sc[0] / l_sc[0] - lam * (acc_sc[1] / l_sc[1])).T
    o_ref[...] = (_rms(out, subln_ref[...]) * (1.0 - lam_init)).astype(o_ref.dtype)


def _diff_attention(hq, vt_arr, lq1, lk1, lq2, lk2, subln, *, q_tile0, k_tile0, lam_init,
                    bq=512, kc=4096):
    _, S, D = hq.shape
    kc = min(kc, S)
    bq = min(bq, S)
    rows = 2 * D
    vec = pl.BlockSpec((1, D), lambda h, i: (0, 0))
    nq = S // bq
    return pl.pallas_call(
        functools.partial(_diff_kernel, kc=kc, lam_init=lam_init),
        grid=(DIFF_HEADS, nq),
        in_specs=[
            pl.BlockSpec((2, bq, D), lambda h, i: (q_tile0 // 2 + h, i, 0)),
            pl.BlockSpec((2, bq, D), lambda h, i: (q_tile0 // 2 + h, jnp.minimum(i + 1, nq - 1), 0)),
            pl.BlockSpec((2, S, D), lambda h, i: (k_tile0 // 2 + h, 0, 0), pipeline_mode=pl.Buffered(1)),
            pl.BlockSpec((rows, vt_arr.shape[1]), lambda h, i: (h, 0), pipeline_mode=pl.Buffered(1)),
            vec, vec, vec, vec,
            pl.BlockSpec((1, 2 * D), lambda h, i: (0, 0)),
        ],
        out_specs=pl.BlockSpec((bq, 2 * D), lambda h, i: (i, h)),
        out_shape=jax.ShapeDtypeStruct((S, DIFF_HEADS * 2 * D), BF16),
        scratch_shapes=[pltpu.VMEM((2, 2, D, bq), BF16), pltpu.VMEM((2, kc, bq), F32),
                        pltpu.VMEM((2, 1, bq), F32), pltpu.VMEM((2, 1, bq), F32),
                        pltpu.VMEM((2, rows, bq), F32)],
        compiler_params=_cparams(("parallel", "arbitrary")),
        name="diff_attention",
    )(hq, hq, hq, vt_arr, lq1, lk1, lq2, lk2, subln)


def _wout_kernel(*refs, n_in):
    a_refs = refs[:n_in]
    w_refs = refs[n_in:2 * n_in]
    x_ref, g_ref, b_ref, wr_ref, xo_ref, xb_ref, lg_ref = refs[2 * n_in:]
    mix = _dot(a_refs[0][...], w_refs[0][...])
    for a, w in zip(a_refs[1:], w_refs[1:]):
        mix = mix + _dot(a[...], w[...])
    y = _layer_norm(DEEPNORM_ALPHA * x_ref[...] + mix, g_ref[...], b_ref[...])
    xo_ref[...] = y
    y_hi = y.astype(BF16)
    xb_ref[...] = y_hi
    y_lo = (y - y_hi.astype(F32)).astype(BF16)
    wr = wr_ref[...]
    w_hi = wr.astype(BF16)
    w_lo = (wr - w_hi.astype(F32)).astype(BF16)
    E = wr.shape[0]
    both = _dot_nt(jnp.concatenate([w_hi, w_lo], axis=0), y_hi)
    lg_ref[...] = both[:E] + both[E:] + _dot_nt(w_hi, y_lo)


def _wout_ln_router(a_list, w_list, x, g, b, wr_t, *, tm=512):
    S, D = x.shape
    n_in = len(a_list)
    E = wr_t.shape[0]
    in_specs = ([pl.BlockSpec((tm, a.shape[1]), lambda i: (i, 0)) for a in a_list]
                + [pl.BlockSpec(w.shape, lambda i: (0, 0)) for w in w_list]
                + [pl.BlockSpec((tm, D), lambda i: (i, 0)),
                   pl.BlockSpec((1, D), lambda i: (0, 0)),
                   pl.BlockSpec((1, D), lambda i: (0, 0)),
                   pl.BlockSpec((E, D), lambda i: (0, 0))])
    return pl.pallas_call(
        functools.partial(_wout_kernel, n_in=n_in),
        grid=(S // tm,),
        in_specs=in_specs,
        out_specs=[pl.BlockSpec((tm, D), lambda i: (i, 0)),
                   pl.BlockSpec((tm, D), lambda i: (i, 0)),
                   pl.BlockSpec((E, tm), lambda i: (0, i))],
        out_shape=[jax.ShapeDtypeStruct((S, D), F32),
                   jax.ShapeDtypeStruct((S, D), BF16),
                   jax.ShapeDtypeStruct((E, S), F32)],
        compiler_params=_cparams(("parallel",)),
        name="wout_ln_router",
    )(*a_list, *w_list, x, g, b, wr_t)


def _select_kernel(lg_ref, pos_ref, gate_ref, off_ref, cnt_ref, *, E, nb, cap):
    lg = lg_ref[...].reshape(E, nb, LANES)
    mx = jnp.max(lg, axis=0, keepdims=True)
    ex = jnp.exp(lg - mx)
    aff = ex / jnp.sum(ex, axis=0, keepdims=True)
    bits = lax.bitcast_convert_type(aff, I32)

    def count(msk):
        c = jnp.sum(msk.astype(F32), axis=1, keepdims=True)
        return jnp.sum(c, axis=2, keepdims=True)

    def search(i, thr):
        cand = thr | lax.shift_left(jnp.int32(1), 30 - i)
        return jnp.where(count(bits >= cand) >= cap, cand, thr)

    thr = lax.fori_loop(0, 31, search, jnp.zeros((E, 1, 1), I32))
    gt = bits > thr
    eq = bits == thr
    need = cap - count(gt)

    col = lax.broadcasted_iota(I32, (LANES, LANES), 1)
    rw = lax.broadcasted_iota(I32, (LANES, LANES), 0)
    upper = (rw < col).astype(BF16)
    ones = jnp.ones((LANES, LANES), BF16)
    lower = (lax.broadcasted_iota(I32, (nb, nb), 1) < lax.broadcasted_iota(I32, (nb, nb), 0)).astype(BF16)

    def prefix(msk):
        m2 = jnp.where(msk, 1.0, 0.0).reshape(E * nb, LANES).astype(BF16)
        within = _dot(m2, upper).reshape(E, nb, LANES)
        tot = _dot(m2, ones).reshape(E, nb, LANES)
        offs = jnp.stack([_dot(lower, tot[e].astype(BF16)) for e in range(E)], axis=0)
        return within + offs, offs, tot

    eq_rank, _, _ = prefix(eq)
    sel = gt | (eq & (eq_rank < need))
    pos, offs, tot = prefix(sel)
    pos_ref[...] = jnp.where(sel, pos, -1.0).astype(I32).reshape(E * nb, LANES)
    gate_ref[...] = jnp.where(sel, aff, 0.0).reshape(E * nb, LANES)
    off_ref[...] = offs.astype(I32).reshape(E * nb, LANES)
    cnt_ref[...] = tot.astype(I32).reshape(E * nb, LANES)


def _select(logits_t, cap):
    E, S = logits_t.shape
    nb = S // LANES
    shp = (E * nb, LANES)
    full = pl.BlockSpec(shp, lambda: (0, 0))
    return pl.pallas_call(
        functools.partial(_select_kernel, E=E, nb=nb, cap=cap),
        in_specs=[full],
        out_specs=[full, full, full, full],
        out_shape=[jax.ShapeDtypeStruct(shp, I32), jax.ShapeDtypeStruct(shp, F32),
                   jax.ShapeDtypeStruct(shp, I32), jax.ShapeDtypeStruct(shp, I32)],
        compiler_params=pltpu.CompilerParams(vmem_limit_bytes=VMEM_LIMIT),
        name="select",
    )(logits_t.reshape(shp))


def _gather_kernel(off_s, cnt_s, x_ref, pos_ref, gate_ref, xe_ref, gs_ref, *, cap, tbs, ep):
    grp = pl.program_id(0)
    t = pl.program_id(1)

    @pl.when(t == 0)
    def _():
        xe_ref[...] = jnp.zeros(xe_ref.shape, xe_ref.dtype)
        gs_ref[...] = jnp.zeros(gs_ref.shape, gs_ref.dtype)

    def place(k, u, e, tb, width):
        base = jnp.minimum((off_s[e, tb] // BF16_ROWS) * BF16_ROWS, cap - width)
        base = pl.multiple_of(base, BF16_ROWS)
        slot = base + lax.broadcasted_iota(I32, (width, LANES), 0)
        hit = pos_ref[k, u] == slot
        rows = _dot(hit.astype(BF16), x_ref[pl.ds(u * LANES, LANES), :])
        win = pl.ds(base, width)
        xe_ref[k, win, :] = xe_ref[k, win, :] + rows.astype(xe_ref.dtype)
        g = jnp.sum(jnp.where(hit, gate_ref[k, u], 0.0), axis=1, keepdims=True)
        gs_ref[k, win, :] = gs_ref[k, win, :] + jnp.broadcast_to(g, (width, LANES))

    for u in range(tbs):
        for k in range(ep):
            e = grp * ep + k
            tb = t * tbs + u
            cnt = cnt_s[e, tb]

            @pl.when((cnt > 0) & (cnt <= GATHER_SMALL))
            def _():
                place(k, u, e, tb, GATHER_SMALL + BF16_ROWS)

            @pl.when(cnt > GATHER_SMALL)
            def _():
                place(k, u, e, tb, GATHER_WIN)


def _gather(off_s, cnt_s, x_bf, pos, gate, cap, *, tbs=16, ep=2):
    S, D = x_bf.shape
    E, nb = off_s.shape
    assert cap >= GATHER_WIN and cap % BF16_ROWS == 0 and nb % tbs == 0 and E % ep == 0
    pos4 = pos.reshape(E, nb, 1, LANES)
    gate4 = gate.reshape(E, nb, 1, LANES)
    return pl.pallas_call(
        functools.partial(_gather_kernel, cap=cap, tbs=tbs, ep=ep),
        grid_spec=pltpu.PrefetchScalarGridSpec(
            num_scalar_prefetch=2,
            grid=(E // ep, nb // tbs),
            in_specs=[
                pl.BlockSpec((tbs * LANES, D), lambda g, t, o, c: (t, 0)),
                pl.BlockSpec((ep, tbs, 1, LANES), lambda g, t, o, c: (g, t, 0, 0)),
                pl.BlockSpec((ep, tbs, 1, LANES), lambda g, t, o, c: (g, t, 0, 0)),
            ],
            out_specs=[pl.BlockSpec((ep, cap, D), lambda g, t, o, c: (g, 0, 0)),
                       pl.BlockSpec((ep, cap, LANES), lambda g, t, o, c: (g, 0, 0))],
        ),
        out_shape=[jax.ShapeDtypeStruct((E, cap, D), BF16),
                   jax.ShapeDtypeStruct((E, cap, LANES), F32)],
        compiler_params=_cparams(("parallel", "arbitrary")),
        name="gather",
    )(off_s, cnt_s, x_bf, pos4, gate4)


def _ffn_kernel(xe_ref, wg_ref, wu_ref, wd_ref, gs_ref, ye_ref, acc_sc, *, sub):
    fc = pl.program_id(2)
    rows = xe_ref.shape[1]
    @pl.when(fc == 0)
    def _():
        acc_sc[...] = jnp.zeros(acc_sc.shape, F32)

    wg = wg_ref[...].astype(BF16)
    wu = wu_ref[...].astype(BF16)
    wd = wd_ref[...].astype(BF16)
    for r in range(rows // sub):
        sl = pl.ds(r * sub, sub)
        xt = xe_ref[0, sl, :]
        hg = _dot(xt, wg)
        hu = _dot(xt, wu)
        hid = (hg * jax.nn.sigmoid(hg) * hu).astype(BF16)
        acc_sc[sl, :] = acc_sc[sl, :] + _dot(hid, wd)

    @pl.when(fc == pl.num_programs(2) - 1)
    def _():
        ye_ref[0] = (acc_sc[...] * gs_ref[0][:, :1]).astype(ye_ref.dtype)


def _ffn(xe, gs, w_gate, w_up, w_down, layer, *, halves=2, fcw=256):
    E, cap, D = xe.shape
    F = w_gate.shape[-1]
    rows = cap // halves
    sub = min(256, rows)
    return pl.pallas_call(
        functools.partial(_ffn_kernel, sub=sub),
        grid=(E, halves, F // fcw),
        in_specs=[
            pl.BlockSpec((1, rows, D), lambda e, t, f: (e, t, 0)),
            pl.BlockSpec((None, None, D, fcw), lambda e, t, f: (layer, e, 0, f)),
            pl.BlockSpec((None, None, D, fcw), lambda e, t, f: (layer, e, 0, f)),
            pl.BlockSpec((None, None, fcw, D), lambda e, t, f: (layer, e, f, 0)),
            pl.BlockSpec((1, rows, LANES), lambda e, t, f: (e, t, 0)),
        ],
        out_specs=pl.BlockSpec((1, rows, D), lambda e, t, f: (e, t, 0)),
        out_shape=jax.ShapeDtypeStruct((E, cap, D), BF16),
        scratch_shapes=[pltpu.VMEM((rows, D), F32)],
        compiler_params=_cparams(("parallel", "parallel", "arbitrary")),
        name="expert_ffn",
    )(xe, w_gate, w_up, w_down, gs)


def _combine_kernel(win_s, *refs, eg):
    y_refs = refs[:eg]
    post_ref, x_ref, g_ref, b_ref, xo_ref, xb_ref, acc_sc = refs[eg:]
    tb = pl.program_id(0)
    grp = pl.program_id(1)

    @pl.when(grp == 0)
    def _():
        acc_sc[...] = jnp.zeros(acc_sc.shape, F32)

    pt = post_ref[...].astype(F32)
    lane = lax.broadcasted_iota(I32, pt.shape, 1)
    col = lax.broadcasted_iota(I32, (LANES, GATHER_WIN), 1)
    total = None
    for k in range(eg):
        e = grp * eg + k
        pcol = jnp.sum(jnp.where(lane == e, pt, 0.0), axis=1, keepdims=True).astype(I32)
        hit = ((pcol - win_s[e, tb]) == col).astype(BF16)
        part = _dot(hit, y_refs[k][0])
        total = part if total is None else total + part
    acc_sc[...] = acc_sc[...] + total

    @pl.when(grp == pl.num_programs(1) - 1)
    def _():
        y = _layer_norm(DEEPNORM_ALPHA * x_ref[...] + acc_sc[...], g_ref[...], b_ref[...])
        xo_ref[...] = y
        xb_ref[...] = y.astype(BF16)


def _combine_ln(win_s, ye, pos_t, x, g, b, *, eg=16):
    S, D = x.shape
    E, cap, _ = ye.shape
    nb = S // LANES
    assert E % eg == 0

    def y_spec(k):
        return pl.BlockSpec((pl.Element(1), pl.Element(GATHER_WIN), pl.Element(D)),
                            lambda t, gr, ws: (gr * eg + k, (ws[gr * eg + k, t] // BF16_ROWS) * BF16_ROWS, 0))

    row = lambda t, gr, ws: (t, 0)
    fixed = lambda t, gr, ws: (0, 0)
    return pl.pallas_call(
        functools.partial(_combine_kernel, eg=eg),
        grid_spec=pltpu.PrefetchScalarGridSpec(
            num_scalar_prefetch=1,
            grid=(nb, E // eg),
            in_specs=([y_spec(k) for k in range(eg)]
                      + [pl.BlockSpec((LANES, E), row), pl.BlockSpec((LANES, D), row),
                         pl.BlockSpec((1, D), fixed), pl.BlockSpec((1, D), fixed)]),
            out_specs=[pl.BlockSpec((LANES, D), row), pl.BlockSpec((LANES, D), row)],
            scratch_shapes=[pltpu.VMEM((LANES, D), F32)],
        ),
        out_shape=[jax.ShapeDtypeStruct((S, D), F32), jax.ShapeDtypeStruct((S, D), BF16)],
        compiler_params=_cparams(("parallel", "arbitrary")),
        name="combine_ln",
    )(win_s, *([ye] * eg), pos_t, x, g, b)


def _moe(x, x_bf, logits_t, w_gate, w_up, w_down, layer, g, b):
    S, D = x.shape
    E = logits_t.shape[0]
    nb = S // LANES
    cap = EC_FACTOR * S // E
    pos, gate, off, cnt = _select(logits_t, cap)
    off_s = off[:, 0].reshape(E, nb)
    cnt_s = cnt[:, 0].reshape(E, nb)
    xe, gs = _gather(off_s, cnt_s, x_bf, pos, gate, cap)
    ye = _ffn(xe, gs, w_gate, w_up, w_down, layer)
    pos_t = pos.reshape(E, S).T
    win_s = jnp.minimum((off_s // BF16_ROWS) * BF16_ROWS, cap - GATHER_WIN)
    return _combine_ln(win_s, ye, pos_t, x, g, b)


def _mla_in_kernel(x_ref, w_ref, cos_ref, sin_ref, qn_ref, kvn_ref, cq_ref, ckv_ref, kr_ref):
    h = _dot(x_ref[...], w_ref[...])
    cq_ref[...] = _rms(h[:, :MLA_Q_RANK], qn_ref[...]).astype(BF16)
    ckv_ref[...] = _rms(h[:, MLA_Q_RANK:MLA_Q_RANK + MLA_KV_RANK], kvn_ref[...]).astype(BF16)
    kr = h[:, MLA_Q_RANK + MLA_KV_RANK:]
    kr_ref[...] = _rope(kr, cos_ref[...], sin_ref[...], MLA_ROPE // 4).astype(BF16)


def _mla_in(x_bf, w_bf, cos, sin, qn, kvn, *, tm=1024):
    S, K = x_bf.shape
    N = w_bf.shape[1]
    row = lambda i: (i, 0)
    fixed = lambda i: (0, 0)
    return pl.pallas_call(
        _mla_in_kernel,
        grid=(S // tm,),
        in_specs=[pl.BlockSpec((tm, K), row), pl.BlockSpec((K, N), fixed),
                  pl.BlockSpec((tm, LANES), row), pl.BlockSpec((tm, LANES), row),
                  pl.BlockSpec((1, MLA_Q_RANK), fixed), pl.BlockSpec((1, MLA_KV_RANK), fixed)],
        out_specs=[pl.BlockSpec((tm, MLA_Q_RANK), row), pl.BlockSpec((tm, MLA_KV_RANK), row),
                   pl.BlockSpec((tm, LANES), row)],
        out_shape=[jax.ShapeDtypeStruct((S, MLA_Q_RANK), BF16),
                   jax.ShapeDtypeStruct((S, MLA_KV_RANK), BF16),
                   jax.ShapeDtypeStruct((S, LANES), BF16)],
        compiler_params=_cparams(("parallel",)),
        name="mla_in",
    )(x_bf, w_bf, cos, sin, qn, kvn)


def _mla_q_kernel(c_ref, w_ref, cos_ref, sin_ref, o_ref, *, heads):
    h = _dot(c_ref[...], w_ref[...])
    scale = (MLA_NOPE + MLA_ROPE) ** -0.5 * LOG2E
    for c in range(heads):
        base = c * MLA_QK_PAD
        o_ref[c, :, :MLA_NOPE] = (h[:, base:base + MLA_NOPE] * scale).astype(BF16)
        rp = _rope(h[:, base + MLA_NOPE:base + MLA_QK_PAD], cos_ref[...], sin_ref[...], MLA_ROPE // 4)
        o_ref[c, :, MLA_NOPE:] = (rp * scale).astype(BF16)


def _mla_q(cq, w_bf, cos, sin, *, tm=1024, heads=4):
    S, K = cq.shape
    tn = heads * MLA_QK_PAD
    return pl.pallas_call(
        functools.partial(_mla_q_kernel, heads=heads),
        grid=(S // tm, MLA_HEADS // heads),
        in_specs=[pl.BlockSpec((tm, K), lambda i, j: (i, 0)),
                  pl.BlockSpec((K, tn), lambda i, j: (0, j)),
                  pl.BlockSpec((tm, LANES), lambda i, j: (i, 0)),
                  pl.BlockSpec((tm, LANES), lambda i, j: (i, 0))],
        out_specs=pl.BlockSpec((heads, tm, MLA_QK_PAD), lambda i, j: (j, i, 0)),
        out_shape=jax.ShapeDtypeStruct((MLA_HEADS, S, MLA_QK_PAD), BF16),
        compiler_params=_cparams(("parallel", "parallel")),
        name="mla_q",
    )(cq, w_bf, cos, sin)


def _mla_k_kernel(c_ref, wk_ref, kr_ref, k_ref, *, heads):
    kn = _dot(c_ref[...], wk_ref[...])
    for hh in range(heads):
        k_ref[hh, :, :MLA_NOPE] = kn[:, hh * MLA_NOPE:(hh + 1) * MLA_NOPE].astype(BF16)
        k_ref[hh, :, MLA_NOPE:] = kr_ref[...]


def _mla_k(ckv, wk_bf, kr, *, tm=2048, heads=4):
    S, K = ckv.shape
    return pl.pallas_call(
        functools.partial(_mla_k_kernel, heads=heads),
        grid=(S // tm, MLA_HEADS // heads),
        in_specs=[pl.BlockSpec((tm, K), lambda i, j: (i, 0)),
                  pl.BlockSpec((K, heads * MLA_NOPE), lambda i, j: (0, j)),
                  pl.BlockSpec((tm, LANES), lambda i, j: (i, 0))],
        out_specs=pl.BlockSpec((heads, tm, MLA_QK_PAD), lambda i, j: (j, i, 0)),
        out_shape=jax.ShapeDtypeStruct((MLA_HEADS, S, MLA_QK_PAD), BF16),
        compiler_params=_cparams(("parallel", "parallel")),
        name="mla_k",
    )(ckv, wk_bf, kr)


def kernel(x, ab_w_in, ab_q_norm, ab_k_norm, ab_lambda_q1, ab_lambda_k1, ab_lambda_q2, ab_lambda_k2, ab_subln, ab_w_out, mla_w_in, mla_q_norm, mla_kv_norm, mla_w_uq, mla_w_ukv, mla_w_out, ln_mix_g, ln_mix_b, moe_w_router, moe_w_gate, moe_w_up, moe_w_down, ln_ffn_g, ln_ffn_b):
    B, S, D = x.shape
    assert B == 1 and D == D_MODEL and S % 512 == 0
    xf = x.reshape(S, D)
    x_bf = xf.astype(BF16)
    cos_a, sin_a = _rope_tables(S, HEAD_DIM // 4, HEAD_DIM)
    cos_c, sin_c = _rope_tables(S, MLA_ROPE // 4, LANES)
    row2 = lambda v: v.reshape(1, -1)

    lam_init = 0.8 - 0.6 * math.exp(-0.3 * 0)
    w_in = ab_w_in[0].astype(BF16)
    c_av = A_Q + A_KV
    c_bq = c_av + A_KV
    c_bv = c_bq + 2 * B_QK
    w_qk = jnp.concatenate([w_in[:, :c_av], w_in[:, c_bq:c_bv]], axis=1)
    hq = _proj0(x_bf, w_qk, cos_a, sin_a, row2(ab_q_norm[0]), row2(ab_k_norm[0]))
    vt_a = _values_t(w_in[:, c_av:c_bq], x_bf, name="values_t_gqa")
    vt_b = _values_t(w_in[:, c_bv:], x_bf, name="values_t_diff")
    t_ak = A_Q // HEAD_DIM
    t_bq = t_ak + GQA_KV_HEADS
    t_bk = t_bq + 2 * DIFF_HEADS
    a_out = _flash(hq, hq, vt_a, n_kv=GQA_KV_HEADS, rep=GQA_Q_HEADS // GQA_KV_HEADS,
                   q_tile0=0, k_tile0=t_ak, dv=HEAD_DIM, bq=128, kc=8192, kv_buffers=1, name="gqa_attention")
    b_out = _diff_attention(hq, vt_b, row2(ab_lambda_q1[0]), row2(ab_lambda_k1[0]),
                            row2(ab_lambda_q2[0]), row2(ab_lambda_k2[0]), row2(ab_subln[0]),
                            q_tile0=t_bq, k_tile0=t_bk, lam_init=lam_init)
    w_out = ab_w_out[0].astype(BF16)
    xf, x_bf, logits_t = _wout_ln_router(
        [a_out, b_out], [w_out[:A_Q], w_out[A_Q:]], xf, row2(ln_mix_g[0]), row2(ln_mix_b[0]),
        moe_w_router[0].T)
    xf, x_bf = _moe(xf, x_bf, logits_t, moe_w_gate, moe_w_up, moe_w_down, 0,
                    row2(ln_ffn_g[0]), row2(ln_ffn_b[0]))

    w1 = jnp.pad(mla_w_in[0], ((0, 0), (0, LANES - MLA_ROPE))).astype(BF16)
    cq, ckv, kr = _mla_in(x_bf, w1, cos_c, sin_c, row2(mla_q_norm[0]), row2(mla_kv_norm[0]))
    w_uq = mla_w_uq[0].reshape(MLA_Q_RANK, MLA_HEADS, MLA_NOPE + MLA_ROPE)
    w_uq = jnp.pad(w_uq, ((0, 0), (0, 0), (0, MLA_QK_PAD - MLA_NOPE - MLA_ROPE)))
    w_uq = w_uq.reshape(MLA_Q_RANK, MLA_HEADS * MLA_QK_PAD).astype(BF16)
    w_ukv = mla_w_ukv[0].reshape(MLA_KV_RANK, MLA_HEADS, MLA_NOPE + MLA_V)
    w_uk = w_ukv[:, :, :MLA_NOPE].reshape(MLA_KV_RANK, MLA_HEADS * MLA_NOPE).astype(BF16)
    w_uv = w_ukv[:, :, MLA_NOPE:].reshape(MLA_KV_RANK, MLA_HEADS * MLA_V)
    q_pad = _mla_q(cq, w_uq, cos_c, sin_c)
    k_pad = _mla_k(ckv, w_uk, kr)
    vt1 = _values_t(w_uv, ckv, name="values_t_mla")
    c_out = _flash(q_pad, k_pad, vt1, n_kv=MLA_HEADS, rep=1, q_tile0=0, k_tile0=0, dv=MLA_V,
                   bq=512, kc=8192, kv_buffers=1, name="mla_attention")
    xf, x_bf, logits_t = _wout_ln_router(
        [c_out], [mla_w_out[0].astype(BF16)], xf, row2(ln_mix_g[1]), row2(ln_mix_b[1]),
        moe_w_router[1].T)
    xf, x_bf = _moe(xf, x_bf, logits_t, moe_w_gate, moe_w_up, moe_w_down, 1,
                    row2(ln_ffn_g[1]), row2(ln_ffn_b[1]))
    return xf.reshape(B, S, D)
```

```python
import functools
import math

import jax
import jax.numpy as jnp
from jax import lax
from jax.experimental import pallas as pl
from jax.experimental.pallas import tpu as pltpu

F32 = jnp.float32
BF16 = jnp.bfloat16
I32 = jnp.int32

D_MODEL = 2048
DEPTH = 2
GRID_W = 64
ROPE_THETA = 10000.0
NORM_EPS = 1e-6
LN_EPS = 1e-5

HEAD_DIM = 128
GQA_Q_HEADS = 8
GQA_KV_HEADS = 2
DIFF_HEADS = 4
A_Q = GQA_Q_HEADS * HEAD_DIM
A_KV = GQA_KV_HEADS * HEAD_DIM
B_QK = DIFF_HEADS * 2 * HEAD_DIM
B_V = DIFF_HEADS * 2 * HEAD_DIM

MLA_HEADS = 16
MLA_Q_RANK = 512
MLA_KV_RANK = 512
MLA_NOPE = 128
MLA_ROPE = 64
MLA_V = 128
MLA_QK_PAD = 256

N_EXPERTS = 16
EXPERT_FF = 2048
EC_FACTOR = 2

DEEPNORM_ALPHA = (2 * DEPTH) ** 0.25
LOG2E = math.log2(math.e)

LANES = 128
BF16_ROWS = 16
GATHER_WIN = LANES + BF16_ROWS
GATHER_SMALL = 32
VMEM_LIMIT = 56 * 1024 * 1024


def _cparams(sem, vmem=VMEM_LIMIT):
    return pltpu.CompilerParams(dimension_semantics=sem, vmem_limit_bytes=vmem)


def _dot(a, b):
    return jnp.dot(a, b, preferred_element_type=F32)


def _dot_nt(a, b):
    return lax.dot_general(a, b, (((1,), (1,)), ((), ())), preferred_element_type=F32)


def _rope(y, cos, sin, half):
    n = y.shape[-1]
    lane = lax.broadcasted_iota(I32, y.shape, 1)
    up = pltpu.roll(y, n - half, 1)
    dn = pltpu.roll(y, half, 1)
    partner = jnp.where((lane % (2 * half)) < half, up, dn)
    return y * cos + partner * sin


def _rms(y, w):
    return y * lax.rsqrt(jnp.mean(y * y, axis=-1, keepdims=True) + NORM_EPS) * w


def _layer_norm(y, g, b):
    mu = jnp.mean(y, axis=-1, keepdims=True)
    d = y - mu
    var = jnp.mean(d * d, axis=-1, keepdims=True)
    return d * lax.rsqrt(var + LN_EPS) * g + b


def _rope_tables(S, half, width):
    n_rows = S // GRID_W
    n = 2 * half
    freqs = ROPE_THETA ** (-jnp.arange(0, n, 2, dtype=F32) / n)[None, :]
    parts_c, parts_s = [], []
    for count, along_rows in ((n_rows, True), (GRID_W, False)):
        ang = jnp.arange(count, dtype=F32)[:, None] * freqs
        c = jnp.concatenate([jnp.cos(ang)] * 2, axis=1)
        s = jnp.concatenate([-jnp.sin(ang), jnp.sin(ang)], axis=1)
        shape = (n_rows, GRID_W, n)
        expand = (lambda a: a[:, None, :]) if along_rows else (lambda a: a[None, :, :])
        parts_c.append(jnp.broadcast_to(expand(c), shape).reshape(S, n))
        parts_s.append(jnp.broadcast_to(expand(s), shape).reshape(S, n))
    pad = width - 4 * half
    if pad:
        parts_c.append(jnp.ones((S, pad), F32))
        parts_s.append(jnp.zeros((S, pad), F32))
    return jnp.concatenate(parts_c, axis=1), jnp.concatenate(parts_s, axis=1)


def _proj0_kernel(x_ref, w_ref, cos_ref, sin_ref, qn_ref, kn_ref, o_ref, *, tn):
    j = pl.program_id(1)
    h = _dot(x_ref[...], w_ref[...])
    nsub = tn // HEAD_DIM
    scale = HEAD_DIM ** -0.5 * LOG2E
    k_tile = A_Q // tn
    bq_lo = (A_Q + A_KV) // tn
    bq_hi = bq_lo + B_QK // tn

    def normed(c, w):
        y = _rms(h[:, c * HEAD_DIM:(c + 1) * HEAD_DIM], w)
        return _rope(y, cos_ref[...], sin_ref[...], HEAD_DIM // 4)

    @pl.when(j < k_tile)
    def _():
        for c in range(nsub):
            o_ref[c] = (normed(c, qn_ref[...]) * scale).astype(o_ref.dtype)

    @pl.when(j == k_tile)
    def _():
        for c in range(nsub):
            o_ref[c] = normed(c, kn_ref[...]).astype(o_ref.dtype)

    @pl.when(j > k_tile)
    def _():
        mul = jnp.where((j >= bq_lo) & (j < bq_hi), scale, 1.0).astype(F32)
        for c in range(nsub):
            o_ref[c] = (h[:, c * HEAD_DIM:(c + 1) * HEAD_DIM] * mul).astype(o_ref.dtype)


def _proj0(x_bf, w_bf, cos, sin, qn, kn, *, tm=2048, tn=256):
    S, K = x_bf.shape
    N = w_bf.shape[1]
    assert A_Q % tn == 0 and tn == A_KV and N % tn == 0
    return pl.pallas_call(
        functools.partial(_proj0_kernel, tn=tn),
        grid=(S // tm, N // tn),
        in_specs=[
            pl.BlockSpec((tm, K), lambda i, j: (i, 0)),
            pl.BlockSpec((K, tn), lambda i, j: (0, j)),
            pl.BlockSpec((tm, HEAD_DIM), lambda i, j: (i, 0)),
            pl.BlockSpec((tm, HEAD_DIM), lambda i, j: (i, 0)),
            pl.BlockSpec((1, HEAD_DIM), lambda i, j: (0, 0)),
            pl.BlockSpec((1, HEAD_DIM), lambda i, j: (0, 0)),
        ],
        out_specs=pl.BlockSpec((tn // HEAD_DIM, tm, HEAD_DIM), lambda i, j: (j, i, 0)),
        out_shape=jax.ShapeDtypeStruct((N // HEAD_DIM, S, HEAD_DIM), BF16),
        compiler_params=_cparams(("parallel", "parallel")),
        name="proj0",
    )(x_bf, w_bf, cos, sin, qn, kn)


def _values_t_kernel(wt_ref, x_ref, o_ref):
    o_ref[...] = _dot_nt(wt_ref[...], x_ref[...]).astype(o_ref.dtype)


def _values_t(w, x_bf, *, tm=2048, tn=256, name="values_t"):
    S, K = x_bf.shape
    N = w.shape[1]
    return pl.pallas_call(
        _values_t_kernel,
        grid=(S // tm, N // tn),
        in_specs=[pl.BlockSpec((tn, K), lambda i, j: (j, 0)),
                  pl.BlockSpec((tm, K), lambda i, j: (i, 0))],
        out_specs=pl.BlockSpec((tn, tm), lambda i, j: (j, i)),
        out_shape=jax.ShapeDtypeStruct((N, S), BF16),
        compiler_params=_cparams(("parallel", "parallel")),
        name=name,
    )(w.T.astype(BF16), x_bf)


def _online_softmax_pv(s, vt, m_ref, l_ref, acc_ref, shift=None):
    m_prev = m_ref[...]
    m_chunk = jnp.max(s, axis=0, keepdims=True)
    if shift is None:
        m_new = jnp.maximum(m_prev, m_chunk)
        p = jnp.exp2(s - m_new)
    else:
        m_new = jnp.maximum(m_prev, m_chunk + shift)
        p = jnp.exp2(s - (m_new - shift))
    alpha = jnp.exp2(m_prev - m_new)
    l_ref[...] = alpha * l_ref[...] + jnp.sum(p, axis=0, keepdims=True)
    acc_ref[...] = alpha * acc_ref[...] + _dot(vt, p.astype(BF16))
    m_ref[...] = m_new


def _flash_kernel(q_ref, qnext_ref, k_ref, vt_ref, o_ref, qt_sc, s_sc, m_sc, l_sc, acc_sc, *, kc):
    R, bq, D = q_ref.shape
    S = k_ref.shape[1]
    Dv = vt_ref.shape[0]
    n = S // kc
    assert n % 2 == 0
    cur = pl.program_id(1) % 2
    nxt = 1 - cur

    def scores(c):
        wrap = c >= n
        off = pl.multiple_of(jnp.where(wrap, 0, c) * kc, kc)
        return _dot(k_ref[0, pl.ds(off, kc), :], qt_sc[jnp.where(wrap, nxt, cur)])

    def values_t(c):
        return vt_ref[:, pl.ds(pl.multiple_of(c * kc, kc), kc)]

    @pl.when(pl.program_id(1) == 0)
    def _():
        qt_sc[0] = q_ref[...].reshape(R * bq, D).T
        s_sc[0] = scores(0)

    qt_sc[nxt] = qnext_ref[...].reshape(R * bq, D).T
    m_sc[...] = jnp.full(m_sc.shape, -jnp.inf, F32)
    l_sc[...] = jnp.zeros(l_sc.shape, F32)
    acc_sc[...] = jnp.zeros(acc_sc.shape, F32)

    def body(i, carry):
        c = 2 * i
        s_sc[1] = scores(c + 1)
        _online_softmax_pv(s_sc[0], values_t(c), m_sc, l_sc, acc_sc)
        s_sc[0] = scores(c + 2)
        _online_softmax_pv(s_sc[1], values_t(c + 1), m_sc, l_sc, acc_sc)
        return carry

    lax.fori_loop(0, n // 2, body, 0)
    out = (acc_sc[...] / l_sc[...]).T
    for r in range(R):
        o_ref[:, r * Dv:(r + 1) * Dv] = out[r * bq:(r + 1) * bq].astype(o_ref.dtype)


def _flash(q_arr, k_arr, vt_arr, *, n_kv, rep, q_tile0, k_tile0, dv, bq, kc, kv_buffers, name):
    _, S, D = q_arr.shape
    kc = min(kc, S // 2)
    bq = min(bq, S)
    assert q_tile0 % rep == 0
    N = rep * bq
    rows = dv
    nq = S // bq
    return pl.pallas_call(
        functools.partial(_flash_kernel, kc=kc),
        grid=(n_kv, nq),
        in_specs=[
            pl.BlockSpec((rep, bq, D), lambda g, i: (q_tile0 // rep + g, i, 0)),
            pl.BlockSpec((rep, bq, D), lambda g, i: (q_tile0 // rep + g, jnp.minimum(i + 1, nq - 1), 0)),
            pl.BlockSpec((1, S, D), lambda g, i: (k_tile0 + g, 0, 0), pipeline_mode=pl.Buffered(kv_buffers)),
            pl.BlockSpec((rows, vt_arr.shape[1]), lambda g, i: (g, 0), pipeline_mode=pl.Buffered(kv_buffers)),
        ],
        out_specs=pl.BlockSpec((bq, rep * dv), lambda g, i: (i, g)),
        out_shape=jax.ShapeDtypeStruct((S, n_kv * rep * dv), BF16),
        scratch_shapes=[pltpu.VMEM((2, D, N), BF16),
                        pltpu.VMEM((2, kc, N), F32),
                        pltpu.VMEM((1, N), F32), pltpu.VMEM((1, N), F32), pltpu.VMEM((rows, N), F32)],
        compiler_params=_cparams(("parallel", "arbitrary")),
        name=name,
    )(q_arr, q_arr, k_arr, vt_arr)


def _diff_kernel(q_ref, qnext_ref, k_ref, vt_ref, lq1_ref, lk1_ref, lq2_ref, lk2_ref, subln_ref, o_ref,
                 qt_sc, s_sc, m_sc, l_sc, acc_sc, *, kc, lam_init):
    _, bq, D = q_ref.shape
    S = k_ref.shape[1]
    h = pl.program_id(0)
    q0 = pl.program_id(1) * bq
    slope = jnp.float32(2.0 ** (-8.0 * DIFF_HEADS / DIFF_HEADS))
    for hh in range(DIFF_HEADS - 1):
        slope = jnp.where(h == hh, jnp.float32(2.0 ** (-8.0 * (hh + 1) / DIFF_HEADS)), slope)
    slope2 = slope * LOG2E
    k_loc = lax.broadcasted_iota(I32, (kc, LANES), 0).astype(F32)
    k_s = k_loc * slope2
    q_loc = lax.broadcasted_iota(I32, (1, bq), 1).astype(F32)
    n = S // kc

    def per_lane_group(s, fn):
        return jnp.concatenate([fn(s[:, g * LANES:(g + 1) * LANES], g) for g in range(bq // LANES)], axis=1)
    cur = pl.program_id(1) % 2
    nxt = 1 - cur

    def scores(j, c):
        wrap = c >= n
        off = pl.multiple_of(jnp.where(wrap, 0, c) * kc, kc)
        return _dot(k_ref[j, pl.ds(off, kc), :], qt_sc[jnp.where(wrap, nxt, cur), j])

    @pl.when(pl.program_id(1) == 0)
    def _():
        for j in range(2):
            qt_sc[0, j] = q_ref[j].T
        s_sc[0] = scores(0, 0)

    for j in range(2):
        qt_sc[nxt, j] = qnext_ref[j].T
    m_sc[...] = jnp.full(m_sc.shape, -jnp.inf, F32)
    l_sc[...] = jnp.zeros(l_sc.shape, F32)
    acc_sc[...] = jnp.zeros(acc_sc.shape, F32)

    def chunk_step(c, penalty):
        off = pl.multiple_of(c * kc, kc)
        delta = (off - q0).astype(F32) - q_loc
        vt = vt_ref[:, pl.ds(off, kc)]
        s_sc[1] = scores(1, c)
        s0, shift = penalty(s_sc[0], delta)
        _online_softmax_pv(s0, vt, m_sc.at[0], l_sc.at[0], acc_sc.at[0], shift)
        s_sc[0] = scores(0, c + 1)
        s1, shift = penalty(s_sc[1], delta)
        _online_softmax_pv(s1, vt, m_sc.at[1], l_sc.at[1], acc_sc.at[1], shift)

    def keys_before(c, carry):
        chunk_step(c, lambda s, delta: (per_lane_group(s, lambda t, g: t + k_s), slope2 * delta))
        return carry

    def keys_overlap(c, carry):
        def general(t, g, delta):
            return t - jnp.abs(k_s + slope2 * delta[:, g * LANES:(g + 1) * LANES])
        chunk_step(c, lambda s, delta: (per_lane_group(s, lambda t, g: general(t, g, delta)), None))
        return carry

    def keys_after(c, carry):
        chunk_step(c, lambda s, delta: (per_lane_group(s, lambda t, g: t - k_s), -(slope2 * delta)))
        return carry

    c_lo = q0 // kc
    c_hi = (q0 + bq + kc - 1) // kc
    lax.fori_loop(0, c_lo, keys_before, 0)
    lax.fori_loop(c_lo, c_hi, keys_overlap, 0)
    lax.fori_loop(c_hi, n, keys_after, 0)
    lam = (jnp.exp(jnp.sum(lq1_ref[...] * lk1_ref[...], axis=1, keepdims=True))
           - jnp.exp(jnp.sum(lq2_ref[...] * lk2_ref[...], axis=1, keepdims=True)) + lam_init)
    out = (acc_sc[0] / l_sc[0] - lam * (acc_sc[1] / l_sc[1])).T
    o_ref[...] = (_rms(out, subln_ref[...]) * (1.0 - lam_init)).astype(o_ref.dtype)


def _diff_attention(hq, vt_arr, lq1, lk1, lq2, lk2, subln, *, q_tile0, k_tile0, lam_init,
                    bq=512, kc=4096):
    _, S, D = hq.shape
    kc = min(kc, S)
    bq = min(bq, S)
    rows = 2 * D
    vec = pl.BlockSpec((1, D), lambda h, i: (0, 0))
    nq = S // bq
    return pl.pallas_call(
        functools.partial(_diff_kernel, kc=kc, lam_init=lam_init),
        grid=(DIFF_HEADS, nq),
        in_specs=[
            pl.BlockSpec((2, bq, D), lambda h, i: (q_tile0 // 2 + h, i, 0)),
            pl.BlockSpec((2, bq, D), lambda h, i: (q_tile0 // 2 + h, jnp.minimum(i + 1, nq - 1), 0)),
            pl.BlockSpec((2, S, D), lambda h, i: (k_tile0 // 2 + h, 0, 0), pipeline_mode=pl.Buffered(1)),
            pl.BlockSpec((rows, vt_arr.shape[1]), lambda h, i: (h, 0), pipeline_mode=pl.Buffered(1)),
            vec, vec, vec, vec,
            pl.BlockSpec((1, 2 * D), lambda h, i: (0, 0)),
        ],
        out_specs=pl.BlockSpec((bq, 2 * D), lambda h, i: (i, h)),
        out_shape=jax.ShapeDtypeStruct((S, DIFF_HEADS * 2 * D), BF16),
        scratch_shapes=[pltpu.VMEM((2, 2, D, bq), BF16), pltpu.VMEM((2, kc, bq), F32),
                        pltpu.VMEM((2, 1, bq), F32), pltpu.VMEM((2, 1, bq), F32),
                        pltpu.VMEM((2, rows, bq), F32)],
        compiler_params=_cparams(("parallel", "arbitrary")),
        name="diff_attention",
    )(hq, hq, hq, vt_arr, lq1, lk1, lq2, lk2, subln)


def _wout_kernel(*refs, n_in):
    a_refs = refs[:n_in]
    w_refs = refs[n_in:2 * n_in]
    x_ref, g_ref, b_ref, wr_ref, xo_ref, xb_ref, lg_ref = refs[2 * n_in:]
    mix = _dot(a_refs[0][...], w_refs[0][...])
    for a, w in zip(a_refs[1:], w_refs[1:]):
        mix = mix + _dot(a[...], w[...])
    y = _layer_norm(DEEPNORM_ALPHA * x_ref[...] + mix, g_ref[...], b_ref[...])
    xo_ref[...] = y
    y_hi = y.astype(BF16)
    xb_ref[...] = y_hi
    y_lo = (y - y_hi.astype(F32)).astype(BF16)
    wr = wr_ref[...]
    w_hi = wr.astype(BF16)
    w_lo = (wr - w_hi.astype(F32)).astype(BF16)
    E = wr.shape[0]
    both = _dot_nt(jnp.concatenate([w_hi, w_lo], axis=0), y_hi)
    lg_ref[...] = both[:E] + both[E:] + _dot_nt(w_hi, y_lo)


def _wout_ln_router(a_list, w_list, x, g, b, wr_t, *, tm=512):
    S, D = x.shape
    n_in = len(a_list)
    E = wr_t.shape[0]
    in_specs = ([pl.BlockSpec((tm, a.shape[1]), lambda i: (i, 0)) for a in a_list]
                + [pl.BlockSpec(w.shape, lambda i: (0, 0)) for w in w_list]
                + [pl.BlockSpec((tm, D), lambda i: (i, 0)),
                   pl.BlockSpec((1, D), lambda i: (0, 0)),
                   pl.BlockSpec((1, D), lambda i: (0, 0)),
                   pl.BlockSpec((E, D), lambda i: (0, 0))])
    return pl.pallas_call(
        functools.partial(_wout_kernel, n_in=n_in),
        grid=(S // tm,),
        in_specs=in_specs,
        out_specs=[pl.BlockSpec((tm, D), lambda i: (i, 0)),
                   pl.BlockSpec((tm, D), lambda i: (i, 0)),
                   pl.BlockSpec((E, tm), lambda i: (0, i))],
        out_shape=[jax.ShapeDtypeStruct((S, D), F32),
                   jax.ShapeDtypeStruct((S, D), BF16),
                   jax.ShapeDtypeStruct((E, S), F32)],
        compiler_params=_cparams(("parallel",)),
        name="wout_ln_router",
    )(*a_list, *w_list, x, g, b, wr_t)


def _select_kernel(lg_ref, pos_ref, gate_ref, off_ref, cnt_ref, *, E, nb, cap):
    lg = lg_ref[...].reshape(E, nb, LANES)
    mx = jnp.max(lg, axis=0, keepdims=True)
    ex = jnp.exp(lg - mx)
    aff = ex / jnp.sum(ex, axis=0, keepdims=True)
    bits = lax.bitcast_convert_type(aff, I32)

    def count(msk):
        c = jnp.sum(msk.astype(F32), axis=1, keepdims=True)
        return jnp.sum(c, axis=2, keepdims=True)

    def search(i, thr):
        cand = thr | lax.shift_left(jnp.int32(1), 30 - i)
        return jnp.where(count(bits >= cand) >= cap, cand, thr)

    thr = lax.fori_loop(0, 31, search, jnp.zeros((E, 1, 1), I32))
    gt = bits > thr
    eq = bits == thr
    need = cap - count(gt)

    col = lax.broadcasted_iota(I32, (LANES, LANES), 1)
    rw = lax.broadcasted_iota(I32, (LANES, LANES), 0)
    upper = (rw < col).astype(BF16)
    ones = jnp.ones((LANES, LANES), BF16)
    lower = (lax.broadcasted_iota(I32, (nb, nb), 1) < lax.broadcasted_iota(I32, (nb, nb), 0)).astype(BF16)

    def prefix(msk):
        m2 = jnp.where(msk, 1.0, 0.0).reshape(E * nb, LANES).astype(BF16)
        within = _dot(m2, upper).reshape(E, nb, LANES)
        tot = _dot(m2, ones).reshape(E, nb, LANES)
        offs = jnp.stack([_dot(lower, tot[e].astype(BF16)) for e in range(E)], axis=0)
        return within + offs, offs, tot

    eq_rank, _, _ = prefix(eq)
    sel = gt | (eq & (eq_rank < need))
    pos, offs, tot = prefix(sel)
    pos_ref[...] = jnp.where(sel, pos, -1.0).astype(I32).reshape(E * nb, LANES)
    gate_ref[...] = jnp.where(sel, aff, 0.0).reshape(E * nb, LANES)
    off_ref[...] = offs.astype(I32).reshape(E * nb, LANES)
    cnt_ref[...] = tot.astype(I32).reshape(E * nb, LANES)


def _select(logits_t, cap):
    E, S = logits_t.shape
    nb = S // LANES
    shp = (E * nb, LANES)
    full = pl.BlockSpec(shp, lambda: (0, 0))
    return pl.pallas_call(
        functools.partial(_select_kernel, E=E, nb=nb, cap=cap),
        in_specs=[full],
        out_specs=[full, full, full, full],
        out_shape=[jax.ShapeDtypeStruct(shp, I32), jax.ShapeDtypeStruct(shp, F32),
                   jax.ShapeDtypeStruct(shp, I32), jax.ShapeDtypeStruct(shp, I32)],
        compiler_params=pltpu.CompilerParams(vmem_limit_bytes=VMEM_LIMIT),
        name="select",
    )(logits_t.reshape(shp))


def _gather_kernel(off_s, cnt_s, x_ref, pos_ref, gate_ref, xe_ref, gs_ref, *, cap, tbs, ep):
    grp = pl.program_id(0)
    t = pl.program_id(1)

    @pl.when(t == 0)
    def _():
        xe_ref[...] = jnp.zeros(xe_ref.shape, xe_ref.dtype)
        gs_ref[...] = jnp.zeros(gs_ref.shape, gs_ref.dtype)

    def place(k, u, e, tb, width):
        base = jnp.minimum((off_s[e, tb] // BF16_ROWS) * BF16_ROWS, cap - width)
        base = pl.multiple_of(base, BF16_ROWS)
        slot = base + lax.broadcasted_iota(I32, (width, LANES), 0)
        hit = pos_ref[k, u] == slot
        rows = _dot(hit.astype(BF16), x_ref[pl.ds(u * LANES, LANES), :])
        win = pl.ds(base, width)
        xe_ref[k, win, :] = xe_ref[k, win, :] + rows.astype(xe_ref.dtype)
        g = jnp.sum(jnp.where(hit, gate_ref[k, u], 0.0), axis=1, keepdims=True)
        gs_ref[k, win, :] = gs_ref[k, win, :] + jnp.broadcast_to(g, (width, LANES))

    for u in range(tbs):
        for k in range(ep):
            e = grp * ep + k
            tb = t * tbs + u
            cnt = cnt_s[e, tb]

            @pl.when((cnt > 0) & (cnt <= GATHER_SMALL))
            def _():
                place(k, u, e, tb, GATHER_SMALL + BF16_ROWS)

            @pl.when(cnt > GATHER_SMALL)
            def _():
                place(k, u, e, tb, GATHER_WIN)


def _gather(off_s, cnt_s, x_bf, pos, gate, cap, *, tbs=8, ep=4):
    S, D = x_bf.shape
    E, nb = off_s.shape
    assert cap >= GATHER_WIN and cap % BF16_ROWS == 0 and nb % tbs == 0 and E % ep == 0
    pos4 = pos.reshape(E, nb, 1, LANES)
    gate4 = gate.reshape(E, nb, 1, LANES)
    return pl.pallas_call(
        functools.partial(_gather_kernel, cap=cap, tbs=tbs, ep=ep),
        grid_spec=pltpu.PrefetchScalarGridSpec(
            num_scalar_prefetch=2,
            grid=(E // ep, nb // tbs),
            in_specs=[
                pl.BlockSpec((tbs * LANES, D), lambda g, t, o, c: (t, 0)),
                pl.BlockSpec((ep, tbs, 1, LANES), lambda g, t, o, c: (g, t, 0, 0)),
                pl.BlockSpec((ep, tbs, 1, LANES), lambda g, t, o, c: (g, t, 0, 0)),
            ],
            out_specs=[pl.BlockSpec((ep, cap, D), lambda g, t, o, c: (g, 0, 0), pipeline_mode=pl.Buffered(1)),
                       pl.BlockSpec((ep, cap, LANES), lambda g, t, o, c: (g, 0, 0),
                                    pipeline_mode=pl.Buffered(1))],
        ),
        out_shape=[jax.ShapeDtypeStruct((E, cap, D), BF16),
                   jax.ShapeDtypeStruct((E, cap, LANES), F32)],
        compiler_params=_cparams(("parallel", "arbitrary")),
        name="gather",
    )(off_s, cnt_s, x_bf, pos4, gate4)


def _ffn_kernel(xe_ref, wg_ref, wu_ref, wd_ref, gs_ref, ye_ref, acc_sc, *, sub):
    fc = pl.program_id(2)
    rows = xe_ref.shape[1]
    @pl.when(fc == 0)
    def _():
        acc_sc[...] = jnp.zeros(acc_sc.shape, F32)

    wg = wg_ref[...].astype(BF16)
    wu = wu_ref[...].astype(BF16)
    wd = wd_ref[...].astype(BF16)
    for r in range(rows // sub):
        sl = pl.ds(r * sub, sub)
        xt = xe_ref[0, sl, :]
        hg = _dot(xt, wg)
        hu = _dot(xt, wu)
        hid = (hg * jax.nn.sigmoid(hg) * hu).astype(BF16)
        acc_sc[sl, :] = acc_sc[sl, :] + _dot(hid, wd)

    @pl.when(fc == pl.num_programs(2) - 1)
    def _():
        ye_ref[0] = (acc_sc[...] * gs_ref[0][:, :1]).astype(ye_ref.dtype)


def _ffn(xe, gs, w_gate, w_up, w_down, layer, *, halves=2, fcw=256):
    E, cap, D = xe.shape
    F = w_gate.shape[-1]
    rows = cap // halves
    sub = min(256, rows)
    return pl.pallas_call(
        functools.partial(_ffn_kernel, sub=sub),
        grid=(E, halves, F // fcw),
        in_specs=[
            pl.BlockSpec((1, rows, D), lambda e, t, f: (e, t, 0)),
            pl.BlockSpec((None, None, D, fcw), lambda e, t, f: (layer, e, 0, f)),
            pl.BlockSpec((None, None, D, fcw), lambda e, t, f: (layer, e, 0, f)),
            pl.BlockSpec((None, None, fcw, D), lambda e, t, f: (layer, e, f, 0)),
            pl.BlockSpec((1, rows, LANES), lambda e, t, f: (e, t, 0)),
        ],
        out_specs=pl.BlockSpec((1, rows, D), lambda e, t, f: (e, t, 0)),
        out_shape=jax.ShapeDtypeStruct((E, cap, D), BF16),
        scratch_shapes=[pltpu.VMEM((rows, D), F32)],
        compiler_params=_cparams(("parallel", "parallel", "arbitrary")),
        name="expert_ffn",
    )(xe, w_gate, w_up, w_down, gs)


def _combine_kernel(win_s, *refs, eg):
    y_refs = refs[:eg]
    post_ref, x_ref, g_ref, b_ref, xo_ref, xb_ref, acc_sc = refs[eg:]
    tb = pl.program_id(0)
    grp = pl.program_id(1)

    @pl.when(grp == 0)
    def _():
        acc_sc[...] = jnp.zeros(acc_sc.shape, F32)

    pt = post_ref[...].astype(F32)
    lane = lax.broadcasted_iota(I32, pt.shape, 1)
    col = lax.broadcasted_iota(I32, (LANES, GATHER_WIN), 1)
    total = None
    for k in range(eg):
        e = grp * eg + k
        pcol = jnp.sum(jnp.where(lane == e, pt, 0.0), axis=1, keepdims=True).astype(I32)
        hit = ((pcol - win_s[e, tb]) == col).astype(BF16)
        part = _dot(hit, y_refs[k][0])
        total = part if total is None else total + part
    acc_sc[...] = acc_sc[...] + total

    @pl.when(grp == pl.num_programs(1) - 1)
    def _():
        y = _layer_norm(DEEPNORM_ALPHA * x_ref[...] + acc_sc[...], g_ref[...], b_ref[...])
        xo_ref[...] = y
        xb_ref[...] = y.astype(BF16)


def _combine_ln(win_s, ye, pos_t, x, g, b, *, eg=16):
    S, D = x.shape
    E, cap, _ = ye.shape
    nb = S // LANES
    assert E % eg == 0

    def y_spec(k):
        return pl.BlockSpec((pl.Element(1), pl.Element(GATHER_WIN), pl.Element(D)),
                            lambda t, gr, ws: (gr * eg + k, (ws[gr * eg + k, t] // BF16_ROWS) * BF16_ROWS, 0))

    row = lambda t, gr, ws: (t, 0)
    fixed = lambda t, gr, ws: (0, 0)
    return pl.pallas_call(
        functools.partial(_combine_kernel, eg=eg),
        grid_spec=pltpu.PrefetchScalarGridSpec(
            num_scalar_prefetch=1,
            grid=(nb, E // eg),
            in_specs=([y_spec(k) for k in range(eg)]
                      + [pl.BlockSpec((LANES, E), row), pl.BlockSpec((LANES, D), row),
                         pl.BlockSpec((1, D), fixed), pl.BlockSpec((1, D), fixed)]),
            out_specs=[pl.BlockSpec((LANES, D), row), pl.BlockSpec((LANES, D), row)],
            scratch_shapes=[pltpu.VMEM((LANES, D), F32)],
        ),
        out_shape=[jax.ShapeDtypeStruct((S, D), F32), jax.ShapeDtypeStruct((S, D), BF16)],
        compiler_params=_cparams(("parallel", "arbitrary")),
        name="combine_ln",
    )(win_s, *([ye] * eg), pos_t, x, g, b)


def _moe(x, x_bf, logits_t, w_gate, w_up, w_down, layer, g, b):
    S, D = x.shape
    E = logits_t.shape[0]
    nb = S // LANES
    cap = EC_FACTOR * S // E
    pos, gate, off, cnt = _select(logits_t, cap)
    off_s = off[:, 0].reshape(E, nb)
    cnt_s = cnt[:, 0].reshape(E, nb)
    xe, gs = _gather(off_s, cnt_s, x_bf, pos, gate, cap)
    ye = _ffn(xe, gs, w_gate, w_up, w_down, layer)
    pos_t = pos.reshape(E, S).T
    win_s = jnp.minimum((off_s // BF16_ROWS) * BF16_ROWS, cap - GATHER_WIN)
    return _combine_ln(win_s, ye, pos_t, x, g, b)


def _mla_in_kernel(x_ref, w_ref, cos_ref, sin_ref, qn_ref, kvn_ref, cq_ref, ckv_ref, kr_ref):
    h = _dot(x_ref[...], w_ref[...])
    cq_ref[...] = _rms(h[:, :MLA_Q_RANK], qn_ref[...]).astype(BF16)
    ckv_ref[...] = _rms(h[:, MLA_Q_RANK:MLA_Q_RANK + MLA_KV_RANK], kvn_ref[...]).astype(BF16)
    kr = h[:, MLA_Q_RANK + MLA_KV_RANK:]
    kr_ref[...] = _rope(kr, cos_ref[...], sin_ref[...], MLA_ROPE // 4).astype(BF16)


def _mla_in(x_bf, w_bf, cos, sin, qn, kvn, *, tm=1024):
    S, K = x_bf.shape
    N = w_bf.shape[1]
    row = lambda i: (i, 0)
    fixed = lambda i: (0, 0)
    return pl.pallas_call(
        _mla_in_kernel,
        grid=(S // tm,),
        in_specs=[pl.BlockSpec((tm, K), row), pl.BlockSpec((K, N), fixed),
                  pl.BlockSpec((tm, LANES), row), pl.BlockSpec((tm, LANES), row),
                  pl.BlockSpec((1, MLA_Q_RANK), fixed), pl.BlockSpec((1, MLA_KV_RANK), fixed)],
        out_specs=[pl.BlockSpec((tm, MLA_Q_RANK), row), pl.BlockSpec((tm, MLA_KV_RANK), row),
                   pl.BlockSpec((tm, LANES), row)],
        out_shape=[jax.ShapeDtypeStruct((S, MLA_Q_RANK), BF16),
                   jax.ShapeDtypeStruct((S, MLA_KV_RANK), BF16),
                   jax.ShapeDtypeStruct((S, LANES), BF16)],
        compiler_params=_cparams(("parallel",)),
        name="mla_in",
    )(x_bf, w_bf, cos, sin, qn, kvn)


def _mla_q_kernel(c_ref, w_ref, cos_ref, sin_ref, o_ref, *, heads):
    h = _dot(c_ref[...], w_ref[...])
    scale = (MLA_NOPE + MLA_ROPE) ** -0.5 * LOG2E
    for c in range(heads):
        base = c * MLA_QK_PAD
        o_ref[c, :, :MLA_NOPE] = (h[:, base:base + MLA_NOPE] * scale).astype(BF16)
        rp = _rope(h[:, base + MLA_NOPE:base + MLA_QK_PAD], cos_ref[...], sin_ref[...], MLA_ROPE // 4)
        o_ref[c, :, MLA_NOPE:] = (rp * scale).astype(BF16)


def _mla_q(cq, w_bf, cos, sin, *, tm=1024, heads=4):
    S, K = cq.shape
    tn = heads * MLA_QK_PAD
    return pl.pallas_call(
        functools.partial(_mla_q_kernel, heads=heads),
        grid=(S // tm, MLA_HEADS // heads),
        in_specs=[pl.BlockSpec((tm, K), lambda i, j: (i, 0)),
                  pl.BlockSpec((K, tn), lambda i, j: (0, j)),
                  pl.BlockSpec((tm, LANES), lambda i, j: (i, 0)),
                  pl.BlockSpec((tm, LANES), lambda i, j: (i, 0))],
        out_specs=pl.BlockSpec((heads, tm, MLA_QK_PAD), lambda i, j: (j, i, 0)),
        out_shape=jax.ShapeDtypeStruct((MLA_HEADS, S, MLA_QK_PAD), BF16),
        compiler_params=_cparams(("parallel", "parallel")),
        name="mla_q",
    )(cq, w_bf, cos, sin)


def _mla_k_kernel(c_ref, wk_ref, kr_ref, k_ref, *, heads):
    kn = _dot(c_ref[...], wk_ref[...])
    for hh in range(heads):
        k_ref[hh, :, :MLA_NOPE] = kn[:, hh * MLA_NOPE:(hh + 1) * MLA_NOPE].astype(BF16)
        k_ref[hh, :, MLA_NOPE:] = kr_ref[...]


def _mla_k(ckv, wk_bf, kr, *, tm=2048, heads=4):
    S, K = ckv.shape
    return pl.pallas_call(
        functools.partial(_mla_k_kernel, heads=heads),
        grid=(S // tm, MLA_HEADS // heads),
        in_specs=[pl.BlockSpec((tm, K), lambda i, j: (i, 0)),
                  pl.BlockSpec((K, heads * MLA_NOPE), lambda i, j: (0, j)),
                  pl.BlockSpec((tm, LANES), lambda i, j: (i, 0))],
        out_specs=pl.BlockSpec((heads, tm, MLA_QK_PAD), lambda i, j: (j, i, 0)),
        out_shape=jax.ShapeDtypeStruct((MLA_HEADS, S, MLA_QK_PAD), BF16),
        compiler_params=_cparams(("parallel", "parallel")),
        name="mla_k",
    )(ckv, wk_bf, kr)


def kernel(x, ab_w_in, ab_q_norm, ab_k_norm, ab_lambda_q1, ab_lambda_k1, ab_lambda_q2, ab_lambda_k2, ab_subln, ab_w_out, mla_w_in, mla_q_norm, mla_kv_norm, mla_w_uq, mla_w_ukv, mla_w_out, ln_mix_g, ln_mix_b, moe_w_router, moe_w_gate, moe_w_up, moe_w_down, ln_ffn_g, ln_ffn_b):
    B, S, D = x.shape
    assert B == 1 and D == D_MODEL and S % 512 == 0
    xf = x.reshape(S, D)
    x_bf = xf.astype(BF16)
    cos_a, sin_a = _rope_tables(S, HEAD_DIM // 4, HEAD_DIM)
    cos_c, sin_c = _rope_tables(S, MLA_ROPE // 4, LANES)
    row2 = lambda v: v.reshape(1, -1)

    lam_init = 0.8 - 0.6 * math.exp(-0.3 * 0)
    w_in = ab_w_in[0].astype(BF16)
    c_av = A_Q + A_KV
    c_bq = c_av + A_KV
    c_bv = c_bq + 2 * B_QK
    w_qk = jnp.concatenate([w_in[:, :c_av], w_in[:, c_bq:c_bv]], axis=1)
    hq = _proj0(x_bf, w_qk, cos_a, sin_a, row2(ab_q_norm[0]), row2(ab_k_norm[0]))
    vt_a = _values_t(w_in[:, c_av:c_bq], x_bf, name="values_t_gqa")
    vt_b = _values_t(w_in[:, c_bv:], x_bf, name="values_t_diff")
    t_ak = A_Q // HEAD_DIM
    t_bq = t_ak + GQA_KV_HEADS
    t_bk = t_bq + 2 * DIFF_HEADS
    a_out = _flash(hq, hq, vt_a, n_kv=GQA_KV_HEADS, rep=GQA_Q_HEADS // GQA_KV_HEADS,
                   q_tile0=0, k_tile0=t_ak, dv=HEAD_DIM, bq=128, kc=8192, kv_buffers=1, name="gqa_attention")
    b_out = _diff_attention(hq, vt_b, row2(ab_lambda_q1[0]), row2(ab_lambda_k1[0]),
                            row2(ab_lambda_q2[0]), row2(ab_lambda_k2[0]), row2(ab_subln[0]),
                            q_tile0=t_bq, k_tile0=t_bk, lam_init=lam_init)
    w_out = ab_w_out[0].astype(BF16)
    xf, x_bf, logits_t = _wout_ln_router(
        [a_out, b_out], [w_out[:A_Q], w_out[A_Q:]], xf, row2(ln_mix_g[0]), row2(ln_mix_b[0]),
        moe_w_router[0].T)
    xf, x_bf = _moe(xf, x_bf, logits_t, moe_w_gate, moe_w_up, moe_w_down, 0,
                    row2(ln_ffn_g[0]), row2(ln_ffn_b[0]))

    w1 = jnp.pad(mla_w_in[0], ((0, 0), (0, LANES - MLA_ROPE))).astype(BF16)
    cq, ckv, kr = _mla_in(x_bf, w1, cos_c, sin_c, row2(mla_q_norm[0]), row2(mla_kv_norm[0]))
    w_uq = mla_w_uq[0].reshape(MLA_Q_RANK, MLA_HEADS, MLA_NOPE + MLA_ROPE)
    w_uq = jnp.pad(w_uq, ((0, 0), (0, 0), (0, MLA_QK_PAD - MLA_NOPE - MLA_ROPE)))
    w_uq = w_uq.reshape(MLA_Q_RANK, MLA_HEADS * MLA_QK_PAD).astype(BF16)
    w_ukv = mla_w_ukv[0].reshape(MLA_KV_RANK, MLA_HEADS, MLA_NOPE + MLA_V)
    w_uk = w_ukv[:, :, :MLA_NOPE].reshape(MLA_KV_RANK, MLA_HEADS * MLA_NOPE).astype(BF16)
    w_uv = w_ukv[:, :, MLA_NOPE:].reshape(MLA_KV_RANK, MLA_HEADS * MLA_V)
    q_pad = _mla_q(cq, w_uq, cos_c, sin_c)
    k_pad = _mla_k(ckv, w_uk, kr)
    vt1 = _values_t(w_uv, ckv, name="values_t_mla")
    c_out = _flash(q_pad, k_pad, vt1, n_kv=MLA_HEADS, rep=1, q_tile0=0, k_tile0=0, dv=MLA_V,
                   bq=512, kc=8192, kv_buffers=1, name="mla_attention")
    xf, x_bf, logits_t = _wout_ln_router(
        [c_out], [mla_w_out[0].astype(BF16)], xf, row2(ln_mix_g[1]), row2(ln_mix_b[1]),
        moe_w_router[1].T)
    xf, x_bf = _moe(xf, x_bf, logits_t, moe_w_gate, moe_w_up, moe_w_down, 1,
                    row2(ln_ffn_g[1]), row2(ln_ffn_b[1]))
    return xf.reshape(B, S, D)
```
